```python
import jax, jax.numpy as jnp
from jax import lax
import numpy as np

D_MODEL = 1024
BATCH = 8
SEQ = 4096
DEPTH = 4

NSA_HEADS = 8
NSA_KV_GROUPS = 2
NSA_HEAD_DIM = 64
NSA_BRANCHES = 3
CMP_BLOCK = 32
CMP_STRIDE = 16
CMP_HIDDEN = 256
SEL_BLOCK = 64
N_SELECT = 16
N_LOCAL = 2
WINDOW = 512
NSA_Q_CHUNK = 64

MLA_HEADS = 8
MLA_NOPE_DIM = 64
MLA_ROPE_DIM = 32
MLA_V_DIM = 64
Q_LORA_RANK = 256
KV_LORA_RANK = 128
ATTN_BLOCK = 128

D_FF = 4 * D_MODEL
ROPE_THETA = 10000.0
NORM_EPS = 1e-6
NEG_INF = -1e30
FORCE_BONUS = 1e4

NSA_WIDTH = NSA_HEADS * NSA_HEAD_DIM
NSA_KV_WIDTH = NSA_KV_GROUPS * NSA_HEAD_DIM
MLA_WIDTH = MLA_HEADS * MLA_V_DIM
MIX_WIDTH = NSA_WIDTH + MLA_WIDTH
IN_SIZES = (NSA_WIDTH, NSA_KV_WIDTH, NSA_KV_WIDTH, NSA_KV_WIDTH, NSA_KV_WIDTH,
            NSA_KV_WIDTH, NSA_KV_WIDTH, NSA_BRANCHES * NSA_HEADS,
            Q_LORA_RANK, KV_LORA_RANK, MLA_ROPE_DIM)
IN_WIDTH = sum(IN_SIZES)

kernel_name = "hymba_nsa_mla_hybrid_trunk"


def _rms_norm(x, gain):
    xf = x.astype(jnp.float32)
    y = xf * lax.rsqrt(jnp.mean(xf * xf, axis=-1, keepdims=True) + NORM_EPS)
    return (y * gain.astype(jnp.float32)).astype(x.dtype)


def _rope_tables(seq, dim, dtype):
    inv_freq = 1.0 / (ROPE_THETA ** (jnp.arange(0, dim, 2, dtype=jnp.float32) / dim))
    ang = jnp.arange(seq, dtype=jnp.float32)[:, None] * inv_freq[None, :]
    ang = jnp.concatenate([ang, ang], axis=-1)
    return jnp.cos(ang).astype(dtype), jnp.sin(ang).astype(dtype)


def _rope(x, cos, sin):
    half = x.shape[-1] // 2
    rot = jnp.concatenate([-x[..., half:], x[..., :half]], axis=-1)
    return x * cos[:, None, :] + rot * sin[:, None, :]


def _masked_softmax(s, mask):
    s = jnp.where(mask, s, NEG_INF)
    m = jnp.max(s, axis=-1, keepdims=True)
    e = jnp.where(mask, jnp.exp(s - m), 0.0)
    return e / jnp.maximum(jnp.sum(e, axis=-1, keepdims=True), 1e-30)


def _heads(t, n):
    b, s, w = t.shape
    return t.reshape(b, s, n, w // n)


def _compress(kv, pos_emb, w1, w2):
    B, S, G, dk = kv.shape
    ratio = CMP_BLOCK // CMP_STRIDE
    n_cmp = S // CMP_STRIDE - ratio + 1
    chunks = kv.reshape(B, S // CMP_STRIDE, CMP_STRIDE, G, dk)
    blocks = jnp.concatenate([chunks[:, i:i + n_cmp] for i in range(ratio)], axis=2)
    blocks = blocks + pos_emb[None, None, :, None, :]
    flat = jnp.swapaxes(blocks, 2, 3).reshape(B, n_cmp, G, CMP_BLOCK * dk)
    return jax.nn.silu(flat @ w1) @ w2


def _nsa(q, k_cmp, v_cmp, k_sel, v_sel, k_win, v_win, gates):
    B, S, H, dk = q.shape
    G = k_sel.shape[2]
    R = H // G
    n_cmp = k_cmp.shape[1]
    n_blk = S // SEL_BLOCK
    n_sel = min(N_SELECT, n_blk)
    QC = NSA_Q_CHUNK
    scale = dk ** -0.5
    qg = q.reshape(B, S, G, R, dk)
    gg = gates.reshape(B, S, G, R, NSA_BRANCHES)
    k_blocks = k_sel.reshape(B, n_blk, SEL_BLOCK, G, dk).transpose(0, 3, 1, 2, 4)
    v_blocks = v_sel.reshape(B, n_blk, SEL_BLOCK, G, dk).transpose(0, 3, 1, 2, 4)
    pad = ((0, 0), (WINDOW, 0), (0, 0), (0, 0))
    k_win_pad = jnp.pad(k_win, pad)
    v_win_pad = jnp.pad(v_win, pad)
    cmp_end = jnp.arange(n_cmp) * CMP_STRIDE + CMP_BLOCK - 1
    blk_ids = jnp.arange(n_blk)
    a = SEL_BLOCK // CMP_STRIDE
    b = CMP_BLOCK // CMP_STRIDE
    overlap = np.convolve(np.ones(a), np.ones(b))
    bi = jnp.arange(B)[:, None, None, None]
    gi = jnp.arange(G)[None, :, None, None]

    def chunk(c):
        s0 = c * QC
        t = s0 + jnp.arange(QC)
        qc = lax.dynamic_slice_in_dim(qg, s0, QC, axis=1)
        gc = lax.dynamic_slice_in_dim(gg, s0, QC, axis=1)
        s_cmp = jnp.einsum('bqgrd,bngd->bgrqn', qc, k_cmp, preferred_element_type=jnp.float32) * scale
        p_cmp = _masked_softmax(s_cmp, cmp_end[None, :] <= t[:, None])
        o_cmp = jnp.einsum('bgrqn,bngd->bqgrd', p_cmp.astype(v_cmp.dtype), v_cmp)
        imp = jnp.pad(p_cmp.sum(axis=2), ((0, 0), (0, 0), (0, 0), (b - 1, b - 1)))
        imp_blk = sum(float(overlap[j]) * imp[..., j:j + a * (n_blk - 1) + 1:a] for j in range(a + b - 1))
        cur = t // SEL_BLOCK
        valid = blk_ids[None, :] <= cur[:, None]
        forced = valid & ((blk_ids[None, :] == 0) | (blk_ids[None, :] > cur[:, None] - N_LOCAL))
        score = jnp.where(forced, FORCE_BONUS, jnp.where(valid, imp_blk, -1.0))
        _, idx = lax.top_k(score, n_sel)
        k_g = k_blocks[bi, gi, idx]
        v_g = v_blocks[bi, gi, idx]
        key_pos = idx[..., None] * SEL_BLOCK + jnp.arange(SEL_BLOCK)
        m_sel = (key_pos <= t[:, None, None]).reshape(B, G, 1, QC, n_sel * SEL_BLOCK)
        s_sel = jnp.einsum('bqgrd,bgqnkd->bgrqnk', qc, k_g, preferred_element_type=jnp.float32) * scale
        p_sel = _masked_softmax(s_sel.reshape(B, G, R, QC, n_sel * SEL_BLOCK), m_sel)
        o_sel = jnp.einsum('bgrqj,bgqjd->bqgrd', p_sel.astype(v_g.dtype),
                           v_g.reshape(B, G, QC, n_sel * SEL_BLOCK, dk))
        kw = lax.dynamic_slice_in_dim(k_win_pad, s0, WINDOW + QC, axis=1)
        vw = lax.dynamic_slice_in_dim(v_win_pad, s0, WINDOW + QC, axis=1)
        win_pos = s0 - WINDOW + jnp.arange(WINDOW + QC)
        m_win = ((win_pos[None, :] <= t[:, None]) & (win_pos[None, :] > t[:, None] - WINDOW)
                 & (win_pos[None, :] >= 0))
        s_win = jnp.einsum('bqgrd,bkgd->bgrqk', qc, kw, preferred_element_type=jnp.float32) * scale
        p_win = _masked_softmax(s_win, m_win)
        o_win = jnp.einsum('bgrqk,bkgd->bqgrd', p_win.astype(vw.dtype), vw)
        return gc[..., 0:1] * o_cmp + gc[..., 1:2] * o_sel + gc[..., 2:3] * o_win

    out = lax.map(chunk, jnp.arange(S // QC))
    return jnp.swapaxes(out, 0, 1).reshape(B, S, H * dk)


def _causal_attention(q, k, v):
    B, S, H, dk = q.shape
    dv = v.shape[-1]
    scale = dk ** -0.5
    kpos = jnp.arange(S)

    def block(c):
        s0 = c * ATTN_BLOCK
        qb = lax.dynamic_slice_in_dim(q, s0, ATTN_BLOCK, axis=1)
        s = jnp.einsum('bqhd,bkhd->bhqk', qb, k, preferred_element_type=jnp.float32) * scale
        mask = kpos[None, :] <= (s0 + jnp.arange(ATTN_BLOCK))[:, None]
        p = _masked_softmax(s, mask)
        return jnp.einsum('bhqk,bkhd->bqhd', p.astype(v.dtype), v)

    out = lax.map(block, jnp.arange(S // ATTN_BLOCK))
    return jnp.swapaxes(out, 0, 1).reshape(B, S, H, dv)


def _mla(c_q, c_kv, k_rope, q_norm, w_q_up, kv_norm, w_kv_up, cos, sin):
    B, S, _ = c_q.shape
    q = (_rms_norm(c_q, q_norm) @ w_q_up).reshape(B, S, MLA_HEADS, MLA_NOPE_DIM + MLA_ROPE_DIM)
    q = jnp.concatenate([q[..., :MLA_NOPE_DIM], _rope(q[..., MLA_NOPE_DIM:], cos, sin)], axis=-1)
    kv = (_rms_norm(c_kv, kv_norm) @ w_kv_up).reshape(B, S, MLA_HEADS, MLA_NOPE_DIM + MLA_V_DIM)
    k_nope, v = kv[..., :MLA_NOPE_DIM], kv[..., MLA_NOPE_DIM:]
    k_pe = _rope(k_rope[:, :, None, :], cos, sin)
    k = jnp.concatenate([k_nope, jnp.broadcast_to(k_pe, (B, S, MLA_HEADS, MLA_ROPE_DIM))], axis=-1)
    return _causal_attention(q, k, v).reshape(B, S, MLA_WIDTH)


def setup_inputs(seed: int = 0) -> dict:
    key = jax.random.key(seed)
    ks = jax.random.split(key, 20)
    L = DEPTH
    dk = NSA_HEAD_DIM

    def dense(k, shape, fan_in):
        return jax.random.normal(k, shape, jnp.float32) * fan_in ** -0.5

    def gain(k, shape):
        return 1.0 + 0.02 * jax.random.normal(k, shape, jnp.float32)

    return {
        "x": jax.random.normal(ks[0], (BATCH, SEQ, D_MODEL), jnp.float32),
        "attn_norm": gain(ks[1], (L, D_MODEL)),
        "w_in": dense(ks[2], (L, D_MODEL, IN_WIDTH), D_MODEL),
        "cmp_pos_k": 0.1 * jax.random.normal(ks[3], (L, CMP_BLOCK, dk), jnp.float32),
        "cmp_w1_k": dense(ks[4], (L, CMP_BLOCK * dk, CMP_HIDDEN), CMP_BLOCK * dk),
        "cmp_w2_k": dense(ks[5], (L, CMP_HIDDEN, dk), CMP_HIDDEN),
        "cmp_pos_v": 0.1 * jax.random.normal(ks[6], (L, CMP_BLOCK, dk), jnp.float32),
        "cmp_w1_v": dense(ks[7], (L, CMP_BLOCK * dk, CMP_HIDDEN), CMP_BLOCK * dk),
        "cmp_w2_v": dense(ks[8], (L, CMP_HIDDEN, dk), CMP_HIDDEN),
        "mla_q_norm": gain(ks[9], (L, Q_LORA_RANK)),
        "w_q_up": dense(ks[10], (L, Q_LORA_RANK, MLA_HEADS * (MLA_NOPE_DIM + MLA_ROPE_DIM)), Q_LORA_RANK),
        "mla_kv_norm": gain(ks[11], (L, KV_LORA_RANK)),
        "w_kv_up": dense(ks[12], (L, KV_LORA_RANK, MLA_HEADS * (MLA_NOPE_DIM + MLA_V_DIM)), KV_LORA_RANK),
        "nsa_out_norm": gain(ks[13], (L, NSA_WIDTH)),
        "mla_out_norm": gain(ks[14], (L, MLA_WIDTH)),
        "w_out": dense(ks[15], (L, MIX_WIDTH, D_MODEL), MIX_WIDTH),
        "mlp_norm": gain(ks[16], (L, D_MODEL)),
        "w_ff1": dense(ks[17], (L, D_MODEL, D_FF), D_MODEL),
        "w_ff2": dense(ks[18], (L, D_FF, D_MODEL), D_FF),
        "final_norm": gain(ks[19], (D_MODEL,)),
    }


def reference(x, attn_norm, w_in, cmp_pos_k, cmp_w1_k, cmp_w2_k, cmp_pos_v, cmp_w1_v, cmp_w2_v,
              mla_q_norm, w_q_up, mla_kv_norm, w_kv_up, nsa_out_norm, mla_out_norm, w_out,
              mlp_norm, w_ff1, w_ff2, final_norm):
    B, S, _ = x.shape
    cos_a, sin_a = _rope_tables(S, NSA_HEAD_DIM, x.dtype)
    cos_b, sin_b = _rope_tables(S, MLA_ROPE_DIM, x.dtype)
    offsets = [int(o) for o in np.cumsum(IN_SIZES)[:-1]]
    G = NSA_KV_GROUPS
    for l in range(DEPTH):
        h = _rms_norm(x, attn_norm[l])
        proj = h @ w_in[l]
        (q_a, k_c, v_c, k_s, v_s, k_w, v_w, g_a, c_q, c_kv, k_r) = jnp.split(proj, offsets, axis=-1)
        q_a = _rope(_heads(q_a, NSA_HEADS), cos_a, sin_a)
        k_cmp = _compress(_rope(_heads(k_c, G), cos_a, sin_a), cmp_pos_k[l], cmp_w1_k[l], cmp_w2_k[l])
        v_cmp = _compress(_heads(v_c, G), cmp_pos_v[l], cmp_w1_v[l], cmp_w2_v[l])
        gates = jax.nn.sigmoid(g_a).reshape(B, S, NSA_HEADS, NSA_BRANCHES)
        o_a = _nsa(q_a, k_cmp, v_cmp,
                   _rope(_heads(k_s, G), cos_a, sin_a), _heads(v_s, G),
                   _rope(_heads(k_w, G), cos_a, sin_a), _heads(v_w, G), gates)
        o_b = _mla(c_q, c_kv, k_r, mla_q_norm[l], w_q_up[l], mla_kv_norm[l], w_kv_up[l], cos_b, sin_b)
        mixed = jnp.concatenate([_rms_norm(o_a, nsa_out_norm[l]), _rms_norm(o_b, mla_out_norm[l])], axis=-1)
        x = x + mixed @ w_out[l]
        h = _rms_norm(x, mlp_norm[l])
        x = x + jnp.square(jax.nn.relu(h @ w_ff1[l])) @ w_ff2[l]
    return _rms_norm(x, final_norm)
```

```python
import functools

import numpy as np
import jax
import jax.numpy as jnp
from jax import lax
from jax.experimental import pallas as pl
from jax.experimental.pallas import tpu as pltpu

D_MODEL = 1024
NSA_HEADS = 8
NSA_KV_GROUPS = 2
NSA_REP = NSA_HEADS // NSA_KV_GROUPS
NSA_HEAD_DIM = 64
NSA_BRANCHES = 3
CMP_BLOCK = 32
CMP_STRIDE = 16
CMP_HIDDEN = 256
SEL_BLOCK = 64
SEL_BLOCK_LOG2 = 6
N_SELECT = 16
N_LOCAL = 2
WINDOW = 512
MLA_HEADS = 8
MLA_NOPE_DIM = 64
MLA_ROPE_DIM = 32
MLA_V_DIM = 64
Q_LORA_RANK = 256
KV_LORA_RANK = 128
D_FF = 4 * D_MODEL
ROPE_THETA = 10000.0
NORM_EPS = 1e-6
NEG_INF = -1e30
FORCE_BONUS = 1e4
NSA_WIDTH = NSA_HEADS * NSA_HEAD_DIM
MLA_WIDTH = MLA_HEADS * MLA_V_DIM

LANES = 128
HALF = LANES // 2
MAX_SEL_BLOCKS = HALF
MXU_DTYPE = jnp.bfloat16
VMEM_LIMIT = 52 * 1024 * 1024
PROJ_ROWS = 512
NSA_Q_TILE = 128
NSA_K_TILE = 512
MLA_TILE = 512
MLP_ROWS = 512
MLP_FF_TILE = 1024

_N_ROPE_BLK = 7
_OFF_ROT = _N_ROPE_BLK * LANES
_OFF_VC = 2 * _OFF_ROT
_OFF_VSEL = _OFF_VC + LANES
_OFF_VWIN = _OFF_VSEL + 2 * LANES
_OFF_CQ = _OFF_VWIN + 2 * LANES
_OFF_CKV = _OFF_CQ + Q_LORA_RANK
_OFF_KRA = _OFF_CKV + KV_LORA_RANK
_OFF_KRB = _OFF_KRA + LANES
_OFF_GATE = _OFF_KRB + LANES
_PROJ_COLS = _OFF_GATE + LANES


def _dot(a, b):
    return jnp.dot(a, b, preferred_element_type=jnp.float32)


def _dot_nt(a, b):
    return lax.dot_general(a, b, (((1,), (1,)), ((), ())), preferred_element_type=jnp.float32)


def _rms(x, gain):
    return x * lax.rsqrt(jnp.mean(x * x, axis=-1, keepdims=True) + NORM_EPS) * gain


def _softmax_rows(s, mask):
    s = jnp.where(mask, s, NEG_INF)
    m = jnp.max(s, axis=-1, keepdims=True)
    e = jnp.where(mask, jnp.exp(s - m), 0.0)
    return e / jnp.maximum(jnp.sum(e, axis=-1, keepdims=True), 1e-30)


def _proj_kernel(x_ref, gain_ref, w_ref, cos_a_ref, sin_a_ref, cos_b_ref, sin_b_ref,
                 qn_ref, wq2_ref, kvn_ref, wkv2_ref,
                 q_nsa_ref, k_sel_ref, k_win_ref, kc_ref, vc_ref, v_sel_ref, v_win_ref, gate_ref,
                 q_mla_ref, k_mla_ref, v_mla_ref):
    rows = x_ref.shape[1]
    hn = _rms(x_ref[0], gain_ref[...]).astype(MXU_DTYPE)
    res = _dot(hn, w_ref[...])
    cos_a, sin_a = cos_a_ref[...], sin_a_ref[...]
    lane = lax.broadcasted_iota(jnp.int32, (rows, LANES), 1)
    lower = lane < HALF

    def rope_blk(j):
        return (res[:, j * LANES:(j + 1) * LANES] * cos_a
                + res[:, _OFF_ROT + j * LANES:_OFF_ROT + (j + 1) * LANES] * sin_a)

    def split_pair(blk):
        return jnp.where(lower, blk, 0.0), jnp.where(lower, pltpu.roll(blk, HALF, 1), 0.0)

    scale = NSA_HEAD_DIM ** -0.5
    for j in range(NSA_HEADS // 2):
        a, b = split_pair(rope_blk(j) * scale)
        q_nsa_ref[0, 2 * j] = a.astype(q_nsa_ref.dtype)
        q_nsa_ref[0, 2 * j + 1] = b.astype(q_nsa_ref.dtype)
    kc_ref[0] = rope_blk(4)
    pos = pl.program_id(1) * rows + lax.broadcasted_iota(jnp.int32, (rows, LANES), 0)
    onehot = jnp.where((lane - HALF) == jnp.right_shift(pos, SEL_BLOCK_LOG2), 1.0, 0.0)
    a, b = split_pair(rope_blk(5))
    k_sel_ref[0, 0] = (a + onehot).astype(k_sel_ref.dtype)
    k_sel_ref[0, 1] = (b + onehot).astype(k_sel_ref.dtype)
    a, b = split_pair(rope_blk(6))
    k_win_ref[0, 0] = a.astype(k_win_ref.dtype)
    k_win_ref[0, 1] = b.astype(k_win_ref.dtype)
    vc_ref[0] = res[:, _OFF_VC:_OFF_VC + LANES]
    for g in range(NSA_KV_GROUPS):
        v_sel_ref[0, g] = res[:, _OFF_VSEL + g * LANES:_OFF_VSEL + (g + 1) * LANES].astype(v_sel_ref.dtype)
        v_win_ref[0, g] = res[:, _OFF_VWIN + g * LANES:_OFF_VWIN + (g + 1) * LANES].astype(v_win_ref.dtype)
    gate_ref[0] = jax.nn.sigmoid(res[:, _OFF_GATE:_OFF_GATE + LANES])

    cos_b, sin_b = cos_b_ref[...], sin_b_ref[...]
    cqn = _rms(res[:, _OFF_CQ:_OFF_CQ + Q_LORA_RANK], qn_ref[...]).astype(MXU_DTYPE)
    yq = _dot(cqn, wq2_ref[...])
    rot_off = MLA_HEADS * LANES
    for h in range(MLA_HEADS):
        q = (yq[:, h * LANES:(h + 1) * LANES] * cos_b
             + yq[:, rot_off + h * LANES:rot_off + (h + 1) * LANES] * sin_b)
        q_mla_ref[0, h] = q.astype(q_mla_ref.dtype)
    ckvn = _rms(res[:, _OFF_CKV:_OFF_CKV + KV_LORA_RANK], kvn_ref[...]).astype(MXU_DTYPE)
    ykv = _dot(ckvn, wkv2_ref[...])
    k_pe = res[:, _OFF_KRA:_OFF_KRA + LANES] * cos_b + res[:, _OFF_KRB:_OFF_KRB + LANES] * sin_b
    for h in range(MLA_HEADS):
        k_mla_ref[0, h] = (ykv[:, h * LANES:(h + 1) * LANES] + k_pe).astype(k_mla_ref.dtype)
    v_off = MLA_HEADS * LANES
    for p in range(MLA_HEADS // 2):
        v_mla_ref[0, p] = ykv[:, v_off + p * LANES:v_off + (p + 1) * LANES].astype(v_mla_ref.dtype)


def _proj_call(x, gain, w_all, cos_a, sin_a, cos_b, sin_b, qn, wq2, kvn, wkv2):
    B, S, D = x.shape
    rows = min(PROJ_ROWS, S)
    ns = S // rows
    const = lambda b, i: (0, 0)
    tab = pl.BlockSpec((rows, LANES), lambda b, i: (i, 0))
    hm = lambda n: pl.BlockSpec((1, n, rows, LANES), lambda b, i: (b, 0, i, 0))
    flat = pl.BlockSpec((1, rows, LANES), lambda b, i: (b, i, 0))
    sds = jax.ShapeDtypeStruct
    return pl.pallas_call(
        _proj_kernel,
        grid=(B, ns),
        in_specs=[
            pl.BlockSpec((1, rows, D), lambda b, i: (b, i, 0)),
            pl.BlockSpec((1, D), const),
            pl.BlockSpec((D, _PROJ_COLS), const),
            tab, tab, tab, tab,
            pl.BlockSpec((1, Q_LORA_RANK), const),
            pl.BlockSpec(wq2.shape, const),
            pl.BlockSpec((1, KV_LORA_RANK), const),
            pl.BlockSpec(wkv2.shape, const),
        ],
        out_specs=[hm(NSA_HEADS), hm(NSA_KV_GROUPS), hm(NSA_KV_GROUPS), flat, flat,
                   hm(NSA_KV_GROUPS), hm(NSA_KV_GROUPS), flat,
                   hm(MLA_HEADS), hm(MLA_HEADS), hm(MLA_HEADS // 2)],
        out_shape=[
            sds((B, NSA_HEADS, S, LANES), MXU_DTYPE),
            sds((B, NSA_KV_GROUPS, S, LANES), MXU_DTYPE),
            sds((B, NSA_KV_GROUPS, S, LANES), MXU_DTYPE),
            sds((B, S, LANES), jnp.float32),
            sds((B, S, LANES), jnp.float32),
            sds((B, NSA_KV_GROUPS, S, LANES), MXU_DTYPE),
            sds((B, NSA_KV_GROUPS, S, LANES), MXU_DTYPE),
            sds((B, S, LANES), jnp.float32),
            sds((B, MLA_HEADS, S, LANES), MXU_DTYPE),
            sds((B, MLA_HEADS, S, LANES), MXU_DTYPE),
            sds((B, MLA_HEADS // 2, S, LANES), MXU_DTYPE),
        ],
        compiler_params=pltpu.CompilerParams(
            dimension_semantics=("parallel", "parallel"), vmem_limit_bytes=VMEM_LIMIT),
        name="proj",
    )(x, gain, w_all, cos_a, sin_a, cos_b, sin_b, qn, wq2, kvn, wkv2)


def _compress_kernel(xk_ref, xv_ref, posk_ref, posv_ref, w1k_ref, w1v_ref, w2k_ref, w2v_ref,
                     kcmp_ref, vcmp_ref):
    n_chunk = xk_ref.shape[1]

    def one(x_ref, pos_ref, w1_ref, w2_ref, out_ref):
        x = x_ref[0]
        h_lo = _dot((x + pos_ref[0:1, :]).astype(MXU_DTYPE), w1_ref[0])
        h_hi = _dot((x + pos_ref[1:2, :]).astype(MXU_DTYPE), w1_ref[1])
        hid = h_lo + pltpu.roll(h_hi, n_chunk - 1, 0)
        act = hid * jax.nn.sigmoid(hid)
        out = _dot(act.astype(MXU_DTYPE), w2_ref[...])
        for g in range(NSA_KV_GROUPS):
            out_ref[0, g] = out[:, g * LANES:(g + 1) * LANES].astype(out_ref.dtype)

    one(xk_ref, posk_ref, w1k_ref, w2k_ref, kcmp_ref)
    one(xv_ref, posv_ref, w1v_ref, w2v_ref, vcmp_ref)


def _compress_call(xk, xv, posk, posv, w1k, w1v, w2k, w2v):
    B, n_chunk, width = xk.shape
    xs = pl.BlockSpec((1, n_chunk, width), lambda b: (b, 0, 0))
    c2 = lambda a: pl.BlockSpec(a.shape, lambda b: (0,) * a.ndim)
    out = pl.BlockSpec((1, NSA_KV_GROUPS, n_chunk, LANES), lambda b: (b, 0, 0, 0))
    shape = jax.ShapeDtypeStruct((B, NSA_KV_GROUPS, n_chunk, LANES), MXU_DTYPE)
    return pl.pallas_call(
        _compress_kernel,
        grid=(B,),
        in_specs=[xs, xs, c2(posk), c2(posv), c2(w1k), c2(w1v), c2(w2k), c2(w2v)],
        out_specs=[out, out],
        out_shape=[shape, shape],
        compiler_params=pltpu.CompilerParams(
            dimension_semantics=("parallel",), vmem_limit_bytes=VMEM_LIMIT),
        name="compress",
    )(xk, xv, posk, posv, w1k, w1v, w2k, w2v)


def _nsa_kernel(q_ref, kcmp_ref, vcmp_ref, ksel_ref, vsel_ref, kwin_ref, vwin_ref, gate_ref,
                imp_ref, o_ref, *, n_sel, win_keys):
    g = pl.program_id(1)
    tq = q_ref.shape[2]
    n_chunk = kcmp_ref.shape[2]
    rows = NSA_REP * tq
    s0 = pl.program_id(2) * tq
    q3 = q_ref[0]
    q2 = q3.reshape(rows, LANES)

    def t_of(shape):
        r = lax.broadcasted_iota(jnp.int32, (NSA_REP, tq) + shape[1:], 1)
        return s0 + r.reshape(shape)

    s = _dot_nt(q2, kcmp_ref[0, 0])
    cmp_end = lax.broadcasted_iota(jnp.int32, (rows, n_chunk), 1) * CMP_STRIDE + (CMP_BLOCK - 1)
    p_cmp = _softmax_rows(s, cmp_end <= t_of((rows, n_chunk)))
    o_cmp = _dot(p_cmp.astype(MXU_DTYPE), vcmp_ref[0, 0])

    p_sum = p_cmp[0:tq]
    for r in range(1, NSA_REP):
        p_sum = p_sum + p_cmp[r * tq:(r + 1) * tq]
    hi = p_sum.astype(MXU_DTYPE)
    rem = p_sum - hi.astype(jnp.float32)
    mid = rem.astype(MXU_DTYPE)
    lo = (rem - mid.astype(jnp.float32)).astype(MXU_DTYPE)
    m_t = imp_ref[...]
    imp_t = _dot_nt(m_t, hi) + _dot_nt(m_t, mid) + _dot_nt(m_t, lo)

    blk = lax.broadcasted_iota(jnp.int32, (MAX_SEL_BLOCKS, tq), 0)
    cur = jnp.right_shift(s0 + lax.broadcasted_iota(jnp.int32, (MAX_SEL_BLOCKS, tq), 1), SEL_BLOCK_LOG2)
    valid = blk <= cur
    forced = valid & ((blk == 0) | (blk > cur - N_LOCAL))
    score = jnp.where(forced, FORCE_BONUS, jnp.where(valid, imp_t, -1.0))
    rank = jnp.zeros((MAX_SEL_BLOCKS, tq), jnp.float32)
    for j in range(MAX_SEL_BLOCKS):
        row = score[j:j + 1, :]
        ge = jnp.where(row >= score, 1.0, 0.0)
        gt = jnp.where(row > score, 1.0, 0.0)
        rank = rank + jnp.where(blk > j, ge, gt)
    bias_t = jnp.where(rank < n_sel, 0.0, NEG_INF)
    bias_t = jnp.concatenate([jnp.zeros_like(bias_t), bias_t], axis=0)
    eye = jnp.where(lax.broadcasted_iota(jnp.int32, (tq, tq), 0)
                    == lax.broadcasted_iota(jnp.int32, (tq, tq), 1), 1.0, 0.0)
    bias = _dot_nt(eye.astype(MXU_DTYPE), bias_t.astype(MXU_DTYPE))

    q_aug = (q3.astype(jnp.float32) + bias[None]).astype(MXU_DTYPE).reshape(rows, LANES)
    t_col = t_of((rows, 1))
    tk = min(NSA_K_TILE, ksel_ref.shape[2])

    def sel_tile(kt, carry, causal):
        m, l, acc = carry
        start = pl.multiple_of(kt * tk, tk)
        s = _dot_nt(q_aug, ksel_ref[0, 0, pl.ds(start, tk), :])
        if causal:
            key = start + lax.broadcasted_iota(jnp.int32, (rows, tk), 1)
            s = jnp.where(key <= t_col, s, NEG_INF)
        m_new = jnp.maximum(m, jnp.max(s, axis=-1, keepdims=True))
        alpha = jnp.exp(m - m_new)
        p = jnp.exp(s - m_new)
        l = alpha * l + jnp.sum(p, axis=-1, keepdims=True)
        acc = alpha * acc + _dot(p.astype(MXU_DTYPE), vsel_ref[0, 0, pl.ds(start, tk), :])
        return m_new, l, acc

    init = (jnp.full((rows, 1), NEG_INF, jnp.float32), jnp.zeros((rows, 1), jnp.float32),
            jnp.zeros((rows, LANES), jnp.float32))
    n_full = s0 // tk
    carry = lax.fori_loop(0, n_full, lambda kt, c: sel_tile(kt, c, False), init)
    _, l, acc = sel_tile(n_full, carry, True)
    o_sel = acc / jnp.maximum(l, 1e-30)

    w_start = pl.multiple_of(jnp.maximum(s0 - WINDOW, 0), tq)
    s = _dot_nt(q2, kwin_ref[0, 0, pl.ds(w_start, win_keys), :])
    key = w_start + lax.broadcasted_iota(jnp.int32, (rows, win_keys), 1)
    p_win = _softmax_rows(s, (key <= t_col) & (key > t_col - WINDOW))
    o_win = _dot(p_win.astype(MXU_DTYPE), vwin_ref[0, 0, pl.ds(w_start, win_keys), :])

    gates = gate_ref[0]

    def gate(r, br):
        c0 = r * NSA_BRANCHES + br
        c1 = (NSA_REP + r) * NSA_BRANCHES + br
        return jnp.where(g == 0, gates[:, c0:c0 + 1], gates[:, c1:c1 + 1])

    lower = lax.broadcasted_iota(jnp.int32, (tq, LANES), 1) < HALF
    heads = []
    for r in range(NSA_REP):
        sl = slice(r * tq, (r + 1) * tq)
        heads.append(gate(r, 0) * o_cmp[sl] + gate(r, 1) * o_sel[sl] + gate(r, 2) * o_win[sl])
    for pair in range(NSA_REP // 2):
        o_ref[0, :, pair * LANES:(pair + 1) * LANES] = jnp.where(lower, heads[2 * pair], heads[2 * pair + 1])


def _nsa_call(q_nsa, k_cmp, v_cmp, k_sel, v_sel, k_win, v_win, gates, imp_mat):
    B, _, S, _ = q_nsa.shape
    n_chunk = k_cmp.shape[2]
    tq = min(NSA_Q_TILE, S)
    n_sel = min(N_SELECT, S // SEL_BLOCK)
    win_keys = min(WINDOW + tq, S)
    kv = lambda n: pl.BlockSpec((1, 1, n, LANES), lambda b, g, i: (b, g, 0, 0))
    return pl.pallas_call(
        functools.partial(_nsa_kernel, n_sel=n_sel, win_keys=win_keys),
        grid=(B, NSA_KV_GROUPS, S // tq),
        in_specs=[
            pl.BlockSpec((1, NSA_REP, tq, LANES), lambda b, g, i: (b, g, i, 0)),
            kv(n_chunk), kv(n_chunk), kv(S), kv(S), kv(S), kv(S),
            pl.BlockSpec((1, tq, LANES), lambda b, g, i: (b, i, 0)),
            pl.BlockSpec(imp_mat.shape, lambda b, g, i: (0, 0)),
        ],
        out_specs=pl.BlockSpec((1, tq, NSA_REP * NSA_HEAD_DIM), lambda b, g, i: (b, i, g)),
        out_shape=jax.ShapeDtypeStruct((B, S, NSA_WIDTH), jnp.float32),
        compiler_params=pltpu.CompilerParams(
            dimension_semantics=("parallel", "parallel", "arbitrary"), vmem_limit_bytes=VMEM_LIMIT),
        name="nsa",
    )(q_nsa, k_cmp, v_cmp, k_sel, v_sel, k_win, v_win, gates, imp_mat)


def _mla_kernel(q_ref, k_ref, v_ref, o_ref):
    tq = q_ref.shape[2]
    i = pl.program_id(2)
    scale = (MLA_NOPE_DIM + MLA_ROPE_DIM) ** -0.5
    t_col = i * tq + lax.broadcasted_iota(jnp.int32, (tq, 1), 0)
    outs = []
    for h in range(2):
        q = q_ref[0, h]

        def tile(kt, carry, causal, q=q, h=h):
            m, l, acc = carry
            start = pl.multiple_of(kt * tq, tq)
            s = _dot_nt(q, k_ref[0, h, pl.ds(start, tq), :]) * scale
            if causal:
                key = start + lax.broadcasted_iota(jnp.int32, (tq, tq), 1)
                s = jnp.where(key <= t_col, s, NEG_INF)
            m_new = jnp.maximum(m, jnp.max(s, axis=-1, keepdims=True))
            alpha = jnp.exp(m - m_new)
            p = jnp.exp(s - m_new)
            l = alpha * l + jnp.sum(p, axis=-1, keepdims=True)
            acc = alpha * acc + _dot(p.astype(MXU_DTYPE), v_ref[0, 0, pl.ds(start, tq), :])
            return m_new, l, acc

        init = (jnp.full((tq, 1), NEG_INF, jnp.float32), jnp.zeros((tq, 1), jnp.float32),
                jnp.zeros((tq, LANES), jnp.float32))
        carry = lax.fori_loop(0, i, lambda kt, c: tile(kt, c, False), init)
        _, l, acc = tile(i, carry, True)
        outs.append(acc / jnp.maximum(l, 1e-30))
    lower = lax.broadcasted_iota(jnp.int32, (tq, LANES), 1) < HALF
    o_ref[0] = jnp.where(lower, outs[0], outs[1])


def _mla_call(q_mla, k_mla, v_mla):
    B, _, S, _ = q_mla.shape
    tq = min(MLA_TILE, S)
    return pl.pallas_call(
        _mla_kernel,
        grid=(B, MLA_HEADS // 2, S // tq),
        in_specs=[
            pl.BlockSpec((1, 2, tq, LANES), lambda b, p, i: (b, p, i, 0)),
            pl.BlockSpec((1, 2, S, LANES), lambda b, p, i: (b, p, 0, 0)),
            pl.BlockSpec((1, 1, S, LANES), lambda b, p, i: (b, p, 0, 0)),
        ],
        out_specs=pl.BlockSpec((1, tq, LANES), lambda b, p, i: (b, i, p)),
        out_shape=jax.ShapeDtypeStruct((B, S, MLA_WIDTH), jnp.float32),
        compiler_params=pltpu.CompilerParams(
            dimension_semantics=("parallel", "parallel", "arbitrary"), vmem_limit_bytes=VMEM_LIMIT),
        name="mla",
    )(q_mla, k_mla, v_mla)


def _mix_mlp_kernel(x_ref, oa_ref, ob_ref, na_ref, nb_ref, woa_ref, wob_ref, mn_ref, w1_ref, w2_ref,
                    fn_ref, o_ref, x1_ref, hn_ref, acc_ref, *, final):
    f = pl.program_id(1)

    @pl.when(f == 0)
    def _():
        na = _rms(oa_ref[...], na_ref[...]).astype(MXU_DTYPE)
        nb = _rms(ob_ref[...], nb_ref[...]).astype(MXU_DTYPE)
        x1 = x_ref[...] + (_dot(na, woa_ref[...]) + _dot(nb, wob_ref[...]))
        x1_ref[...] = x1
        hn_ref[...] = _rms(x1, mn_ref[...]).astype(MXU_DTYPE)
        acc_ref[...] = jnp.zeros_like(acc_ref)

    a = jnp.square(jnp.maximum(_dot(hn_ref[...], w1_ref[...]), 0.0))
    acc_ref[...] += _dot(a.astype(MXU_DTYPE), w2_ref[...])

    @pl.when(f == pl.num_programs(1) - 1)
    def _():
        y = x1_ref[...] + acc_ref[...]
        o_ref[...] = _rms(y, fn_ref[...]) if final else y


def _mix_mlp_call(x, o_a, o_b, na, nb, wo_a, wo_b, mn, w1, w2, fn, final):
    T, D = x.shape
    rows = min(MLP_ROWS, T)
    tf = MLP_FF_TILE
    row = lambda w: pl.BlockSpec((rows, w), lambda i, f: (i, 0))
    const = lambda a: pl.BlockSpec(a.shape, lambda i, f: (0, 0))
    return pl.pallas_call(
        functools.partial(_mix_mlp_kernel, final=final),
        grid=(T // rows, D_FF // tf),
        in_specs=[row(D), row(NSA_WIDTH), row(MLA_WIDTH), const(na), const(nb), const(wo_a), const(wo_b),
                  const(mn),
                  pl.BlockSpec((D, tf), lambda i, f: (0, f)),
                  pl.BlockSpec((tf, D), lambda i, f: (f, 0)),
                  const(fn)],
        out_specs=row(D),
        out_shape=jax.ShapeDtypeStruct((T, D), jnp.float32),
        scratch_shapes=[pltpu.VMEM((rows, D), jnp.float32), pltpu.VMEM((rows, D), MXU_DTYPE),
                        pltpu.VMEM((rows, D), jnp.float32)],
        compiler_params=pltpu.CompilerParams(
            dimension_semantics=("parallel", "arbitrary"), vmem_limit_bytes=VMEM_LIMIT),
        name="mix_mlp",
    )(x, o_a, o_b, na, nb, wo_a, wo_b, mn, w1, w2, fn)


def _rot_cols(w, dim):
    lead = w.shape[:-1]
    w4 = w.reshape(lead + (w.shape[-1] // dim, 2, dim // 2))
    return jnp.concatenate([-w4[..., 1, :], w4[..., 0, :]], axis=-1).reshape(w.shape)


def _pad_cols(w, left, total):
    pad = [(0, 0)] * (w.ndim - 1) + [(left, total - left - w.shape[-1])]
    return jnp.pad(w, pad)


def _pack_in_proj(w_in):
    sizes = (NSA_WIDTH,) + (NSA_KV_GROUPS * NSA_HEAD_DIM,) * 6 + (
        NSA_BRANCHES * NSA_HEADS, Q_LORA_RANK, KV_LORA_RANK, MLA_ROPE_DIM)
    offs = np.cumsum((0,) + sizes)
    q_a, k_c, v_c, k_s, v_s, k_w, v_w, g_a, c_q, c_kv, k_r = (
        w_in[..., offs[i]:offs[i + 1]] for i in range(len(sizes)))
    roped = jnp.concatenate([q_a, k_c, k_s, k_w], axis=-1)
    dup = lambda v: jnp.concatenate(
        [jnp.concatenate([v[..., g * HALF:(g + 1) * HALF]] * 2, axis=-1) for g in range(NSA_KV_GROUPS)], axis=-1)
    cols = [roped, _rot_cols(roped, NSA_HEAD_DIM), v_c, dup(v_s), dup(v_w), c_q, c_kv,
            _pad_cols(k_r, HALF, LANES), _pad_cols(_rot_cols(k_r, MLA_ROPE_DIM), HALF, LANES),
            _pad_cols(g_a, 0, LANES)]
    out = jnp.concatenate(cols, axis=-1)
    assert out.shape[-1] == _PROJ_COLS
    return out.astype(MXU_DTYPE)


def _pack_q_up(w_q_up):
    L, R, _ = w_q_up.shape
    w = w_q_up.reshape(L, R, MLA_HEADS, MLA_NOPE_DIM + MLA_ROPE_DIM)
    a = _pad_cols(w, 0, LANES)
    b = _pad_cols(_rot_cols(w[..., MLA_NOPE_DIM:], MLA_ROPE_DIM), MLA_NOPE_DIM, LANES)
    return jnp.concatenate([a.reshape(L, R, -1), b.reshape(L, R, -1)], axis=-1).astype(MXU_DTYPE)


def _pack_kv_up(w_kv_up):
    L, R, _ = w_kv_up.shape
    w = w_kv_up.reshape(L, R, MLA_HEADS, MLA_NOPE_DIM + MLA_V_DIM)
    k = _pad_cols(w[..., :MLA_NOPE_DIM], 0, LANES).reshape(L, R, -1)
    v = w[..., MLA_NOPE_DIM:].reshape(L, R, -1)
    return jnp.concatenate([k, v], axis=-1).astype(MXU_DTYPE)


def _pack_compress(pos, w1, w2, dup_out):
    L = pos.shape[0]
    G, dk, H = NSA_KV_GROUPS, NSA_HEAD_DIM, CMP_HIDDEN
    pos2 = jnp.concatenate([pos] * G, axis=-1).reshape(L, 2, CMP_STRIDE * G * dk)
    w1r = w1.reshape(L, 2, CMP_STRIDE, dk, H)
    w1b = jnp.zeros((L, 2, CMP_STRIDE, G, dk, G, H), w1.dtype)
    for g in range(G):
        w1b = w1b.at[:, :, :, g, :, g, :].set(w1r)
    w1b = w1b.reshape(L, 2, CMP_STRIDE * G * dk, G * H)
    w2b = jnp.zeros((L, G, H, G, LANES), w2.dtype)
    for g in range(G):
        w2b = w2b.at[:, g, :, g, :dk].set(w2)
        if dup_out:
            w2b = w2b.at[:, g, :, g, dk:].set(w2)
    return pos2, w1b.astype(MXU_DTYPE), w2b.reshape(L, G * H, G * LANES).astype(MXU_DTYPE)


def _importance_matrix(n_chunk):
    a, b = SEL_BLOCK // CMP_STRIDE, CMP_BLOCK // CMP_STRIDE
    overlap = np.convolve(np.ones(a), np.ones(b))
    n_cmp = n_chunk - (b - 1)
    m = np.zeros((MAX_SEL_BLOCKS, n_chunk), np.float32)
    for blk in range(n_chunk * CMP_STRIDE // SEL_BLOCK):
        for j, wgt in enumerate(overlap):
            i = a * blk + j - (b - 1)
            if 0 <= i < n_cmp:
                m[blk, i] = wgt
    return jnp.asarray(m, MXU_DTYPE)


def _rope_tables(seq, dim):
    inv_freq = 1.0 / (ROPE_THETA ** (jnp.arange(0, dim, 2, dtype=jnp.float32) / dim))
    ang = jnp.arange(seq, dtype=jnp.float32)[:, None] * inv_freq[None, :]
    ang = jnp.concatenate([ang, ang], axis=-1)
    return jnp.cos(ang), jnp.sin(ang)


def kernel(x, attn_norm, w_in, cmp_pos_k, cmp_w1_k, cmp_w2_k, cmp_pos_v, cmp_w1_v, cmp_w2_v, mla_q_norm,
           w_q_up, mla_kv_norm, w_kv_up, nsa_out_norm, mla_out_norm, w_out, mlp_norm, w_ff1, w_ff2, final_norm):
    B, S, D = x.shape
    depth = w_in.shape[0]
    assert S % SEL_BLOCK == 0 and S // SEL_BLOCK <= MAX_SEL_BLOCKS and S % CMP_STRIDE == 0
    n_chunk = S // CMP_STRIDE

    cos64, sin64 = _rope_tables(S, NSA_HEAD_DIM)
    cos_a, sin_a = jnp.concatenate([cos64, cos64], -1), jnp.concatenate([sin64, sin64], -1)
    cos32, sin32 = _rope_tables(S, MLA_ROPE_DIM)
    pad = jnp.zeros((S, LANES - MLA_NOPE_DIM - MLA_ROPE_DIM), jnp.float32)
    cos_b = jnp.concatenate([jnp.ones((S, MLA_NOPE_DIM), jnp.float32), cos32, pad], -1)
    sin_b = jnp.concatenate([jnp.zeros((S, MLA_NOPE_DIM), jnp.float32), sin32, pad], -1)

    w_all = _pack_in_proj(w_in)
    wq2 = _pack_q_up(w_q_up)
    wkv2 = _pack_kv_up(w_kv_up)
    posk, w1k, w2k = _pack_compress(cmp_pos_k, cmp_w1_k, cmp_w2_k, dup_out=False)
    posv, w1v, w2v = _pack_compress(cmp_pos_v, cmp_w1_v, cmp_w2_v, dup_out=True)
    imp_mat = _importance_matrix(n_chunk)
    wo = w_out.astype(MXU_DTYPE)
    w1 = w_ff1.astype(MXU_DTYPE)
    w2 = w_ff2.astype(MXU_DTYPE)
    row = lambda v: v.reshape(1, -1)

    for l in range(depth):
        (q_nsa, k_sel, k_win, k_c, v_c, v_sel, v_win, gates, q_mla, k_mla, v_mla) = _proj_call(
            x, row(attn_norm[l]), w_all[l], cos_a, sin_a, cos_b, sin_b,
            row(mla_q_norm[l]), wq2[l], row(mla_kv_norm[l]), wkv2[l])
        k_cmp, v_cmp = _compress_call(
            k_c.reshape(B, n_chunk, CMP_STRIDE * LANES), v_c.reshape(B, n_chunk, CMP_STRIDE * LANES),
            posk[l], posv[l], w1k[l], w1v[l], w2k[l], w2v[l])
        o_a = _nsa_call(q_nsa, k_cmp, v_cmp, k_sel, v_sel, k_win, v_win, gates, imp_mat)
        o_b = _mla_call(q_mla, k_mla, v_mla)
        x = _mix_mlp_call(
            x.reshape(B * S, D), o_a.reshape(B * S, NSA_WIDTH), o_b.reshape(B * S, MLA_WIDTH),
            row(nsa_out_norm[l]), row(mla_out_norm[l]), wo[l, :NSA_WIDTH], wo[l, NSA_WIDTH:],
            row(mlp_norm[l]), w1[l], w2[l], row(final_norm), final=(l == depth - 1)).reshape(B, S, D)
    return x
```

```python
import functools

import numpy as np
import jax
import jax.numpy as jnp
from jax import lax
from jax.experimental import pallas as pl
from jax.experimental.pallas import tpu as pltpu

D_MODEL = 1024
NSA_HEADS = 8
NSA_KV_GROUPS = 2
NSA_REP = NSA_HEADS // NSA_KV_GROUPS
NSA_HEAD_DIM = 64
NSA_BRANCHES = 3
CMP_BLOCK = 32
CMP_STRIDE = 16
CMP_HIDDEN = 256
SEL_BLOCK = 64
SEL_BLOCK_LOG2 = 6
N_SELECT = 16
N_LOCAL = 2
WINDOW = 512
MLA_HEADS = 8
MLA_NOPE_DIM = 64
MLA_ROPE_DIM = 32
MLA_V_DIM = 64
Q_LORA_RANK = 256
KV_LORA_RANK = 128
D_FF = 4 * D_MODEL
ROPE_THETA = 10000.0
NORM_EPS = 1e-6
NEG_INF = -1e30
FORCE_BONUS = 1e4
LOG2_E = 1.4426950408889634
NSA_WIDTH = NSA_HEADS * NSA_HEAD_DIM
MLA_WIDTH = MLA_HEADS * MLA_V_DIM

LANES = 128
HALF = LANES // 2
MAX_SEL_BLOCKS = HALF
MXU_DTYPE = jnp.bfloat16
VMEM_LIMIT = 52 * 1024 * 1024
PROJ_ROWS = 512
ATTN_K_TILE = 256
ATTN_K_SPLIT = 2
NSA_Q_TILE = 256
NSA_SUB_TILE = 128
WIN_V_TILE = 128
MLA_TILE = ATTN_K_TILE * ATTN_K_SPLIT
MLA_HEADS_PER_STEP = 4
MLP_ROWS = 512
MLP_FF_TILE = 1024

_N_ROPE_BLK = 3
_OFF_ROT = _N_ROPE_BLK * LANES
_OFF_VC = 2 * _OFF_ROT
_OFF_CQ = _OFF_VC + LANES
_OFF_CKV = _OFF_CQ + Q_LORA_RANK
_OFF_KRA = _OFF_CKV + KV_LORA_RANK
_OFF_KRB = _OFF_KRA + LANES
_PROJ_COLS = _OFF_KRB + LANES
_FT_QROT = NSA_WIDTH
_FT_VSEL = 2 * NSA_WIDTH
_FT_VWIN = _FT_VSEL + NSA_KV_GROUPS * NSA_HEAD_DIM
_FT_GATE = _FT_VWIN + NSA_KV_GROUPS * NSA_HEAD_DIM
_GATE_ROWS = 32
_FT_ROWS = _FT_GATE + _GATE_ROWS


def _dot(a, b):
    return jnp.dot(a, b, preferred_element_type=jnp.float32)


def _dot_nt(a, b):
    return lax.dot_general(a, b, (((1,), (1,)), ((), ())), preferred_element_type=jnp.float32)


def _rms(x, gain):
    return x * lax.rsqrt(jnp.mean(x * x, axis=-1, keepdims=True) + NORM_EPS) * gain


def _ones_row_block(n):
    return jnp.where(lax.broadcasted_iota(jnp.int32, (HALF, n), 0) == 0, 1.0, 0.0)


def _flash_update_t(carry, s_t, v_t, scale):
    m, acc = carry
    c = scale * LOG2_E
    m_new = jnp.maximum(m, jnp.max(s_t, axis=0, keepdims=True))
    alpha = jnp.exp2((m - m_new) * c)
    p_t = jnp.exp2((s_t - m_new) * c)
    return m_new, alpha * acc + _dot(v_t, p_t.astype(MXU_DTYPE))


def _flash_init_t(n):
    return jnp.full((1, n), NEG_INF, jnp.float32), jnp.zeros((LANES, n), jnp.float32)


def _flash_merge_t(states, scale):
    c = scale * LOG2_E
    m = states[0][0]
    for st in states[1:]:
        m = jnp.maximum(m, st[0])
    acc = jnp.exp2((states[0][0] - m) * c) * states[0][1]
    for st in states[1:]:
        acc = acc + jnp.exp2((st[0] - m) * c) * st[1]
    return m, acc


def _flash_finish_t(carry):
    acc = carry[1]
    return acc[:HALF] / jnp.maximum(acc[HALF:HALF + 1], 1e-30)


def _proj_kernel(x_ref, gain_ref, w_ref, wt_ref, cos_a_ref, sin_a_ref, cos_at_ref, sin_at_ref,
                 cos_b_ref, sin_b_ref, cos_bt_ref, sin_bt_ref, qn_ref, wq2t_ref, kvn_ref, wk2_ref, wv2t_ref,
                 q_nsa_ref, k_sel_ref, k_win_ref, kc_ref, vc_ref, vt_sel_ref, vt_win_ref, gate_ref,
                 q_mla_ref, k_mla_ref, vt_mla_ref):
    rows = x_ref.shape[1]
    hn = _rms(x_ref[0], gain_ref[...]).astype(MXU_DTYPE)
    res = _dot(hn, w_ref[...])
    feat_t = _dot_nt(wt_ref[...], hn)
    ones_rows = _ones_row_block(rows)

    cos_at, sin_at = cos_at_ref[...], sin_at_ref[...]
    scale = NSA_HEAD_DIM ** -0.5
    for h in range(NSA_HEADS):
        lo = h * NSA_HEAD_DIM
        q_t = feat_t[lo:lo + NSA_HEAD_DIM] * cos_at + feat_t[_FT_QROT + lo:_FT_QROT + lo + NSA_HEAD_DIM] * sin_at
        q_nsa_ref[0, h] = (q_t * scale).astype(q_nsa_ref.dtype)
    for off, out_ref in ((_FT_VSEL, vt_sel_ref), (_FT_VWIN, vt_win_ref)):
        tile = out_ref.shape[4]
        for g in range(NSA_KV_GROUPS):
            lo = off + g * NSA_HEAD_DIM
            v_t = jnp.concatenate([feat_t[lo:lo + NSA_HEAD_DIM], ones_rows], axis=0)
            for j in range(rows // tile):
                out_ref[0, g, j] = v_t[:, j * tile:(j + 1) * tile].astype(out_ref.dtype)
    gate_ref[0] = jax.nn.sigmoid(feat_t[_FT_GATE:_FT_GATE + _GATE_ROWS])

    cos_a, sin_a = cos_a_ref[...], sin_a_ref[...]
    lane = lax.broadcasted_iota(jnp.int32, (rows, LANES), 1)
    lower = lane < HALF

    def rope_blk(j):
        return (res[:, j * LANES:(j + 1) * LANES] * cos_a
                + res[:, _OFF_ROT + j * LANES:_OFF_ROT + (j + 1) * LANES] * sin_a)

    def split_pair(blk):
        return jnp.where(lower, blk, 0.0), jnp.where(lower, pltpu.roll(blk, HALF, 1), 0.0)

    kc_ref[0] = rope_blk(0)
    vc_ref[0] = res[:, _OFF_VC:_OFF_VC + LANES]
    pos = pl.program_id(1) * rows + lax.broadcasted_iota(jnp.int32, (rows, LANES), 0)
    onehot = jnp.where((lane - HALF) == jnp.right_shift(pos, SEL_BLOCK_LOG2), 1.0, 0.0)
    a, b = split_pair(rope_blk(1))
    k_sel_ref[0, 0] = (a + onehot).astype(k_sel_ref.dtype)
    k_sel_ref[0, 1] = (b + onehot).astype(k_sel_ref.dtype)
    a, b = split_pair(rope_blk(2))
    k_win_ref[0, 0] = a.astype(k_win_ref.dtype)
    k_win_ref[0, 1] = b.astype(k_win_ref.dtype)

    cos_b, sin_b = cos_b_ref[...], sin_b_ref[...]
    cos_bt, sin_bt = cos_bt_ref[...], sin_bt_ref[...]
    cqn = _rms(res[:, _OFF_CQ:_OFF_CQ + Q_LORA_RANK], qn_ref[...]).astype(MXU_DTYPE)
    yq_t = _dot_nt(wq2t_ref[...], cqn)
    rot_off = MLA_HEADS * LANES
    for h in range(MLA_HEADS):
        q_t = (yq_t[h * LANES:(h + 1) * LANES] * cos_bt
               + yq_t[rot_off + h * LANES:rot_off + (h + 1) * LANES] * sin_bt)
        q_mla_ref[0, h] = q_t.astype(q_mla_ref.dtype)
    ckvn = _rms(res[:, _OFF_CKV:_OFF_CKV + KV_LORA_RANK], kvn_ref[...]).astype(MXU_DTYPE)
    yk = _dot(ckvn, wk2_ref[...])
    k_pe = res[:, _OFF_KRA:_OFF_KRA + LANES] * cos_b + res[:, _OFF_KRB:_OFF_KRB + LANES] * sin_b
    for h in range(MLA_HEADS):
        k_mla_ref[0, h] = (yk[:, h * LANES:(h + 1) * LANES] + k_pe).astype(k_mla_ref.dtype)
    yv_t = _dot_nt(wv2t_ref[...], ckvn)
    tk = vt_mla_ref.shape[4]
    for h in range(MLA_HEADS):
        v_t = jnp.concatenate([yv_t[h * MLA_V_DIM:(h + 1) * MLA_V_DIM], ones_rows], axis=0)
        for j in range(rows // tk):
            vt_mla_ref[0, h, j] = v_t[:, j * tk:(j + 1) * tk].astype(vt_mla_ref.dtype)


def _proj_call(x, gain, w_pm, w_fm, tabs, qn, wq2t, kvn, wk2, wv2t):
    B, S, D = x.shape
    rows = min(PROJ_ROWS, S)
    ns = S // rows
    tk = min(ATTN_K_TILE, S)
    tw = min(WIN_V_TILE, S)
    const = lambda b, i: (0, 0)
    tab = lambda t: (pl.BlockSpec((rows, t.shape[1]), lambda b, i: (i, 0)) if t.shape[0] == S
                     else pl.BlockSpec((t.shape[0], rows), lambda b, i: (0, i)))
    hm = lambda n: pl.BlockSpec((1, n, rows, LANES), lambda b, i: (b, 0, i, 0))
    fm = lambda n, f: pl.BlockSpec((1, n, f, rows), lambda b, i: (b, 0, 0, i))
    tiles = lambda n, t: pl.BlockSpec((1, n, rows // t, LANES, t), lambda b, i: (b, 0, i, 0, 0))
    flat = pl.BlockSpec((1, rows, LANES), lambda b, i: (b, i, 0))
    sds = jax.ShapeDtypeStruct
    G = NSA_KV_GROUPS
    return pl.pallas_call(
        _proj_kernel,
        grid=(B, ns),
        in_specs=[
            pl.BlockSpec((1, rows, D), lambda b, i: (b, i, 0)),
            pl.BlockSpec((1, D), const),
            pl.BlockSpec(w_pm.shape, const),
            pl.BlockSpec(w_fm.shape, const),
            *[tab(t) for t in tabs],
            pl.BlockSpec((1, Q_LORA_RANK), const),
            pl.BlockSpec(wq2t.shape, const),
            pl.BlockSpec((1, KV_LORA_RANK), const),
            pl.BlockSpec(wk2.shape, const),
            pl.BlockSpec(wv2t.shape, const),
        ],
        out_specs=[fm(NSA_HEADS, NSA_HEAD_DIM), hm(G), hm(G), flat, flat, tiles(G, tk), tiles(G, tw),
                   pl.BlockSpec((1, _GATE_ROWS, rows), lambda b, i: (b, 0, i)),
                   fm(MLA_HEADS, LANES), hm(MLA_HEADS), tiles(MLA_HEADS, tk)],
        out_shape=[
            sds((B, NSA_HEADS, NSA_HEAD_DIM, S), MXU_DTYPE),
            sds((B, G, S, LANES), MXU_DTYPE),
            sds((B, G, S, LANES), MXU_DTYPE),
            sds((B, S, LANES), jnp.float32),
            sds((B, S, LANES), jnp.float32),
            sds((B, G, S // tk, LANES, tk), MXU_DTYPE),
            sds((B, G, S // tw, LANES, tw), MXU_DTYPE),
            sds((B, _GATE_ROWS, S), jnp.float32),
            sds((B, MLA_HEADS, LANES, S), MXU_DTYPE),
            sds((B, MLA_HEADS, S, LANES), MXU_DTYPE),
            sds((B, MLA_HEADS, S // tk, LANES, tk), MXU_DTYPE),
        ],
        compiler_params=pltpu.CompilerParams(
            dimension_semantics=("parallel", "parallel"), vmem_limit_bytes=VMEM_LIMIT),
        name="proj",
    )(x, gain, w_pm, w_fm, *tabs, qn, wq2t, kvn, wk2, wv2t)


def _compress_kernel(xk_ref, xv_ref, posk_ref, posv_ref, w1k_ref, w1v_ref, w2k_ref, w2vt_ref,
                     kcmp_ref, vtcmp_ref):
    n_chunk = xk_ref.shape[1]

    def hidden(x_ref, pos_ref, w1_ref):
        x = x_ref[0]
        h_lo = _dot((x + pos_ref[0:1, :]).astype(MXU_DTYPE), w1_ref[0])
        h_hi = _dot((x + pos_ref[1:2, :]).astype(MXU_DTYPE), w1_ref[1])
        hid = h_lo + pltpu.roll(h_hi, n_chunk - 1, 0)
        return (hid * jax.nn.sigmoid(hid)).astype(MXU_DTYPE)

    k_out = _dot(hidden(xk_ref, posk_ref, w1k_ref), w2k_ref[...])
    v_out_t = _dot_nt(w2vt_ref[...], hidden(xv_ref, posv_ref, w1v_ref))
    for g in range(NSA_KV_GROUPS):
        kcmp_ref[0, g] = k_out[:, g * LANES:(g + 1) * LANES].astype(kcmp_ref.dtype)
        vtcmp_ref[0, g] = v_out_t[g * LANES:(g + 1) * LANES].astype(vtcmp_ref.dtype)


def _compress_call(xk, xv, posk, posv, w1k, w1v, w2k, w2vt):
    B, n_chunk, width = xk.shape
    xs = pl.BlockSpec((1, n_chunk, width), lambda b: (b, 0, 0))
    c2 = lambda a: pl.BlockSpec(a.shape, lambda b: (0,) * a.ndim)
    G = NSA_KV_GROUPS
    return pl.pallas_call(
        _compress_kernel,
        grid=(B,),
        in_specs=[xs, xs, c2(posk), c2(posv), c2(w1k), c2(w1v), c2(w2k), c2(w2vt)],
        out_specs=[pl.BlockSpec((1, G, n_chunk, LANES), lambda b: (b, 0, 0, 0)),
                   pl.BlockSpec((1, G, LANES, n_chunk), lambda b: (b, 0, 0, 0))],
        out_shape=[jax.ShapeDtypeStruct((B, G, n_chunk, LANES), MXU_DTYPE),
                   jax.ShapeDtypeStruct((B, G, LANES, n_chunk), MXU_DTYPE)],
        compiler_params=pltpu.CompilerParams(
            dimension_semantics=("parallel",), vmem_limit_bytes=VMEM_LIMIT),
        name="compress",
    )(xk, xv, posk, posv, w1k, w1v, w2k, w2vt)


def _nsa_kernel(qt_ref, kcmp_ref, vtcmp_ref, ksel_ref, vtsel_ref, kwin_ref, vtwin_ref, gate_ref,
                imp_ref, o_ref, *, n_sel, win_keys, sub):
    g = pl.program_id(1)
    tq = qt_ref.shape[3]
    n_sub = tq // sub
    n_chunk = kcmp_ref.shape[2]
    tk = vtsel_ref.shape[4]
    tw = vtwin_ref.shape[4]
    lanes = NSA_REP * sub
    tile0 = pl.program_id(2) * tq
    m_t = imp_ref[...]
    blk8 = lax.broadcasted_iota(jnp.int32, (8, sub), 0)
    lane_pos = jnp.bitwise_and(lax.broadcasted_iota(jnp.int32, (1, lanes), 1), sub - 1)

    def front(u):
        s0 = tile0 + u * sub
        q_t = jnp.concatenate([qt_ref[0, r, :, u * sub:(u + 1) * sub] for r in range(NSA_REP)], axis=1)
        q_t0 = jnp.concatenate([q_t, jnp.zeros_like(q_t)], axis=0)
        t_row = s0 + lane_pos

        s = _dot(kcmp_ref[0, 0], q_t0)
        cmp_end = lax.broadcasted_iota(jnp.int32, (n_chunk, lanes), 0) * CMP_STRIDE + (CMP_BLOCK - 1)
        mask = cmp_end <= t_row
        s = jnp.where(mask, s, NEG_INF)
        e = jnp.where(mask, jnp.exp(s - jnp.max(s, axis=0, keepdims=True)), 0.0)
        p_cmp = e / jnp.maximum(jnp.sum(e, axis=0, keepdims=True), 1e-30)
        o_cmp = _dot(vtcmp_ref[0, 0], p_cmp.astype(MXU_DTYPE))[:HALF]

        p_sum = p_cmp[:, 0:sub]
        for r in range(1, NSA_REP):
            p_sum = p_sum + p_cmp[:, r * sub:(r + 1) * sub]
        hi = p_sum.astype(MXU_DTYPE)
        rem = p_sum - hi.astype(jnp.float32)
        mid = rem.astype(MXU_DTYPE)
        lo = (rem - mid.astype(jnp.float32)).astype(MXU_DTYPE)
        imp_t = _dot(m_t, hi) + _dot(m_t, mid) + _dot(m_t, lo)

        blk = lax.broadcasted_iota(jnp.int32, (MAX_SEL_BLOCKS, sub), 0)
        cur = jnp.right_shift(s0 + lax.broadcasted_iota(jnp.int32, (MAX_SEL_BLOCKS, sub), 1), SEL_BLOCK_LOG2)
        valid = blk <= cur
        forced = valid & ((blk == 0) | (blk > cur - N_LOCAL))
        score = jnp.where(forced, FORCE_BONUS, jnp.where(valid, imp_t, -1.0))
        n_grp = MAX_SEL_BLOCKS // 8
        grp = [score[8 * a:8 * (a + 1)] for a in range(n_grp)]
        rank = [jnp.zeros((8, sub), jnp.float32) for _ in range(n_grp)]
        for j in range(MAX_SEL_BLOCKS):
            row = score[j:j + 1, :]
            for a in range(n_grp):
                if a > j // 8:
                    beats = jnp.where(row >= grp[a], 1.0, 0.0)
                elif a < j // 8:
                    beats = jnp.where(row > grp[a], 1.0, 0.0)
                else:
                    beats = jnp.where(blk8 > j % 8, jnp.where(row >= grp[a], 1.0, 0.0),
                                      jnp.where(row > grp[a], 1.0, 0.0))
                rank[a] = rank[a] + beats
        bias_t = jnp.where(jnp.concatenate(rank, axis=0) < n_sel, 0.0, NEG_INF).astype(MXU_DTYPE)
        q_aug = jnp.concatenate([q_t, jnp.concatenate([bias_t] * NSA_REP, axis=1)], axis=0)

        w_start = pl.multiple_of(jnp.maximum(s0 - WINDOW, 0), sub)
        s = _dot(kwin_ref[0, 0, pl.ds(w_start, win_keys), :], q_t0)
        key = w_start + lax.broadcasted_iota(jnp.int32, (win_keys, lanes), 0)
        mask = (key <= t_row) & (key > t_row - WINDOW)
        s = jnp.where(mask, s, NEG_INF)
        e = jnp.where(mask, jnp.exp(s - jnp.max(s, axis=0, keepdims=True)), 0.0)
        w_tile = w_start // tw
        v_t = jnp.concatenate([vtwin_ref[0, 0, w_tile + j] for j in range(win_keys // tw)], axis=1)
        o_win = _flash_finish_t((None, _dot(v_t, e.astype(MXU_DTYPE))))
        return q_aug, t_row, o_cmp, o_win

    fronts = [front(u) for u in range(n_sub)]

    split = ATTN_K_SPLIT
    chains = [(u, j) for u in range(n_sub) for j in range(split)]

    def sel_step(it, state, causal):
        s_ts = []
        for u, j in chains:
            start = pl.multiple_of((it * split + j) * tk, tk)
            s_ts.append(_dot(ksel_ref[0, 0, pl.ds(start, tk), :], fronts[u][0]))
        if causal:
            for c, (u, j) in enumerate(chains):
                key = (it * split + j) * tk + lax.broadcasted_iota(jnp.int32, (tk, lanes), 0)
                s_ts[c] = jnp.where(key <= fronts[u][1], s_ts[c], NEG_INF)
        return tuple(_flash_update_t(state[c], s_ts[c], vtsel_ref[0, 0, it * split + j], 1.0)
                     for c, (u, j) in enumerate(chains))

    n_full = tile0 // (split * tk)
    carry = lax.fori_loop(0, n_full, lambda it, c: sel_step(it, c, False),
                          tuple(_flash_init_t(lanes) for _ in chains))
    carry = sel_step(n_full, carry, True)

    for u in range(n_sub):
        _, _, o_cmp, o_win = fronts[u]
        o_sel = _flash_finish_t(_flash_merge_t(carry[u * split:(u + 1) * split], 1.0))
        gates = gate_ref[0, :, u * sub:(u + 1) * sub]

        def gate(r, br, gates=gates):
            r0 = r * NSA_BRANCHES + br
            r1 = (NSA_REP + r) * NSA_BRANCHES + br
            return jnp.where(g == 0, gates[r0:r0 + 1], gates[r1:r1 + 1])

        heads = []
        for r in range(NSA_REP):
            sl = slice(r * sub, (r + 1) * sub)
            heads.append(gate(r, 0) * o_cmp[:, sl] + gate(r, 1) * o_sel[:, sl] + gate(r, 2) * o_win[:, sl])
        for pair in range(NSA_REP // 2):
            o_ref[0, u * sub:(u + 1) * sub, pair * LANES:(pair + 1) * LANES] = jnp.concatenate(
                [heads[2 * pair], heads[2 * pair + 1]], axis=0).T


def _nsa_call(qt_nsa, k_cmp, vt_cmp, k_sel, vt_sel, k_win, vt_win, gates_t, imp_mat):
    B, _, _, S = qt_nsa.shape
    n_chunk = k_cmp.shape[2]
    tq = min(NSA_Q_TILE, S)
    sub = min(NSA_SUB_TILE, tq)
    tk, tw = vt_sel.shape[4], vt_win.shape[4]
    assert (ATTN_K_SPLIT * tk) % tq == 0 and tq % sub == 0 and sub == LANES and sub % tw == 0
    n_sel = min(N_SELECT, S // SEL_BLOCK)
    win_keys = min(WINDOW + sub, S)
    kv = lambda n: pl.BlockSpec((1, 1, n, LANES), lambda b, g, i: (b, g, 0, 0))
    vt = lambda a: pl.BlockSpec((1, 1) + a.shape[2:], lambda b, g, i: (b, g) + (0,) * (a.ndim - 2))
    return pl.pallas_call(
        functools.partial(_nsa_kernel, n_sel=n_sel, win_keys=win_keys, sub=sub),
        grid=(B, NSA_KV_GROUPS, S // tq),
        in_specs=[
            pl.BlockSpec((1, NSA_REP, NSA_HEAD_DIM, tq), lambda b, g, i: (b, g, 0, i)),
            kv(n_chunk), vt(vt_cmp), kv(S), vt(vt_sel), kv(S), vt(vt_win),
            pl.BlockSpec((1, _GATE_ROWS, tq), lambda b, g, i: (b, 0, i)),
            pl.BlockSpec(imp_mat.shape, lambda b, g, i: (0, 0)),
        ],
        out_specs=pl.BlockSpec((1, tq, NSA_REP * NSA_HEAD_DIM), lambda b, g, i: (b, i, g)),
        out_shape=jax.ShapeDtypeStruct((B, S, NSA_WIDTH), jnp.float32),
        compiler_params=pltpu.CompilerParams(
            dimension_semantics=("parallel", "parallel", "arbitrary"), vmem_limit_bytes=VMEM_LIMIT),
        name="nsa",
    )(qt_nsa, k_cmp, vt_cmp, k_sel, vt_sel, k_win, vt_win, gates_t, imp_mat)


def _mla_kernel(qt_ref, k_ref, vt_ref, o_ref):
    n_heads = qt_ref.shape[1]
    tq = qt_ref.shape[3]
    tk = vt_ref.shape[4]
    q0 = pl.program_id(2) * tq
    scale = (MLA_NOPE_DIM + MLA_ROPE_DIM) ** -0.5
    t_row = q0 + lax.broadcasted_iota(jnp.int32, (1, tq), 1)
    qts = [qt_ref[0, h] for h in range(n_heads)]
    split = tq // tk
    chains = [(h, j) for h in range(n_heads) for j in range(split)]

    def step(it, state, causal):
        s_ts = []
        for h, j in chains:
            start = pl.multiple_of((it * split + j) * tk, tk)
            s_ts.append(_dot(k_ref[0, h, pl.ds(start, tk), :], qts[h]))
        if causal:
            for c, (h, j) in enumerate(chains):
                key = (it * split + j) * tk + lax.broadcasted_iota(jnp.int32, (tk, tq), 0)
                s_ts[c] = jnp.where(key <= t_row, s_ts[c], NEG_INF)
        return tuple(_flash_update_t(state[c], s_ts[c], vt_ref[0, h, it * split + j], scale)
                     for c, (h, j) in enumerate(chains))

    n_full = pl.program_id(2)
    carry = lax.fori_loop(0, n_full, lambda it, c: step(it, c, False),
                          tuple(_flash_init_t(tq) for _ in chains))
    carry = step(n_full, carry, True)
    outs = [_flash_finish_t(_flash_merge_t(carry[h * split:(h + 1) * split], scale)) for h in range(n_heads)]
    for p in range(n_heads // 2):
        o_ref[0, :, p * LANES:(p + 1) * LANES] = jnp.concatenate([outs[2 * p], outs[2 * p + 1]], axis=0).T


def _mla_call(qt_mla, k_mla, vt_mla):
    B, _, _, S = qt_mla.shape
    tq = min(MLA_TILE, S)
    tk = vt_mla.shape[4]
    nh = MLA_HEADS_PER_STEP
    assert tq % tk == 0 and MLA_HEADS % nh == 0 and nh % 2 == 0
    return pl.pallas_call(
        _mla_kernel,
        grid=(B, MLA_HEADS // nh, S // tq),
        in_specs=[
            pl.BlockSpec((1, nh, LANES, tq), lambda b, p, i: (b, p, 0, i)),
            pl.BlockSpec((1, nh, S, LANES), lambda b, p, i: (b, p, 0, 0)),
            pl.BlockSpec((1, nh, S // tk, LANES, tk), lambda b, p, i: (b, p, 0, 0, 0)),
        ],
        out_specs=pl.BlockSpec((1, tq, nh * MLA_V_DIM), lambda b, p, i: (b, i, p)),
        out_shape=jax.ShapeDtypeStruct((B, S, MLA_WIDTH), jnp.float32),
        compiler_params=pltpu.CompilerParams(
            dimension_semantics=("parallel", "parallel", "arbitrary"), vmem_limit_bytes=VMEM_LIMIT),
        name="mla",
    )(qt_mla, k_mla, vt_mla)


def _mix_mlp_kernel(x_ref, oa_ref, ob_ref, na_ref, nb_ref, woa_ref, wob_ref, mn_ref, w1_ref, w2_ref,
                    fn_ref, o_ref, x1_ref, hn_ref, acc_ref, *, final):
    f = pl.program_id(1)

    @pl.when(f == 0)
    def _():
        na = _rms(oa_ref[...], na_ref[...]).astype(MXU_DTYPE)
        nb = _rms(ob_ref[...], nb_ref[...]).astype(MXU_DTYPE)
        x1 = x_ref[...] + (_dot(na, woa_ref[...]) + _dot(nb, wob_ref[...]))
        x1_ref[...] = x1
        hn_ref[...] = _rms(x1, mn_ref[...]).astype(MXU_DTYPE)
        acc_ref[...] = jnp.zeros_like(acc_ref)

    a = jnp.square(jnp.maximum(_dot(hn_ref[...], w1_ref[...]), 0.0))
    acc_ref[...] += _dot(a.astype(MXU_DTYPE), w2_ref[...])

    @pl.when(f == pl.num_programs(1) - 1)
    def _():
        y = x1_ref[...] + acc_ref[...]
        o_ref[...] = _rms(y, fn_ref[...]) if final else y


def _mix_mlp_call(x, o_a, o_b, na, nb, wo_a, wo_b, mn, w1, w2, fn, final):
    T, D = x.shape
    rows = min(MLP_ROWS, T)
    tf = MLP_FF_TILE
    row = lambda w: pl.BlockSpec((rows, w), lambda i, f: (i, 0))
    const = lambda a: pl.BlockSpec(a.shape, lambda i, f: (0, 0))
    return pl.pallas_call(
        functools.partial(_mix_mlp_kernel, final=final),
        grid=(T // rows, D_FF // tf),
        in_specs=[row(D), row(NSA_WIDTH), row(MLA_WIDTH), const(na), const(nb), const(wo_a), const(wo_b),
                  const(mn),
                  pl.BlockSpec((D, tf), lambda i, f: (0, f)),
                  pl.BlockSpec((tf, D), lambda i, f: (f, 0)),
                  const(fn)],
        out_specs=row(D),
        out_shape=jax.ShapeDtypeStruct((T, D), jnp.float32),
        scratch_shapes=[pltpu.VMEM((rows, D), jnp.float32), pltpu.VMEM((rows, D), MXU_DTYPE),
                        pltpu.VMEM((rows, D), jnp.float32)],
        compiler_params=pltpu.CompilerParams(
            dimension_semantics=("parallel", "arbitrary"), vmem_limit_bytes=VMEM_LIMIT),
        name="mix_mlp",
    )(x, o_a, o_b, na, nb, wo_a, wo_b, mn, w1, w2, fn)


def _rot_cols(w, dim):
    lead = w.shape[:-1]
    w4 = w.reshape(lead + (w.shape[-1] // dim, 2, dim // 2))
    return jnp.concatenate([-w4[..., 1, :], w4[..., 0, :]], axis=-1).reshape(w.shape)


def _pad_cols(w, left, total):
    pad = [(0, 0)] * (w.ndim - 1) + [(left, total - left - w.shape[-1])]
    return jnp.pad(w, pad)


def _pack_in_proj(w_in):
    sizes = (NSA_WIDTH,) + (NSA_KV_GROUPS * NSA_HEAD_DIM,) * 6 + (
        NSA_BRANCHES * NSA_HEADS, Q_LORA_RANK, KV_LORA_RANK, MLA_ROPE_DIM)
    offs = np.cumsum((0,) + sizes)
    q_a, k_c, v_c, k_s, v_s, k_w, v_w, g_a, c_q, c_kv, k_r = (
        w_in[..., offs[i]:offs[i + 1]] for i in range(len(sizes)))
    roped = jnp.concatenate([k_c, k_s, k_w], axis=-1)
    w_pm = jnp.concatenate(
        [roped, _rot_cols(roped, NSA_HEAD_DIM), v_c, c_q, c_kv,
         _pad_cols(k_r, HALF, LANES), _pad_cols(_rot_cols(k_r, MLA_ROPE_DIM), HALF, LANES)], axis=-1)
    w_fm = jnp.concatenate([q_a, _rot_cols(q_a, NSA_HEAD_DIM), v_s, v_w, _pad_cols(g_a, 0, _GATE_ROWS)], axis=-1)
    assert w_pm.shape[-1] == _PROJ_COLS and w_fm.shape[-1] == _FT_ROWS
    return w_pm.astype(MXU_DTYPE), jnp.swapaxes(w_fm, 1, 2).astype(MXU_DTYPE)


def _pack_q_up(w_q_up):
    L, R, _ = w_q_up.shape
    w = w_q_up.reshape(L, R, MLA_HEADS, MLA_NOPE_DIM + MLA_ROPE_DIM)
    a = _pad_cols(w, 0, LANES)
    b = _pad_cols(_rot_cols(w[..., MLA_NOPE_DIM:], MLA_ROPE_DIM), MLA_NOPE_DIM, LANES)
    packed = jnp.concatenate([a.reshape(L, R, -1), b.reshape(L, R, -1)], axis=-1)
    return jnp.swapaxes(packed, 1, 2).astype(MXU_DTYPE)


def _pack_kv_up(w_kv_up):
    L, R, _ = w_kv_up.shape
    w = w_kv_up.reshape(L, R, MLA_HEADS, MLA_NOPE_DIM + MLA_V_DIM)
    k = _pad_cols(w[..., :MLA_NOPE_DIM], 0, LANES).reshape(L, R, -1)
    v_t = jnp.swapaxes(w[..., MLA_NOPE_DIM:].reshape(L, R, -1), 1, 2)
    return k.astype(MXU_DTYPE), v_t.astype(MXU_DTYPE)


def _pack_compress(pos, w1, w2):
    L = pos.shape[0]
    G, dk, H = NSA_KV_GROUPS, NSA_HEAD_DIM, CMP_HIDDEN
    pos2 = jnp.concatenate([pos] * G, axis=-1).reshape(L, 2, CMP_STRIDE * G * dk)
    w1r = w1.reshape(L, 2, CMP_STRIDE, dk, H)
    w1b = jnp.zeros((L, 2, CMP_STRIDE, G, dk, G, H), w1.dtype)
    for g in range(G):
        w1b = w1b.at[:, :, :, g, :, g, :].set(w1r)
    w1b = w1b.reshape(L, 2, CMP_STRIDE * G * dk, G * H)
    w2b = jnp.zeros((L, G, H, G, LANES), w2.dtype)
    for g in range(G):
        w2b = w2b.at[:, g, :, g, :dk].set(w2)
    return pos2, w1b.astype(MXU_DTYPE), w2b.reshape(L, G * H, G * LANES).astype(MXU_DTYPE)


def _importance_matrix(n_chunk):
    a, b = SEL_BLOCK // CMP_STRIDE, CMP_BLOCK // CMP_STRIDE
    overlap = np.convolve(np.ones(a), np.ones(b))
    n_cmp = n_chunk - (b - 1)
    m = np.zeros((MAX_SEL_BLOCKS, n_chunk), np.float32)
    for blk in range(n_chunk * CMP_STRIDE // SEL_BLOCK):
        for j, wgt in enumerate(overlap):
            i = a * blk + j - (b - 1)
            if 0 <= i < n_cmp:
                m[blk, i] = wgt
    return jnp.asarray(m, MXU_DTYPE)


def _rope_tables(seq, dim):
    inv_freq = 1.0 / (ROPE_THETA ** (jnp.arange(0, dim, 2, dtype=jnp.float32) / dim))
    ang = jnp.arange(seq, dtype=jnp.float32)[:, None] * inv_freq[None, :]
    ang = jnp.concatenate([ang, ang], axis=-1)
    return jnp.cos(ang), jnp.sin(ang)


def kernel(x, attn_norm, w_in, cmp_pos_k, cmp_w1_k, cmp_w2_k, cmp_pos_v, cmp_w1_v, cmp_w2_v, mla_q_norm,
           w_q_up, mla_kv_norm, w_kv_up, nsa_out_norm, mla_out_norm, w_out, mlp_norm, w_ff1, w_ff2, final_norm):
    B, S, D = x.shape
    depth = w_in.shape[0]
    assert S % SEL_BLOCK == 0 and S // SEL_BLOCK <= MAX_SEL_BLOCKS and S % CMP_STRIDE == 0
    n_chunk = S // CMP_STRIDE

    cos64, sin64 = _rope_tables(S, NSA_HEAD_DIM)
    cos_a, sin_a = jnp.concatenate([cos64, cos64], -1), jnp.concatenate([sin64, sin64], -1)
    cos32, sin32 = _rope_tables(S, MLA_ROPE_DIM)
    pad = jnp.zeros((S, LANES - MLA_NOPE_DIM - MLA_ROPE_DIM), jnp.float32)
    cos_b = jnp.concatenate([jnp.ones((S, MLA_NOPE_DIM), jnp.float32), cos32, pad], -1)
    sin_b = jnp.concatenate([jnp.zeros((S, MLA_NOPE_DIM), jnp.float32), sin32, pad], -1)
    tabs = (cos_a, sin_a, cos64.T, sin64.T, cos_b, sin_b, cos_b.T, sin_b.T)

    w_pm, w_fm = _pack_in_proj(w_in)
    wq2t = _pack_q_up(w_q_up)
    wk2, wv2t = _pack_kv_up(w_kv_up)
    posk, w1k, w2k = _pack_compress(cmp_pos_k, cmp_w1_k, cmp_w2_k)
    posv, w1v, w2v = _pack_compress(cmp_pos_v, cmp_w1_v, cmp_w2_v)
    w2vt = jnp.swapaxes(w2v, 1, 2)
    imp_mat = _importance_matrix(n_chunk)
    wo = w_out.astype(MXU_DTYPE)
    w1 = w_ff1.astype(MXU_DTYPE)
    w2 = w_ff2.astype(MXU_DTYPE)
    row = lambda v: v.reshape(1, -1)

    for l in range(depth):
        (qt_nsa, k_sel, k_win, k_c, v_c, vt_sel, vt_win, gates_t, qt_mla, k_mla, vt_mla) = _proj_call(
            x, row(attn_norm[l]), w_pm[l], w_fm[l], tabs,
            row(mla_q_norm[l]), wq2t[l], row(mla_kv_norm[l]), wk2[l], wv2t[l])
        k_cmp, vt_cmp = _compress_call(
            k_c.reshape(B, n_chunk, CMP_STRIDE * LANES), v_c.reshape(B, n_chunk, CMP_STRIDE * LANES),
            posk[l], posv[l], w1k[l], w1v[l], w2k[l], w2vt[l])
        o_a = _nsa_call(qt_nsa, k_cmp, vt_cmp, k_sel, vt_sel, k_win, vt_win, gates_t, imp_mat)
        o_b = _mla_call(qt_mla, k_mla, vt_mla)
        x = _mix_mlp_call(
            x.reshape(B * S, D), o_a.reshape(B * S, NSA_WIDTH), o_b.reshape(B * S, MLA_WIDTH),
            row(nsa_out_norm[l]), row(mla_out_norm[l]), wo[l, :NSA_WIDTH], wo[l, NSA_WIDTH:],
            row(mlp_norm[l]), w1[l], w2[l], row(final_norm), final=(l == depth - 1)).reshape(B, S, D)
    return x
```

```python
import functools

import numpy as np
import jax
import jax.numpy as jnp
from jax import lax
from jax.experimental import pallas as pl
from jax.experimental.pallas import tpu as pltpu

D_MODEL = 1024
NSA_HEADS = 8
NSA_KV_GROUPS = 2
NSA_REP = NSA_HEADS // NSA_KV_GROUPS
NSA_HEAD_DIM = 64
NSA_BRANCHES = 3
CMP_BLOCK = 32
CMP_STRIDE = 16
CMP_HIDDEN = 256
SEL_BLOCK = 64
SEL_BLOCK_LOG2 = 6
N_SELECT = 16
N_LOCAL = 2
WINDOW = 512
MLA_HEADS = 8
MLA_NOPE_DIM = 64
MLA_ROPE_DIM = 32
MLA_V_DIM = 64
Q_LORA_RANK = 256
KV_LORA_RANK = 128
D_FF = 4 * D_MODEL
ROPE_THETA = 10000.0
NORM_EPS = 1e-6
NEG_INF = -1e30
FORCE_BONUS = 1e4
LOG2_E = 1.4426950408889634
NSA_WIDTH = NSA_HEADS * NSA_HEAD_DIM
MLA_WIDTH = MLA_HEADS * MLA_V_DIM

LANES = 128
HALF = LANES // 2
MAX_SEL_BLOCKS = HALF
MXU_DTYPE = jnp.bfloat16
VMEM_LIMIT = 52 * 1024 * 1024
PROJ_ROWS = 512
ATTN_K_TILE = 256
ATTN_K_SPLIT = 2
NSA_Q_TILE = 256
NSA_SUB_TILE = 128
WIN_V_TILE = 128
MLA_TILE = ATTN_K_TILE * ATTN_K_SPLIT
MLA_HEADS_PER_STEP = 4
MLP_ROWS = 512
MLP_FF_TILE = 2048

_N_ROPE_BLK = 3
_OFF_ROT = _N_ROPE_BLK * LANES
_OFF_VC = 2 * _OFF_ROT
_OFF_CQ = _OFF_VC + LANES
_OFF_CKV = _OFF_CQ + Q_LORA_RANK
_OFF_KRA = _OFF_CKV + KV_LORA_RANK
_OFF_KRB = _OFF_KRA + LANES
_PROJ_COLS = _OFF_KRB + LANES
_FT_QROT = NSA_WIDTH
_FT_VSEL = 2 * NSA_WIDTH
_FT_VWIN = _FT_VSEL + NSA_KV_GROUPS * NSA_HEAD_DIM
_FT_GATE = _FT_VWIN + NSA_KV_GROUPS * NSA_HEAD_DIM
_GATE_ROWS = 32
_FT_ROWS = _FT_GATE + _GATE_ROWS


def _dot(a, b):
    return jnp.dot(a, b, preferred_element_type=jnp.float32)


def _dot_nt(a, b):
    return lax.dot_general(a, b, (((1,), (1,)), ((), ())), preferred_element_type=jnp.float32)


def _rms(x, gain):
    return x * lax.rsqrt(jnp.mean(x * x, axis=-1, keepdims=True) + NORM_EPS) * gain


def _ones_row_block(n):
    return jnp.where(lax.broadcasted_iota(jnp.int32, (HALF, n), 0) == 0, 1.0, 0.0)


def _flash_update_t(carry, s_t, v_t, scale):
    m, acc = carry
    c = scale * LOG2_E
    m_new = jnp.maximum(m, jnp.max(s_t, axis=0, keepdims=True))
    alpha = jnp.exp2((m - m_new) * c)
    p_t = jnp.exp2((s_t - m_new).astype(MXU_DTYPE) * c)
    return m_new, alpha * acc + _dot(v_t, p_t)


def _flash_init_t(n):
    return jnp.full((1, n), NEG_INF, jnp.float32), jnp.zeros((LANES, n), jnp.float32)


def _flash_merge_t(states, scale):
    c = scale * LOG2_E
    m = states[0][0]
    for st in states[1:]:
        m = jnp.maximum(m, st[0])
    acc = jnp.exp2((states[0][0] - m) * c) * states[0][1]
    for st in states[1:]:
        acc = acc + jnp.exp2((st[0] - m) * c) * st[1]
    return m, acc


def _flash_finish_t(carry):
    acc = carry[1]
    return acc[:HALF] / jnp.maximum(acc[HALF:HALF + 1], 1e-30)


def _proj_kernel(x_ref, gain_ref, w_ref, wt_ref, cos_a_ref, sin_a_ref, cos_at_ref, sin_at_ref,
                 cos_b_ref, sin_b_ref, cos_bt_ref, sin_bt_ref, qn_ref, wq2t_ref, kvn_ref, wk2_ref, wv2t_ref,
                 q_nsa_ref, k_sel_ref, k_win_ref, kc_ref, vc_ref, vt_sel_ref, vt_win_ref, gate_ref,
                 q_mla_ref, k_mla_ref, vt_mla_ref):
    rows = x_ref.shape[1]
    hn = _rms(x_ref[0], gain_ref[...]).astype(MXU_DTYPE)
    res = _dot(hn, w_ref[...])
    feat_t = _dot_nt(wt_ref[...], hn)
    ones_rows = _ones_row_block(rows)

    cos_at, sin_at = cos_at_ref[...], sin_at_ref[...]
    scale = NSA_HEAD_DIM ** -0.5
    for h in range(NSA_HEADS):
        lo = h * NSA_HEAD_DIM
        q_t = feat_t[lo:lo + NSA_HEAD_DIM] * cos_at + feat_t[_FT_QROT + lo:_FT_QROT + lo + NSA_HEAD_DIM] * sin_at
        q_nsa_ref[0, h] = (q_t * scale).astype(q_nsa_ref.dtype)
    for off, out_ref in ((_FT_VSEL, vt_sel_ref), (_FT_VWIN, vt_win_ref)):
        tile = out_ref.shape[4]
        for g in range(NSA_KV_GROUPS):
            lo = off + g * NSA_HEAD_DIM
            v_t = jnp.concatenate([feat_t[lo:lo + NSA_HEAD_DIM], ones_rows], axis=0)
            for j in range(rows // tile):
                out_ref[0, g, j] = v_t[:, j * tile:(j + 1) * tile].astype(out_ref.dtype)
    gate_ref[0] = jax.nn.sigmoid(feat_t[_FT_GATE:_FT_GATE + _GATE_ROWS])

    cos_a, sin_a = cos_a_ref[...], sin_a_ref[...]
    lane = lax.broadcasted_iota(jnp.int32, (rows, LANES), 1)
    lower = lane < HALF

    def rope_blk(j):
        return (res[:, j * LANES:(j + 1) * LANES] * cos_a
                + res[:, _OFF_ROT + j * LANES:_OFF_ROT + (j + 1) * LANES] * sin_a)

    def split_pair(blk):
        return jnp.where(lower, blk, 0.0), jnp.where(lower, pltpu.roll(blk, HALF, 1), 0.0)

    kc_ref[0] = rope_blk(0)
    vc_ref[0] = res[:, _OFF_VC:_OFF_VC + LANES]
    pos = pl.program_id(1) * rows + lax.broadcasted_iota(jnp.int32, (rows, LANES), 0)
    onehot = jnp.where((lane - HALF) == jnp.right_shift(pos, SEL_BLOCK_LOG2), 1.0, 0.0)
    a, b = split_pair(rope_blk(1))
    k_sel_ref[0, 0] = (a + onehot).astype(k_sel_ref.dtype)
    k_sel_ref[0, 1] = (b + onehot).astype(k_sel_ref.dtype)
    a, b = split_pair(rope_blk(2))
    k_win_ref[0, 0] = a.astype(k_win_ref.dtype)
    k_win_ref[0, 1] = b.astype(k_win_ref.dtype)

    cos_b, sin_b = cos_b_ref[...], sin_b_ref[...]
    cos_bt, sin_bt = cos_bt_ref[...], sin_bt_ref[...]
    cqn = _rms(res[:, _OFF_CQ:_OFF_CQ + Q_LORA_RANK], qn_ref[...]).astype(MXU_DTYPE)
    yq_t = _dot_nt(wq2t_ref[...], cqn)
    rot_off = MLA_HEADS * LANES
    for h in range(MLA_HEADS):
        q_t = (yq_t[h * LANES:(h + 1) * LANES] * cos_bt
               + yq_t[rot_off + h * LANES:rot_off + (h + 1) * LANES] * sin_bt)
        q_mla_ref[0, h] = q_t.astype(q_mla_ref.dtype)
    ckvn = _rms(res[:, _OFF_CKV:_OFF_CKV + KV_LORA_RANK], kvn_ref[...]).astype(MXU_DTYPE)
    yk = _dot(ckvn, wk2_ref[...])
    k_pe = res[:, _OFF_KRA:_OFF_KRA + LANES] * cos_b + res[:, _OFF_KRB:_OFF_KRB + LANES] * sin_b
    for h in range(MLA_HEADS):
        k_mla_ref[0, h] = (yk[:, h * LANES:(h + 1) * LANES] + k_pe).astype(k_mla_ref.dtype)
    yv_t = _dot_nt(wv2t_ref[...], ckvn)
    tk = vt_mla_ref.shape[4]
    for h in range(MLA_HEADS):
        v_t = jnp.concatenate([yv_t[h * MLA_V_DIM:(h + 1) * MLA_V_DIM], ones_rows], axis=0)
        for j in range(rows // tk):
            vt_mla_ref[0, h, j] = v_t[:, j * tk:(j + 1) * tk].astype(vt_mla_ref.dtype)


def _proj_call(x, gain, w_pm, w_fm, tabs, qn, wq2t, kvn, wk2, wv2t):
    B, S, D = x.shape
    rows = min(PROJ_ROWS, S)
    ns = S // rows
    tk = min(ATTN_K_TILE, S)
    tw = min(WIN_V_TILE, S)
    const = lambda b, i: (0, 0)
    tab = lambda t: (pl.BlockSpec((rows, t.shape[1]), lambda b, i: (i, 0)) if t.shape[0] == S
                     else pl.BlockSpec((t.shape[0], rows), lambda b, i: (0, i)))
    hm = lambda n: pl.BlockSpec((1, n, rows, LANES), lambda b, i: (b, 0, i, 0))
    fm = lambda n, f: pl.BlockSpec((1, n, f, rows), lambda b, i: (b, 0, 0, i))
    tiles = lambda n, t: pl.BlockSpec((1, n, rows // t, LANES, t), lambda b, i: (b, 0, i, 0, 0))
    flat = pl.BlockSpec((1, rows, LANES), lambda b, i: (b, i, 0))
    sds = jax.ShapeDtypeStruct
    G = NSA_KV_GROUPS
    return pl.pallas_call(
        _proj_kernel,
        grid=(B, ns),
        in_specs=[
            pl.BlockSpec((1, rows, D), lambda b, i: (b, i, 0)),
            pl.BlockSpec((1, D), const),
            pl.BlockSpec(w_pm.shape, const),
            pl.BlockSpec(w_fm.shape, const),
            *[tab(t) for t in tabs],
            pl.BlockSpec((1, Q_LORA_RANK), const),
            pl.BlockSpec(wq2t.shape, const),
            pl.BlockSpec((1, KV_LORA_RANK), const),
            pl.BlockSpec(wk2.shape, const),
            pl.BlockSpec(wv2t.shape, const),
        ],
        out_specs=[fm(NSA_HEADS, NSA_HEAD_DIM), hm(G), hm(G), flat, flat, tiles(G, tk), tiles(G, tw),
                   pl.BlockSpec((1, _GATE_ROWS, rows), lambda b, i: (b, 0, i)),
                   fm(MLA_HEADS, LANES), hm(MLA_HEADS), tiles(MLA_HEADS, tk)],
        out_shape=[
            sds((B, NSA_HEADS, NSA_HEAD_DIM, S), MXU_DTYPE),
            sds((B, G, S, LANES), MXU_DTYPE),
            sds((B, G, S, LANES), MXU_DTYPE),
            sds((B, S, LANES), jnp.float32),
            sds((B, S, LANES), jnp.float32),
            sds((B, G, S // tk, LANES, tk), MXU_DTYPE),
            sds((B, G, S // tw, LANES, tw), MXU_DTYPE),
            sds((B, _GATE_ROWS, S), jnp.float32),
            sds((B, MLA_HEADS, LANES, S), MXU_DTYPE),
            sds((B, MLA_HEADS, S, LANES), MXU_DTYPE),
            sds((B, MLA_HEADS, S // tk, LANES, tk), MXU_DTYPE),
        ],
        compiler_params=pltpu.CompilerParams(
            dimension_semantics=("parallel", "parallel"), vmem_limit_bytes=VMEM_LIMIT),
        name="proj",
    )(x, gain, w_pm, w_fm, *tabs, qn, wq2t, kvn, wk2, wv2t)


def _compress_kernel(xk_ref, xv_ref, posk_ref, posv_ref, w1k_ref, w1v_ref, w2k_ref, w2vt_ref,
                     kcmp_ref, vtcmp_ref):
    n_chunk = xk_ref.shape[1]

    def hidden(x_ref, pos_ref, w1_ref):
        x = x_ref[0]
        h_lo = _dot((x + pos_ref[0:1, :]).astype(MXU_DTYPE), w1_ref[0])
        h_hi = _dot((x + pos_ref[1:2, :]).astype(MXU_DTYPE), w1_ref[1])
        hid = h_lo + pltpu.roll(h_hi, n_chunk - 1, 0)
        return (hid * jax.nn.sigmoid(hid)).astype(MXU_DTYPE)

    k_out = _dot(hidden(xk_ref, posk_ref, w1k_ref), w2k_ref[...])
    v_out_t = _dot_nt(w2vt_ref[...], hidden(xv_ref, posv_ref, w1v_ref))
    for g in range(NSA_KV_GROUPS):
        kcmp_ref[0, g] = k_out[:, g * LANES:(g + 1) * LANES].astype(kcmp_ref.dtype)
        vtcmp_ref[0, g] = v_out_t[g * LANES:(g + 1) * LANES].astype(vtcmp_ref.dtype)


def _compress_call(xk, xv, posk, posv, w1k, w1v, w2k, w2vt):
    B, n_chunk, width = xk.shape
    xs = pl.BlockSpec((1, n_chunk, width), lambda b: (b, 0, 0))
    c2 = lambda a: pl.BlockSpec(a.shape, lambda b: (0,) * a.ndim)
    G = NSA_KV_GROUPS
    return pl.pallas_call(
        _compress_kernel,
        grid=(B,),
        in_specs=[xs, xs, c2(posk), c2(posv), c2(w1k), c2(w1v), c2(w2k), c2(w2vt)],
        out_specs=[pl.BlockSpec((1, G, n_chunk, LANES), lambda b: (b, 0, 0, 0)),
                   pl.BlockSpec((1, G, LANES, n_chunk), lambda b: (b, 0, 0, 0))],
        out_shape=[jax.ShapeDtypeStruct((B, G, n_chunk, LANES), MXU_DTYPE),
                   jax.ShapeDtypeStruct((B, G, LANES, n_chunk), MXU_DTYPE)],
        compiler_params=pltpu.CompilerParams(
            dimension_semantics=("parallel",), vmem_limit_bytes=VMEM_LIMIT),
        name="compress",
    )(xk, xv, posk, posv, w1k, w1v, w2k, w2vt)


def _nsa_kernel(qt_ref, kcmp_ref, vtcmp_ref, ksel_ref, vtsel_ref, kwin_ref, vtwin_ref, gate_ref,
                imp_ref, cbias_ref, wbias_ref, o_ref, *, n_sel, win_keys, sub):
    g = pl.program_id(1)
    tq = qt_ref.shape[3]
    n_sub = tq // sub
    n_chunk = kcmp_ref.shape[2]
    tk = vtsel_ref.shape[4]
    tw = vtwin_ref.shape[4]
    lanes = NSA_REP * sub
    tile0 = pl.program_id(2) * tq
    m_t = imp_ref[...]
    blk8 = lax.broadcasted_iota(jnp.int32, (8, sub), 0)
    lane_pos = jnp.bitwise_and(lax.broadcasted_iota(jnp.int32, (1, lanes), 1), sub - 1)

    def front(u):
        s0 = tile0 + u * sub
        q_t = jnp.concatenate([qt_ref[0, r, :, u * sub:(u + 1) * sub] for r in range(NSA_REP)], axis=1)
        q_t0 = jnp.concatenate([q_t, jnp.zeros_like(q_t)], axis=0)
        t_row = s0 + lane_pos

        c_off = pl.multiple_of(n_chunk - s0 // CMP_STRIDE, 8)
        s = _dot(kcmp_ref[0, 0], q_t0) + cbias_ref[pl.ds(c_off, n_chunk), :]
        e = jnp.exp(s - jnp.max(s, axis=0, keepdims=True))
        any_valid = t_row >= CMP_BLOCK - 1
        inv = jnp.where(any_valid, 1.0 / jnp.maximum(jnp.sum(e, axis=0, keepdims=True), 1e-30), 0.0)
        p_cmp = e * inv
        o_cmp = _dot(vtcmp_ref[0, 0], p_cmp.astype(MXU_DTYPE))[:HALF]

        p_sum = p_cmp[:, 0:sub]
        for r in range(1, NSA_REP):
            p_sum = p_sum + p_cmp[:, r * sub:(r + 1) * sub]
        hi = p_sum.astype(MXU_DTYPE)
        rem = p_sum - hi.astype(jnp.float32)
        mid = rem.astype(MXU_DTYPE)
        lo = (rem - mid.astype(jnp.float32)).astype(MXU_DTYPE)
        imp_t = _dot(m_t, hi) + _dot(m_t, mid) + _dot(m_t, lo)

        blk = lax.broadcasted_iota(jnp.int32, (MAX_SEL_BLOCKS, sub), 0)
        cur = jnp.right_shift(s0 + lax.broadcasted_iota(jnp.int32, (MAX_SEL_BLOCKS, sub), 1), SEL_BLOCK_LOG2)
        valid = blk <= cur
        forced = valid & ((blk == 0) | (blk > cur - N_LOCAL))
        score = jnp.where(forced, FORCE_BONUS, jnp.where(valid, imp_t, -1.0))
        n_grp = MAX_SEL_BLOCKS // 8
        grp = [score[8 * a:8 * (a + 1)] for a in range(n_grp)]
        rank = [jnp.zeros((8, sub), jnp.float32) for _ in range(n_grp)]
        for j in range(MAX_SEL_BLOCKS):
            row = score[j:j + 1, :]
            for a in range(n_grp):
                if a > j // 8:
                    beats = jnp.where(row >= grp[a], 1.0, 0.0)
                elif a < j // 8:
                    beats = jnp.where(row > grp[a], 1.0, 0.0)
                else:
                    beats = jnp.where(blk8 > j % 8, jnp.where(row >= grp[a], 1.0, 0.0),
                                      jnp.where(row > grp[a], 1.0, 0.0))
                rank[a] = rank[a] + beats
        bias_t = jnp.where(jnp.concatenate(rank, axis=0) < n_sel, 0.0, NEG_INF).astype(MXU_DTYPE)
        q_aug = jnp.concatenate([q_t, jnp.concatenate([bias_t] * NSA_REP, axis=1)], axis=0)

        w_start = pl.multiple_of(jnp.maximum(s0 - WINDOW, 0), sub)
        w_off = pl.multiple_of(WINDOW - (s0 - w_start), sub)
        s = _dot(kwin_ref[0, 0, pl.ds(w_start, win_keys), :], q_t0) + wbias_ref[pl.ds(w_off, win_keys), :]
        e = jnp.exp(s - jnp.max(s, axis=0, keepdims=True))
        w_tile = w_start // tw
        v_t = jnp.concatenate([vtwin_ref[0, 0, w_tile + j] for j in range(win_keys // tw)], axis=1)
        o_win = _flash_finish_t((None, _dot(v_t, e.astype(MXU_DTYPE))))
        return q_aug, t_row, o_cmp, o_win

    fronts = [front(u) for u in range(n_sub)]

    split = ATTN_K_SPLIT
    chains = [(u, j) for u in range(n_sub) for j in range(split)]

    def sel_step(it, state, causal):
        s_ts = []
        for u, j in chains:
            start = pl.multiple_of((it * split + j) * tk, tk)
            s_ts.append(_dot(ksel_ref[0, 0, pl.ds(start, tk), :], fronts[u][0]))
        if causal:
            for c, (u, j) in enumerate(chains):
                key = (it * split + j) * tk + lax.broadcasted_iota(jnp.int32, (tk, lanes), 0)
                s_ts[c] = jnp.where(key <= fronts[u][1], s_ts[c], NEG_INF)
        return tuple(_flash_update_t(state[c], s_ts[c], vtsel_ref[0, 0, it * split + j], 1.0)
                     for c, (u, j) in enumerate(chains))

    n_full = tile0 // (split * tk)
    carry = lax.fori_loop(0, n_full, lambda it, c: sel_step(it, c, False),
                          tuple(_flash_init_t(lanes) for _ in chains))
    carry = sel_step(n_full, carry, True)

    for u in range(n_sub):
        _, _, o_cmp, o_win = fronts[u]
        o_sel = _flash_finish_t(_flash_merge_t(carry[u * split:(u + 1) * split], 1.0))
        gates = gate_ref[0, :, u * sub:(u + 1) * sub]

        def gate(r, br, gates=gates):
            r0 = r * NSA_BRANCHES + br
            r1 = (NSA_REP + r) * NSA_BRANCHES + br
            return jnp.where(g == 0, gates[r0:r0 + 1], gates[r1:r1 + 1])

        heads = []
        for r in range(NSA_REP):
            sl = slice(r * sub, (r + 1) * sub)
            heads.append(gate(r, 0) * o_cmp[:, sl] + gate(r, 1) * o_sel[:, sl] + gate(r, 2) * o_win[:, sl])
        for pair in range(NSA_REP // 2):
            o_ref[0, u * sub:(u + 1) * sub, pair * LANES:(pair + 1) * LANES] = jnp.concatenate(
                [heads[2 * pair], heads[2 * pair + 1]], axis=0).T


def _mask_tables(n_chunk, sub, win_keys):
    pos = (np.arange(NSA_REP * sub) % sub)[None, :]
    r = np.arange(2 * n_chunk)[:, None]
    cmp_ok = CMP_STRIDE * (r - n_chunk) + CMP_BLOCK - 1 <= pos
    r = np.arange(WINDOW + win_keys)[:, None]
    win_ok = (r <= pos + WINDOW) & (r > pos)
    to_bias = lambda ok: jnp.asarray(np.where(ok, 0.0, NEG_INF), jnp.float32)
    return to_bias(cmp_ok), to_bias(win_ok)


def _nsa_call(qt_nsa, k_cmp, vt_cmp, k_sel, vt_sel, k_win, vt_win, gates_t, imp_mat):
    B, _, _, S = qt_nsa.shape
    n_chunk = k_cmp.shape[2]
    tq = min(NSA_Q_TILE, S)
    sub = min(NSA_SUB_TILE, tq)
    tk, tw = vt_sel.shape[4], vt_win.shape[4]
    assert (ATTN_K_SPLIT * tk) % tq == 0 and tq % sub == 0 and sub == LANES and sub % tw == 0
    n_sel = min(N_SELECT, S // SEL_BLOCK)
    win_keys = min(WINDOW + sub, S)
    cmp_bias, win_bias = _mask_tables(n_chunk, sub, win_keys)
    const = lambda a: pl.BlockSpec(a.shape, lambda b, g, i: (0, 0))
    kv = lambda n: pl.BlockSpec((1, 1, n, LANES), lambda b, g, i: (b, g, 0, 0))
    vt = lambda a: pl.BlockSpec((1, 1) + a.shape[2:], lambda b, g, i: (b, g) + (0,) * (a.ndim - 2))
    return pl.pallas_call(
        functools.partial(_nsa_kernel, n_sel=n_sel, win_keys=win_keys, sub=sub),
        grid=(B, NSA_KV_GROUPS, S // tq),
        in_specs=[
            pl.BlockSpec((1, NSA_REP, NSA_HEAD_DIM, tq), lambda b, g, i: (b, g, 0, i)),
            kv(n_chunk), vt(vt_cmp), kv(S), vt(vt_sel), kv(S), vt(vt_win),
            pl.BlockSpec((1, _GATE_ROWS, tq), lambda b, g, i: (b, 0, i)),
            const(imp_mat), const(cmp_bias), const(win_bias),
        ],
        out_specs=pl.BlockSpec((1, tq, NSA_REP * NSA_HEAD_DIM), lambda b, g, i: (b, i, g)),
        out_shape=jax.ShapeDtypeStruct((B, S, NSA_WIDTH), jnp.float32),
        compiler_params=pltpu.CompilerParams(
            dimension_semantics=("parallel", "parallel", "arbitrary"), vmem_limit_bytes=VMEM_LIMIT),
        name="nsa",
    )(qt_nsa, k_cmp, vt_cmp, k_sel, vt_sel, k_win, vt_win, gates_t, imp_mat, cmp_bias, win_bias)


def _mla_kernel(qt_ref, k_ref, vt_ref, o_ref):
    n_heads = qt_ref.shape[1]
    tq = qt_ref.shape[3]
    tk = vt_ref.shape[4]
    q0 = pl.program_id(2) * tq
    scale = (MLA_NOPE_DIM + MLA_ROPE_DIM) ** -0.5
    t_row = q0 + lax.broadcasted_iota(jnp.int32, (1, tq), 1)
    qts = [qt_ref[0, h] for h in range(n_heads)]
    split = tq // tk
    chains = [(h, j) for h in range(n_heads) for j in range(split)]

    def step(it, state, causal):
        s_ts = []
        for h, j in chains:
            start = pl.multiple_of((it * split + j) * tk, tk)
            s_ts.append(_dot(k_ref[0, h, pl.ds(start, tk), :], qts[h]))
        if causal:
            for c, (h, j) in enumerate(chains):
                key = (it * split + j) * tk + lax.broadcasted_iota(jnp.int32, (tk, tq), 0)
                s_ts[c] = jnp.where(key <= t_row, s_ts[c], NEG_INF)
        return tuple(_flash_update_t(state[c], s_ts[c], vt_ref[0, h, it * split + j], scale)
                     for c, (h, j) in enumerate(chains))

    n_full = pl.program_id(2)
    carry = lax.fori_loop(0, n_full, lambda it, c: step(it, c, False),
                          tuple(_flash_init_t(tq) for _ in chains))
    carry = step(n_full, carry, True)
    outs = [_flash_finish_t(_flash_merge_t(carry[h * split:(h + 1) * split], scale)) for h in range(n_heads)]
    for p in range(n_heads // 2):
        o_ref[0, :, p * LANES:(p + 1) * LANES] = jnp.concatenate([outs[2 * p], outs[2 * p + 1]], axis=0).T


def _mla_call(qt_mla, k_mla, vt_mla):
    B, _, _, S = qt_mla.shape
    tq = min(MLA_TILE, S)
    tk = vt_mla.shape[4]
    nh = MLA_HEADS_PER_STEP
    assert tq % tk == 0 and MLA_HEADS % nh == 0 and nh % 2 == 0
    return pl.pallas_call(
        _mla_kernel,
        grid=(B, MLA_HEADS // nh, S // tq),
        in_specs=[
            pl.BlockSpec((1, nh, LANES, tq), lambda b, p, i: (b, p, 0, i)),
            pl.BlockSpec((1, nh, S, LANES), lambda b, p, i: (b, p, 0, 0)),
            pl.BlockSpec((1, nh, S // tk, LANES, tk), lambda b, p, i: (b, p, 0, 0, 0)),
        ],
        out_specs=pl.BlockSpec((1, tq, nh * MLA_V_DIM), lambda b, p, i: (b, i, p)),
        out_shape=jax.ShapeDtypeStruct((B, S, MLA_WIDTH), jnp.float32),
        compiler_params=pltpu.CompilerParams(
            dimension_semantics=("parallel", "parallel", "arbitrary"), vmem_limit_bytes=VMEM_LIMIT),
        name="mla",
    )(qt_mla, k_mla, vt_mla)


def _mix_mlp_kernel(x_ref, oa_ref, ob_ref, na_ref, nb_ref, woa_ref, wob_ref, mn_ref, w1_ref, w2_ref,
                    fn_ref, o_ref, x1_ref, hn_ref, acc_ref, *, final):
    f = pl.program_id(1)

    @pl.when(f == 0)
    def _():
        na = _rms(oa_ref[...], na_ref[...]).astype(MXU_DTYPE)
        nb = _rms(ob_ref[...], nb_ref[...]).astype(MXU_DTYPE)
        x1 = x_ref[...] + (_dot(na, woa_ref[...]) + _dot(nb, wob_ref[...]))
        x1_ref[...] = x1
        hn_ref[...] = _rms(x1, mn_ref[...]).astype(MXU_DTYPE)
        acc_ref[...] = jnp.zeros_like(acc_ref)

    a = jnp.square(jnp.maximum(_dot(hn_ref[...], w1_ref[...]), 0.0))
    acc_ref[...] += _dot(a.astype(MXU_DTYPE), w2_ref[...])

    @pl.when(f == pl.num_programs(1) - 1)
    def _():
        y = x1_ref[...] + acc_ref[...]
        o_ref[...] = _rms(y, fn_ref[...]) if final else y


def _mix_mlp_call(x, o_a, o_b, na, nb, wo_a, wo_b, mn, w1, w2, fn, final):
    T, D = x.shape
    rows = min(MLP_ROWS, T)
    tf = MLP_FF_TILE
    row = lambda w: pl.BlockSpec((rows, w), lambda i, f: (i, 0))
    const = lambda a: pl.BlockSpec(a.shape, lambda i, f: (0, 0))
    return pl.pallas_call(
        functools.partial(_mix_mlp_kernel, final=final),
        grid=(T // rows, D_FF // tf),
        in_specs=[row(D), row(NSA_WIDTH), row(MLA_WIDTH), const(na), const(nb), const(wo_a), const(wo_b),
                  const(mn),
                  pl.BlockSpec((D, tf), lambda i, f: (0, f)),
                  pl.BlockSpec((tf, D), lambda i, f: (f, 0)),
                  const(fn)],
        out_specs=row(D),
        out_shape=jax.ShapeDtypeStruct((T, D), jnp.float32),
        scratch_shapes=[pltpu.VMEM((rows, D), jnp.float32), pltpu.VMEM((rows, D), MXU_DTYPE),
                        pltpu.VMEM((rows, D), jnp.float32)],
        compiler_params=pltpu.CompilerParams(
            dimension_semantics=("parallel", "arbitrary"), vmem_limit_bytes=VMEM_LIMIT),
        name="mix_mlp",
    )(x, o_a, o_b, na, nb, wo_a, wo_b, mn, w1, w2, fn)


def _rot_cols(w, dim):
    lead = w.shape[:-1]
    w4 = w.reshape(lead + (w.shape[-1] // dim, 2, dim // 2))
    return jnp.concatenate([-w4[..., 1, :], w4[..., 0, :]], axis=-1).reshape(w.shape)


def _pad_cols(w, left, total):
    pad = [(0, 0)] * (w.ndim - 1) + [(left, total - left - w.shape[-1])]
    return jnp.pad(w, pad)


def _pack_in_proj(w_in):
    sizes = (NSA_WIDTH,) + (NSA_KV_GROUPS * NSA_HEAD_DIM,) * 6 + (
        NSA_BRANCHES * NSA_HEADS, Q_LORA_RANK, KV_LORA_RANK, MLA_ROPE_DIM)
    offs = np.cumsum((0,) + sizes)
    q_a, k_c, v_c, k_s, v_s, k_w, v_w, g_a, c_q, c_kv, k_r = (
        w_in[..., offs[i]:offs[i + 1]] for i in range(len(sizes)))
    roped = jnp.concatenate([k_c, k_s, k_w], axis=-1)
    w_pm = jnp.concatenate(
        [roped, _rot_cols(roped, NSA_HEAD_DIM), v_c, c_q, c_kv,
         _pad_cols(k_r, HALF, LANES), _pad_cols(_rot_cols(k_r, MLA_ROPE_DIM), HALF, LANES)], axis=-1)
    w_fm = jnp.concatenate([q_a, _rot_cols(q_a, NSA_HEAD_DIM), v_s, v_w, _pad_cols(g_a, 0, _GATE_ROWS)], axis=-1)
    assert w_pm.shape[-1] == _PROJ_COLS and w_fm.shape[-1] == _FT_ROWS
    return w_pm.astype(MXU_DTYPE), jnp.swapaxes(w_fm, 1, 2).astype(MXU_DTYPE)


def _pack_q_up(w_q_up):
    L, R, _ = w_q_up.shape
    w = w_q_up.reshape(L, R, MLA_HEADS, MLA_NOPE_DIM + MLA_ROPE_DIM)
    a = _pad_cols(w, 0, LANES)
    b = _pad_cols(_rot_cols(w[..., MLA_NOPE_DIM:], MLA_ROPE_DIM), MLA_NOPE_DIM, LANES)
    packed = jnp.concatenate([a.reshape(L, R, -1), b.reshape(L, R, -1)], axis=-1)
    return jnp.swapaxes(packed, 1, 2).astype(MXU_DTYPE)


def _pack_kv_up(w_kv_up):
    L, R, _ = w_kv_up.shape
    w = w_kv_up.reshape(L, R, MLA_HEADS, MLA_NOPE_DIM + MLA_V_DIM)
    k = _pad_cols(w[..., :MLA_NOPE_DIM], 0, LANES).reshape(L, R, -1)
    v_t = jnp.swapaxes(w[..., MLA_NOPE_DIM:].reshape(L, R, -1), 1, 2)
    return k.astype(MXU_DTYPE), v_t.astype(MXU_DTYPE)


def _pack_compress(pos, w1, w2):
    L = pos.shape[0]
    G, dk, H = NSA_KV_GROUPS, NSA_HEAD_DIM, CMP_HIDDEN
    pos2 = jnp.concatenate([pos] * G, axis=-1).reshape(L, 2, CMP_STRIDE * G * dk)
    on_diag = lambda w, g, axis: jnp.stack([w if g2 == g else jnp.zeros_like(w) for g2 in range(G)], axis=axis)
    w1r = w1.astype(MXU_DTYPE).reshape(L, 2, CMP_STRIDE, dk, H)
    w1b = jnp.stack([on_diag(w1r, g, 4) for g in range(G)], axis=3)
    w1b = w1b.reshape(L, 2, CMP_STRIDE * G * dk, G * H)
    w2p = _pad_cols(w2.astype(MXU_DTYPE), 0, LANES)
    w2b = jnp.stack([on_diag(w2p, g, 2) for g in range(G)], axis=1)
    return pos2, w1b, w2b.reshape(L, G * H, G * LANES)


def _importance_matrix(n_chunk):
    a, b = SEL_BLOCK // CMP_STRIDE, CMP_BLOCK // CMP_STRIDE
    overlap = np.convolve(np.ones(a), np.ones(b))
    n_cmp = n_chunk - (b - 1)
    m = np.zeros((MAX_SEL_BLOCKS, n_chunk), np.float32)
    for blk in range(n_chunk * CMP_STRIDE // SEL_BLOCK):
        for j, wgt in enumerate(overlap):
            i = a * blk + j - (b - 1)
            if 0 <= i < n_cmp:
                m[blk, i] = wgt
    return jnp.asarray(m, MXU_DTYPE)


def _rope_tables(seq, dim):
    inv_freq = 1.0 / (ROPE_THETA ** (jnp.arange(0, dim, 2, dtype=jnp.float32) / dim))
    ang = jnp.arange(seq, dtype=jnp.float32)[:, None] * inv_freq[None, :]
    ang = jnp.concatenate([ang, ang], axis=-1)
    return jnp.cos(ang), jnp.sin(ang)


def kernel(x, attn_norm, w_in, cmp_pos_k, cmp_w1_k, cmp_w2_k, cmp_pos_v, cmp_w1_v, cmp_w2_v, mla_q_norm,
           w_q_up, mla_kv_norm, w_kv_up, nsa_out_norm, mla_out_norm, w_out, mlp_norm, w_ff1, w_ff2, final_norm):
    B, S, D = x.shape
    depth = w_in.shape[0]
    assert S % SEL_BLOCK == 0 and S // SEL_BLOCK <= MAX_SEL_BLOCKS and S % CMP_STRIDE == 0
    n_chunk = S // CMP_STRIDE

    cos64, sin64 = _rope_tables(S, NSA_HEAD_DIM)
    cos_a, sin_a = jnp.concatenate([cos64, cos64], -1), jnp.concatenate([sin64, sin64], -1)
    cos32, sin32 = _rope_tables(S, MLA_ROPE_DIM)
    pad = jnp.zeros((S, LANES - MLA_NOPE_DIM - MLA_ROPE_DIM), jnp.float32)
    cos_b = jnp.concatenate([jnp.ones((S, MLA_NOPE_DIM), jnp.float32), cos32, pad], -1)
    sin_b = jnp.concatenate([jnp.zeros((S, MLA_NOPE_DIM), jnp.float32), sin32, pad], -1)
    tabs = (cos_a, sin_a, cos64.T, sin64.T, cos_b, sin_b, cos_b.T, sin_b.T)

    w_pm, w_fm = _pack_in_proj(w_in)
    wq2t = _pack_q_up(w_q_up)
    wk2, wv2t = _pack_kv_up(w_kv_up)
    posk, w1k, w2k = _pack_compress(cmp_pos_k, cmp_w1_k, cmp_w2_k)
    posv, w1v, w2v = _pack_compress(cmp_pos_v, cmp_w1_v, cmp_w2_v)
    w2vt = jnp.swapaxes(w2v, 1, 2)
    imp_mat = _importance_matrix(n_chunk)
    wo = w_out.astype(MXU_DTYPE)
    w1 = w_ff1.astype(MXU_DTYPE)
    w2 = w_ff2.astype(MXU_DTYPE)
    row = lambda v: v.reshape(1, -1)

    for l in range(depth):
        (qt_nsa, k_sel, k_win, k_c, v_c, vt_sel, vt_win, gates_t, qt_mla, k_mla, vt_mla) = _proj_call(
            x, row(attn_norm[l]), w_pm[l], w_fm[l], tabs,
            row(mla_q_norm[l]), wq2t[l], row(mla_kv_norm[l]), wk2[l], wv2t[l])
        k_cmp, vt_cmp = _compress_call(
            k_c.reshape(B, n_chunk, CMP_STRIDE * LANES), v_c.reshape(B, n_chunk, CMP_STRIDE * LANES),
            posk[l], posv[l], w1k[l], w1v[l], w2k[l], w2vt[l])
        o_a = _nsa_call(qt_nsa, k_cmp, vt_cmp, k_sel, vt_sel, k_win, vt_win, gates_t, imp_mat)
        o_b = _mla_call(qt_mla, k_mla, vt_mla)
        x = _mix_mlp_call(
            x.reshape(B * S, D), o_a.reshape(B * S, NSA_WIDTH), o_b.reshape(B * S, MLA_WIDTH),
            row(nsa_out_norm[l]), row(mla_out_norm[l]), wo[l, :NSA_WIDTH], wo[l, NSA_WIDTH:],
            row(mlp_norm[l]), w1[l], w2[l], row(final_norm), final=(l == depth - 1)).reshape(B, S, D)
    return x
```

```python
import functools

import numpy as np
import jax
import jax.numpy as jnp
from jax import lax
from jax.experimental import pallas as pl
from jax.experimental.pallas import tpu as pltpu

D_MODEL = 1024
NSA_HEADS = 8
NSA_KV_GROUPS = 2
NSA_REP = NSA_HEADS // NSA_KV_GROUPS
NSA_HEAD_DIM = 64
NSA_BRANCHES = 3
CMP_BLOCK = 32
CMP_STRIDE = 16
CMP_HIDDEN = 256
SEL_BLOCK = 64
SEL_BLOCK_LOG2 = 6
N_SELECT = 16
N_LOCAL = 2
WINDOW = 512
MLA_HEADS = 8
MLA_NOPE_DIM = 64
MLA_ROPE_DIM = 32
MLA_V_DIM = 64
Q_LORA_RANK = 256
KV_LORA_RANK = 128
D_FF = 4 * D_MODEL
ROPE_THETA = 10000.0
NORM_EPS = 1e-6
NEG_INF = -1e30
FORCE_BONUS = 1e4
LOG2_E = 1.4426950408889634
NSA_WIDTH = NSA_HEADS * NSA_HEAD_DIM
MLA_WIDTH = MLA_HEADS * MLA_V_DIM

LANES = 128
HALF = LANES // 2
V_ROWS = HALF + 16
MAX_SEL_BLOCKS = HALF
MXU_DTYPE = jnp.bfloat16
VMEM_LIMIT = 52 * 1024 * 1024
PROJ_ROWS = 512
ATTN_K_TILE = 256
ATTN_K_SPLIT = 2
NSA_Q_TILE = ATTN_K_TILE * ATTN_K_SPLIT
NSA_SUB_TILE = 128
WIN_V_TILE = 128
MLA_TILE = ATTN_K_TILE * ATTN_K_SPLIT
MLA_HEADS_PER_STEP = 8
MLP_ROWS = 512
MLP_FF_TILE = 2048

_N_ROPE_BLK = 3
_OFF_ROT = _N_ROPE_BLK * LANES
_OFF_VC = 2 * _OFF_ROT
_OFF_CQ = _OFF_VC + LANES
_OFF_CKV = _OFF_CQ + Q_LORA_RANK
_OFF_KRA = _OFF_CKV + KV_LORA_RANK
_OFF_KRB = _OFF_KRA + LANES
_PROJ_COLS = _OFF_KRB + LANES
_FT_QROT = NSA_WIDTH
_FT_VSEL = 2 * NSA_WIDTH
_FT_VWIN = _FT_VSEL + NSA_KV_GROUPS * NSA_HEAD_DIM
_FT_GATE = _FT_VWIN + NSA_KV_GROUPS * NSA_HEAD_DIM
_GATE_ROWS = 32
_FT_ROWS = _FT_GATE + _GATE_ROWS


def _dot(a, b):
    return jnp.dot(a, b, preferred_element_type=jnp.float32)


def _dot_nt(a, b):
    return lax.dot_general(a, b, (((1,), (1,)), ((), ())), preferred_element_type=jnp.float32)


def _rms(x, gain):
    return x * lax.rsqrt(jnp.mean(x * x, axis=-1, keepdims=True) + NORM_EPS) * gain


def _ones_row_block(n):
    return jnp.where(lax.broadcasted_iota(jnp.int32, (V_ROWS - HALF, n), 0) == 0, 1.0, 0.0)


def _flash_update_t(carry, s_t, v_t, scale):
    m, acc = carry
    c = scale * LOG2_E
    m_new = jnp.maximum(m, jnp.max(s_t, axis=0, keepdims=True))
    alpha = jnp.exp2((m - m_new) * c)
    p_t = jnp.exp2((s_t - m_new).astype(MXU_DTYPE) * c)
    return m_new, alpha * acc + _dot(v_t, p_t)


def _flash_init_t(n):
    return jnp.full((1, n), NEG_INF, jnp.float32), jnp.zeros((V_ROWS, n), jnp.float32)


def _flash_merge_t(states, scale):
    c = scale * LOG2_E
    m = states[0][0]
    for st in states[1:]:
        m = jnp.maximum(m, st[0])
    acc = jnp.exp2((states[0][0] - m) * c) * states[0][1]
    for st in states[1:]:
        acc = acc + jnp.exp2((st[0] - m) * c) * st[1]
    return m, acc


def _flash_finish_t(carry):
    acc = carry[1]
    return acc[:HALF] / jnp.maximum(acc[HALF:HALF + 1], 1e-30)


def _proj_kernel(x_ref, gain_ref, w_ref, wt_ref, cos_a_ref, sin_a_ref, cos_at_ref, sin_at_ref,
                 cos_b_ref, sin_b_ref, cos_bt_ref, sin_bt_ref, qn_ref, wq2t_ref, kvn_ref, wk2_ref, wv2t_ref,
                 q_nsa_ref, k_sel_ref, k_win_ref, kc_ref, vc_ref, vt_sel_ref, vt_win_ref, gate_ref,
                 q_mla_ref, k_mla_ref, vt_mla_ref):
    rows = x_ref.shape[1]
    hn = _rms(x_ref[0], gain_ref[...]).astype(MXU_DTYPE)
    res = _dot(hn, w_ref[...])
    feat_t = _dot_nt(wt_ref[...], hn)
    ones_rows = _ones_row_block(rows)

    cos_at, sin_at = cos_at_ref[...], sin_at_ref[...]
    scale = NSA_HEAD_DIM ** -0.5
    for h in range(NSA_HEADS):
        lo = h * NSA_HEAD_DIM
        q_t = feat_t[lo:lo + NSA_HEAD_DIM] * cos_at + feat_t[_FT_QROT + lo:_FT_QROT + lo + NSA_HEAD_DIM] * sin_at
        q_nsa_ref[0, h] = (q_t * scale).astype(q_nsa_ref.dtype)
    for off, out_ref in ((_FT_VSEL, vt_sel_ref), (_FT_VWIN, vt_win_ref)):
        tile = out_ref.shape[4]
        for g in range(NSA_KV_GROUPS):
            lo = off + g * NSA_HEAD_DIM
            v_t = jnp.concatenate([feat_t[lo:lo + NSA_HEAD_DIM], ones_rows], axis=0)
            for j in range(rows // tile):
                out_ref[0, g, j] = v_t[:, j * tile:(j + 1) * tile].astype(out_ref.dtype)
    gate_ref[0] = jax.nn.sigmoid(feat_t[_FT_GATE:_FT_GATE + _GATE_ROWS])

    cos_a, sin_a = cos_a_ref[...], sin_a_ref[...]
    lane = lax.broadcasted_iota(jnp.int32, (rows, LANES), 1)
    lower = lane < HALF

    def rope_blk(j):
        return (res[:, j * LANES:(j + 1) * LANES] * cos_a
                + res[:, _OFF_ROT + j * LANES:_OFF_ROT + (j + 1) * LANES] * sin_a)

    def split_pair(blk):
        return jnp.where(lower, blk, 0.0), jnp.where(lower, pltpu.roll(blk, HALF, 1), 0.0)

    kc_ref[0] = rope_blk(0)
    vc_ref[0] = res[:, _OFF_VC:_OFF_VC + LANES]
    pos = pl.program_id(1) * rows + lax.broadcasted_iota(jnp.int32, (rows, LANES), 0)
    onehot = jnp.where((lane - HALF) == jnp.right_shift(pos, SEL_BLOCK_LOG2), 1.0, 0.0)
    a, b = split_pair(rope_blk(1))
    k_sel_ref[0, 0] = (a + onehot).astype(k_sel_ref.dtype)
    k_sel_ref[0, 1] = (b + onehot).astype(k_sel_ref.dtype)
    a, b = split_pair(rope_blk(2))
    k_win_ref[0, 0] = a.astype(k_win_ref.dtype)
    k_win_ref[0, 1] = b.astype(k_win_ref.dtype)

    cos_b, sin_b = cos_b_ref[...], sin_b_ref[...]
    cos_bt, sin_bt = cos_bt_ref[...], sin_bt_ref[...]
    cqn = _rms(res[:, _OFF_CQ:_OFF_CQ + Q_LORA_RANK], qn_ref[...]).astype(MXU_DTYPE)
    yq_t = _dot_nt(wq2t_ref[...], cqn)
    rot_off = MLA_HEADS * LANES
    for h in range(MLA_HEADS):
        q_t = (yq_t[h * LANES:(h + 1) * LANES] * cos_bt
               + yq_t[rot_off + h * LANES:rot_off + (h + 1) * LANES] * sin_bt)
        q_mla_ref[0, h] = q_t.astype(q_mla_ref.dtype)
    ckvn = _rms(res[:, _OFF_CKV:_OFF_CKV + KV_LORA_RANK], kvn_ref[...]).astype(MXU_DTYPE)
    yk = _dot(ckvn, wk2_ref[...])
    k_pe = res[:, _OFF_KRA:_OFF_KRA + LANES] * cos_b + res[:, _OFF_KRB:_OFF_KRB + LANES] * sin_b
    for h in range(MLA_HEADS):
        k_mla_ref[0, h] = (yk[:, h * LANES:(h + 1) * LANES] + k_pe).astype(k_mla_ref.dtype)
    yv_t = _dot_nt(wv2t_ref[...], ckvn)
    tk = vt_mla_ref.shape[4]
    for h in range(MLA_HEADS):
        v_t = jnp.concatenate([yv_t[h * MLA_V_DIM:(h + 1) * MLA_V_DIM], ones_rows], axis=0)
        for j in range(rows // tk):
            vt_mla_ref[0, h, j] = v_t[:, j * tk:(j + 1) * tk].astype(vt_mla_ref.dtype)


def _proj_call(x, gain, w_pm, w_fm, tabs, qn, wq2t, kvn, wk2, wv2t):
    B, S, D = x.shape
    rows = min(PROJ_ROWS, S)
    ns = S // rows
    tk = min(ATTN_K_TILE, S)
    tw = min(WIN_V_TILE, S)
    const = lambda b, i: (0, 0)
    tab = lambda t: (pl.BlockSpec((rows, t.shape[1]), lambda b, i: (i, 0)) if t.shape[0] == S
                     else pl.BlockSpec((t.shape[0], rows), lambda b, i: (0, i)))
    hm = lambda n: pl.BlockSpec((1, n, rows, LANES), lambda b, i: (b, 0, i, 0))
    fm = lambda n, f: pl.BlockSpec((1, n, f, rows), lambda b, i: (b, 0, 0, i))
    tiles = lambda n, t: pl.BlockSpec((1, n, rows // t, V_ROWS, t), lambda b, i: (b, 0, i, 0, 0))
    flat = pl.BlockSpec((1, rows, LANES), lambda b, i: (b, i, 0))
    sds = jax.ShapeDtypeStruct
    G = NSA_KV_GROUPS
    return pl.pallas_call(
        _proj_kernel,
        grid=(B, ns),
        in_specs=[
            pl.BlockSpec((1, rows, D), lambda b, i: (b, i, 0)),
            pl.BlockSpec((1, D), const),
            pl.BlockSpec(w_pm.shape, const),
            pl.BlockSpec(w_fm.shape, const),
            *[tab(t) for t in tabs],
            pl.BlockSpec((1, Q_LORA_RANK), const),
            pl.BlockSpec(wq2t.shape, const),
            pl.BlockSpec((1, KV_LORA_RANK), const),
            pl.BlockSpec(wk2.shape, const),
            pl.BlockSpec(wv2t.shape, const),
        ],
        out_specs=[fm(NSA_HEADS, NSA_HEAD_DIM), hm(G), hm(G), flat, flat, tiles(G, tk), tiles(G, tw),
                   pl.BlockSpec((1, _GATE_ROWS, rows), lambda b, i: (b, 0, i)),
                   fm(MLA_HEADS, LANES), hm(MLA_HEADS), tiles(MLA_HEADS, tk)],
        out_shape=[
            sds((B, NSA_HEADS, NSA_HEAD_DIM, S), MXU_DTYPE),
            sds((B, G, S, LANES), MXU_DTYPE),
            sds((B, G, S, LANES), MXU_DTYPE),
            sds((B, S, LANES), jnp.float32),
            sds((B, S, LANES), jnp.float32),
            sds((B, G, S // tk, V_ROWS, tk), MXU_DTYPE),
            sds((B, G, S // tw, V_ROWS, tw), MXU_DTYPE),
            sds((B, _GATE_ROWS, S), jnp.float32),
            sds((B, MLA_HEADS, LANES, S), MXU_DTYPE),
            sds((B, MLA_HEADS, S, LANES), MXU_DTYPE),
            sds((B, MLA_HEADS, S // tk, V_ROWS, tk), MXU_DTYPE),
        ],
        compiler_params=pltpu.CompilerParams(
            dimension_semantics=("parallel", "parallel"), vmem_limit_bytes=VMEM_LIMIT),
        name="proj",
    )(x, gain, w_pm, w_fm, *tabs, qn, wq2t, kvn, wk2, wv2t)


def _compress_kernel(xk_ref, xv_ref, posk_ref, posv_ref, w1k_ref, w1v_ref, w2k_ref, w2vt_ref,
                     kcmp_ref, vtcmp_ref):
    n_chunk = xk_ref.shape[1]

    def hidden(x_ref, pos_ref, w1_ref):
        x = x_ref[0]
        h_lo = _dot((x + pos_ref[0:1, :]).astype(MXU_DTYPE), w1_ref[0])
        h_hi = _dot((x + pos_ref[1:2, :]).astype(MXU_DTYPE), w1_ref[1])
        hid = h_lo + pltpu.roll(h_hi, n_chunk - 1, 0)
        return (hid * jax.nn.sigmoid(hid)).astype(MXU_DTYPE)

    k_out = _dot(hidden(xk_ref, posk_ref, w1k_ref), w2k_ref[...])
    v_out_t = _dot_nt(w2vt_ref[...], hidden(xv_ref, posv_ref, w1v_ref))
    for g in range(NSA_KV_GROUPS):
        kcmp_ref[0, g] = k_out[:, g * LANES:(g + 1) * LANES].astype(kcmp_ref.dtype)
        vtcmp_ref[0, g] = v_out_t[g * LANES:(g + 1) * LANES].astype(vtcmp_ref.dtype)


def _compress_call(xk, xv, posk, posv, w1k, w1v, w2k, w2vt):
    B, n_chunk, width = xk.shape
    xs = pl.BlockSpec((1, n_chunk, width), lambda b: (b, 0, 0))
    c2 = lambda a: pl.BlockSpec(a.shape, lambda b: (0,) * a.ndim)
    G = NSA_KV_GROUPS
    return pl.pallas_call(
        _compress_kernel,
        grid=(B,),
        in_specs=[xs, xs, c2(posk), c2(posv), c2(w1k), c2(w1v), c2(w2k), c2(w2vt)],
        out_specs=[pl.BlockSpec((1, G, n_chunk, LANES), lambda b: (b, 0, 0, 0)),
                   pl.BlockSpec((1, G, LANES, n_chunk), lambda b: (b, 0, 0, 0))],
        out_shape=[jax.ShapeDtypeStruct((B, G, n_chunk, LANES), MXU_DTYPE),
                   jax.ShapeDtypeStruct((B, G, LANES, n_chunk), MXU_DTYPE)],
        compiler_params=pltpu.CompilerParams(
            dimension_semantics=("parallel",), vmem_limit_bytes=VMEM_LIMIT),
        name="compress",
    )(xk, xv, posk, posv, w1k, w1v, w2k, w2vt)


def _nsa_kernel(qt_ref, kcmp_ref, vtcmp_ref, ksel_ref, vtsel_ref, kwin_ref, vtwin_ref, gate_ref,
                imp_ref, cbias_ref, wbias_ref, o_ref, *, n_sel, win_keys, sub):
    g = pl.program_id(1)
    tq = qt_ref.shape[3]
    n_sub = tq // sub
    n_chunk = kcmp_ref.shape[2]
    tk = vtsel_ref.shape[4]
    tw = vtwin_ref.shape[4]
    lanes = NSA_REP * sub
    tile0 = pl.program_id(2) * tq
    m_t = imp_ref[...]
    blk8 = lax.broadcasted_iota(jnp.int32, (8, sub), 0)
    lane_pos = jnp.bitwise_and(lax.broadcasted_iota(jnp.int32, (1, lanes), 1), sub - 1)

    def front(u):
        s0 = tile0 + u * sub
        q_t = jnp.concatenate([qt_ref[0, r, :, u * sub:(u + 1) * sub] for r in range(NSA_REP)], axis=1)
        q_t0 = jnp.concatenate([q_t, jnp.zeros_like(q_t)], axis=0)
        t_row = s0 + lane_pos

        c_off = pl.multiple_of(n_chunk - s0 // CMP_STRIDE, 8)
        s = _dot(kcmp_ref[0, 0], q_t0) + cbias_ref[pl.ds(c_off, n_chunk), :]
        e = jnp.exp(s - jnp.max(s, axis=0, keepdims=True))
        any_valid = t_row >= CMP_BLOCK - 1
        inv = jnp.where(any_valid, 1.0 / jnp.maximum(jnp.sum(e, axis=0, keepdims=True), 1e-30), 0.0)
        p_cmp = e * inv
        o_cmp = _dot(vtcmp_ref[0, 0], p_cmp.astype(MXU_DTYPE))[:HALF]

        p_sum = p_cmp[:, 0:sub]
        for r in range(1, NSA_REP):
            p_sum = p_sum + p_cmp[:, r * sub:(r + 1) * sub]
        hi = p_sum.astype(MXU_DTYPE)
        rem = p_sum - hi.astype(jnp.float32)
        mid = rem.astype(MXU_DTYPE)
        lo = (rem - mid.astype(jnp.float32)).astype(MXU_DTYPE)
        imp_t = _dot(m_t, hi) + _dot(m_t, mid) + _dot(m_t, lo)

        blk = lax.broadcasted_iota(jnp.int32, (MAX_SEL_BLOCKS, sub), 0)
        cur = jnp.right_shift(s0 + lax.broadcasted_iota(jnp.int32, (MAX_SEL_BLOCKS, sub), 1), SEL_BLOCK_LOG2)
        valid = blk <= cur
        forced = valid & ((blk == 0) | (blk > cur - N_LOCAL))
        score = jnp.where(forced, FORCE_BONUS, jnp.where(valid, imp_t, -1.0))
        n_grp = MAX_SEL_BLOCKS // 8
        grp = [score[8 * a:8 * (a + 1)] for a in range(n_grp)]
        rank = [jnp.zeros((8, sub), jnp.float32) for _ in range(n_grp)]
        for j in range(MAX_SEL_BLOCKS):
            row = score[j:j + 1, :]
            for a in range(n_grp):
                if a > j // 8:
                    beats = jnp.where(row >= grp[a], 1.0, 0.0)
                elif a < j // 8:
                    beats = jnp.where(row > grp[a], 1.0, 0.0)
                else:
                    beats = jnp.where(blk8 > j % 8, jnp.where(row >= grp[a], 1.0, 0.0),
                                      jnp.where(row > grp[a], 1.0, 0.0))
                rank[a] = rank[a] + beats
        bias_t = jnp.where(jnp.concatenate(rank, axis=0) < n_sel, 0.0, NEG_INF).astype(MXU_DTYPE)
        q_aug = jnp.concatenate([q_t, jnp.concatenate([bias_t] * NSA_REP, axis=1)], axis=0)

        w_start = pl.multiple_of(jnp.maximum(s0 - WINDOW, 0), sub)
        w_off = pl.multiple_of(WINDOW - (s0 - w_start), sub)
        s = _dot(kwin_ref[0, 0, pl.ds(w_start, win_keys), :], q_t0) + wbias_ref[pl.ds(w_off, win_keys), :]
        e = jnp.exp((s - jnp.max(s, axis=0, keepdims=True)).astype(MXU_DTYPE))
        w_tile = w_start // tw
        v_t = jnp.concatenate([vtwin_ref[0, 0, w_tile + j] for j in range(win_keys // tw)], axis=1)
        o_win = _flash_finish_t((None, _dot(v_t, e)))
        return q_aug, t_row, o_cmp, o_win

    fronts = [front(u) for u in range(n_sub)]

    split = ATTN_K_SPLIT
    chains = [(u, j) for u in range(n_sub) for j in range(split)]

    def sel_step(it, state, diagonal):
        active = [c for c, (u, j) in enumerate(chains) if not (diagonal and j * tk >= (u + 1) * sub)]
        s_ts = {}
        for c in active:
            u, j = chains[c]
            start = pl.multiple_of((it * split + j) * tk, tk)
            s_ts[c] = _dot(ksel_ref[0, 0, pl.ds(start, tk), :], fronts[u][0])
        if diagonal:
            for c in active:
                u, j = chains[c]
                if (j + 1) * tk > u * sub:
                    key = (it * split + j) * tk + lax.broadcasted_iota(jnp.int32, (tk, lanes), 0)
                    s_ts[c] = jnp.where(key <= fronts[u][1], s_ts[c], NEG_INF)
        out = list(state)
        for c in active:
            out[c] = _flash_update_t(state[c], s_ts[c], vtsel_ref[0, 0, it * split + chains[c][1]], 1.0)
        return tuple(out)

    n_full = pl.program_id(2)
    carry = lax.fori_loop(0, n_full, lambda it, c: sel_step(it, c, False),
                          tuple(_flash_init_t(lanes) for _ in chains))
    carry = sel_step(n_full, carry, True)

    for u in range(n_sub):
        _, _, o_cmp, o_win = fronts[u]
        o_sel = _flash_finish_t(_flash_merge_t(carry[u * split:(u + 1) * split], 1.0))
        gates = gate_ref[0, :, u * sub:(u + 1) * sub]

        def gate(r, br, gates=gates):
            r0 = r * NSA_BRANCHES + br
            r1 = (NSA_REP + r) * NSA_BRANCHES + br
            return jnp.where(g == 0, gates[r0:r0 + 1], gates[r1:r1 + 1])

        heads = []
        for r in range(NSA_REP):
            sl = slice(r * sub, (r + 1) * sub)
            heads.append(gate(r, 0) * o_cmp[:, sl] + gate(r, 1) * o_sel[:, sl] + gate(r, 2) * o_win[:, sl])
        for pair in range(NSA_REP // 2):
            o_ref[0, u * sub:(u + 1) * sub, pair * LANES:(pair + 1) * LANES] = jnp.concatenate(
                [heads[2 * pair], heads[2 * pair + 1]], axis=0).T


def _mask_tables(n_chunk, sub, win_keys):
    pos = (np.arange(NSA_REP * sub) % sub)[None, :]
    r = np.arange(2 * n_chunk)[:, None]
    cmp_ok = CMP_STRIDE * (r - n_chunk) + CMP_BLOCK - 1 <= pos
    r = np.arange(WINDOW + win_keys)[:, None]
    win_ok = (r <= pos + WINDOW) & (r > pos)
    to_bias = lambda ok: jnp.asarray(np.where(ok, 0.0, NEG_INF), jnp.float32)
    return to_bias(cmp_ok), to_bias(win_ok)


def _nsa_call(qt_nsa, k_cmp, vt_cmp, k_sel, vt_sel, k_win, vt_win, gates_t, imp_mat):
    B, _, _, S = qt_nsa.shape
    n_chunk = k_cmp.shape[2]
    tq = min(NSA_Q_TILE, S)
    sub = min(NSA_SUB_TILE, tq)
    tk, tw = vt_sel.shape[4], vt_win.shape[4]
    assert ATTN_K_SPLIT * tk == tq and tq % sub == 0 and sub == LANES and sub % tw == 0
    n_sel = min(N_SELECT, S // SEL_BLOCK)
    win_keys = min(WINDOW + sub, S)
    cmp_bias, win_bias = _mask_tables(n_chunk, sub, win_keys)
    const = lambda a: pl.BlockSpec(a.shape, lambda b, g, i: (0, 0))
    kv = lambda n: pl.BlockSpec((1, 1, n, LANES), lambda b, g, i: (b, g, 0, 0))
    vt = lambda a: pl.BlockSpec((1, 1) + a.shape[2:], lambda b, g, i: (b, g) + (0,) * (a.ndim - 2))
    return pl.pallas_call(
        functools.partial(_nsa_kernel, n_sel=n_sel, win_keys=win_keys, sub=sub),
        grid=(B, NSA_KV_GROUPS, S // tq),
        in_specs=[
            pl.BlockSpec((1, NSA_REP, NSA_HEAD_DIM, tq), lambda b, g, i: (b, g, 0, i)),
            kv(n_chunk), vt(vt_cmp), kv(S), vt(vt_sel), kv(S), vt(vt_win),
            pl.BlockSpec((1, _GATE_ROWS, tq), lambda b, g, i: (b, 0, i)),
            const(imp_mat), const(cmp_bias), const(win_bias),
        ],
        out_specs=pl.BlockSpec((1, tq, NSA_REP * NSA_HEAD_DIM), lambda b, g, i: (b, i, g)),
        out_shape=jax.ShapeDtypeStruct((B, S, NSA_WIDTH), jnp.float32),
        compiler_params=pltpu.CompilerParams(
            dimension_semantics=("parallel", "parallel", "arbitrary"), vmem_limit_bytes=VMEM_LIMIT),
        name="nsa",
    )(qt_nsa, k_cmp, vt_cmp, k_sel, vt_sel, k_win, vt_win, gates_t, imp_mat, cmp_bias, win_bias)


def _mla_kernel(qt_ref, k_ref, vt_ref, o_ref):
    n_heads = qt_ref.shape[1]
    tq = qt_ref.shape[3]
    tk = vt_ref.shape[4]
    q0 = pl.program_id(2) * tq
    scale = (MLA_NOPE_DIM + MLA_ROPE_DIM) ** -0.5
    t_row = q0 + lax.broadcasted_iota(jnp.int32, (1, tq), 1)
    qts = [qt_ref[0, h] for h in range(n_heads)]
    split = tq // tk
    chains = [(h, j) for h in range(n_heads) for j in range(split)]

    def step(it, state, causal):
        s_ts = []
        for h, j in chains:
            start = pl.multiple_of((it * split + j) * tk, tk)
            s_ts.append(_dot(k_ref[0, h, pl.ds(start, tk), :], qts[h]))
        if causal:
            for c, (h, j) in enumerate(chains):
                key = (it * split + j) * tk + lax.broadcasted_iota(jnp.int32, (tk, tq), 0)
                s_ts[c] = jnp.where(key <= t_row, s_ts[c], NEG_INF)
        return tuple(_flash_update_t(state[c], s_ts[c], vt_ref[0, h, it * split + j], scale)
                     for c, (h, j) in enumerate(chains))

    n_full = pl.program_id(2)
    carry = lax.fori_loop(0, n_full, lambda it, c: step(it, c, False),
                          tuple(_flash_init_t(tq) for _ in chains))
    carry = step(n_full, carry, True)
    outs = [_flash_finish_t(_flash_merge_t(carry[h * split:(h + 1) * split], scale)) for h in range(n_heads)]
    for p in range(n_heads // 2):
        o_ref[0, :, p * LANES:(p + 1) * LANES] = jnp.concatenate([outs[2 * p], outs[2 * p + 1]], axis=0).T


def _mla_call(qt_mla, k_mla, vt_mla):
    B, _, _, S = qt_mla.shape
    tq = min(MLA_TILE, S)
    tk = vt_mla.shape[4]
    nh = MLA_HEADS_PER_STEP
    assert tq % tk == 0 and MLA_HEADS % nh == 0 and nh % 2 == 0
    return pl.pallas_call(
        _mla_kernel,
        grid=(B, MLA_HEADS // nh, S // tq),
        in_specs=[
            pl.BlockSpec((1, nh, LANES, tq), lambda b, p, i: (b, p, 0, i)),
            pl.BlockSpec((1, nh, S, LANES), lambda b, p, i: (b, p, 0, 0)),
            pl.BlockSpec((1, nh, S // tk, V_ROWS, tk), lambda b, p, i: (b, p, 0, 0, 0)),
        ],
        out_specs=pl.BlockSpec((1, tq, nh * MLA_V_DIM), lambda b, p, i: (b, i, p)),
        out_shape=jax.ShapeDtypeStruct((B, S, MLA_WIDTH), jnp.float32),
        compiler_params=pltpu.CompilerParams(
            dimension_semantics=("parallel", "parallel", "arbitrary"), vmem_limit_bytes=VMEM_LIMIT),
        name="mla",
    )(qt_mla, k_mla, vt_mla)


def _mix_mlp_kernel(x_ref, oa_ref, ob_ref, na_ref, nb_ref, woa_ref, wob_ref, mn_ref, w1_ref, w2_ref,
                    fn_ref, o_ref, x1_ref, hn_ref, acc_ref, *, final):
    f = pl.program_id(1)

    @pl.when(f == 0)
    def _():
        na = _rms(oa_ref[...], na_ref[...]).astype(MXU_DTYPE)
        nb = _rms(ob_ref[...], nb_ref[...]).astype(MXU_DTYPE)
        x1 = x_ref[...] + (_dot(na, woa_ref[...]) + _dot(nb, wob_ref[...]))
        x1_ref[...] = x1
        hn_ref[...] = _rms(x1, mn_ref[...]).astype(MXU_DTYPE)
        acc_ref[...] = jnp.zeros_like(acc_ref)

    a = jnp.square(jnp.maximum(_dot(hn_ref[...], w1_ref[...]), 0.0))
    acc_ref[...] += _dot(a.astype(MXU_DTYPE), w2_ref[...])

    @pl.when(f == pl.num_programs(1) - 1)
    def _():
        y = x1_ref[...] + acc_ref[...]
        o_ref[...] = _rms(y, fn_ref[...]) if final else y


def _mix_mlp_call(x, o_a, o_b, na, nb, wo_a, wo_b, mn, w1, w2, fn, final):
    T, D = x.shape
    rows = min(MLP_ROWS, T)
    tf = MLP_FF_TILE
    row = lambda w: pl.BlockSpec((rows, w), lambda i, f: (i, 0))
    const = lambda a: pl.BlockSpec(a.shape, lambda i, f: (0, 0))
    return pl.pallas_call(
        functools.partial(_mix_mlp_kernel, final=final),
        grid=(T // rows, D_FF // tf),
        in_specs=[row(D), row(NSA_WIDTH), row(MLA_WIDTH), const(na), const(nb), const(wo_a), const(wo_b),
                  const(mn),
                  pl.BlockSpec((D, tf), lambda i, f: (0, f)),
                  pl.BlockSpec((tf, D), lambda i, f: (f, 0)),
                  const(fn)],
        out_specs=row(D),
        out_shape=jax.ShapeDtypeStruct((T, D), jnp.float32),
        scratch_shapes=[pltpu.VMEM((rows, D), jnp.float32), pltpu.VMEM((rows, D), MXU_DTYPE),
                        pltpu.VMEM((rows, D), jnp.float32)],
        compiler_params=pltpu.CompilerParams(
            dimension_semantics=("parallel", "arbitrary"), vmem_limit_bytes=VMEM_LIMIT),
        name="mix_mlp",
    )(x, o_a, o_b, na, nb, wo_a, wo_b, mn, w1, w2, fn)


def _rot_cols(w, dim):
    lead = w.shape[:-1]
    w4 = w.reshape(lead + (w.shape[-1] // dim, 2, dim // 2))
    return jnp.concatenate([-w4[..., 1, :], w4[..., 0, :]], axis=-1).reshape(w.shape)


def _pad_cols(w, left, total):
    pad = [(0, 0)] * (w.ndim - 1) + [(left, total - left - w.shape[-1])]
    return jnp.pad(w, pad)


def _pack_in_proj(w_in):
    sizes = (NSA_WIDTH,) + (NSA_KV_GROUPS * NSA_HEAD_DIM,) * 6 + (
        NSA_BRANCHES * NSA_HEADS, Q_LORA_RANK, KV_LORA_RANK, MLA_ROPE_DIM)
    offs = np.cumsum((0,) + sizes)
    q_a, k_c, v_c, k_s, v_s, k_w, v_w, g_a, c_q, c_kv, k_r = (
        w_in[..., offs[i]:offs[i + 1]] for i in range(len(sizes)))
    roped = jnp.concatenate([k_c, k_s, k_w], axis=-1)
    w_pm = jnp.concatenate(
        [roped, _rot_cols(roped, NSA_HEAD_DIM), v_c, c_q, c_kv,
         _pad_cols(k_r, HALF, LANES), _pad_cols(_rot_cols(k_r, MLA_ROPE_DIM), HALF, LANES)], axis=-1)
    w_fm = jnp.concatenate([q_a, _rot_cols(q_a, NSA_HEAD_DIM), v_s, v_w, _pad_cols(g_a, 0, _GATE_ROWS)], axis=-1)
    assert w_pm.shape[-1] == _PROJ_COLS and w_fm.shape[-1] == _FT_ROWS
    return w_pm.astype(MXU_DTYPE), jnp.swapaxes(w_fm, 1, 2).astype(MXU_DTYPE)


def _pack_q_up(w_q_up):
    L, R, _ = w_q_up.shape
    w = w_q_up.reshape(L, R, MLA_HEADS, MLA_NOPE_DIM + MLA_ROPE_DIM)
    a = _pad_cols(w, 0, LANES)
    b = _pad_cols(_rot_cols(w[..., MLA_NOPE_DIM:], MLA_ROPE_DIM), MLA_NOPE_DIM, LANES)
    packed = jnp.concatenate([a.reshape(L, R, -1), b.reshape(L, R, -1)], axis=-1)
    return jnp.swapaxes(packed, 1, 2).astype(MXU_DTYPE)


def _pack_kv_up(w_kv_up):
    L, R, _ = w_kv_up.shape
    w = w_kv_up.reshape(L, R, MLA_HEADS, MLA_NOPE_DIM + MLA_V_DIM)
    k = _pad_cols(w[..., :MLA_NOPE_DIM], 0, LANES).reshape(L, R, -1)
    v_t = jnp.swapaxes(w[..., MLA_NOPE_DIM:].reshape(L, R, -1), 1, 2)
    return k.astype(MXU_DTYPE), v_t.astype(MXU_DTYPE)


def _pack_compress(pos, w1, w2):
    L = pos.shape[0]
    G, dk, H = NSA_KV_GROUPS, NSA_HEAD_DIM, CMP_HIDDEN
    pos2 = jnp.concatenate([pos] * G, axis=-1).reshape(L, 2, CMP_STRIDE * G * dk)
    on_diag = lambda w, g, axis: jnp.stack([w if g2 == g else jnp.zeros_like(w) for g2 in range(G)], axis=axis)
    w1r = w1.astype(MXU_DTYPE).reshape(L, 2, CMP_STRIDE, dk, H)
    w1b = jnp.stack([on_diag(w1r, g, 4) for g in range(G)], axis=3)
    w1b = w1b.reshape(L, 2, CMP_STRIDE * G * dk, G * H)
    w2p = _pad_cols(w2.astype(MXU_DTYPE), 0, LANES)
    w2b = jnp.stack([on_diag(w2p, g, 2) for g in range(G)], axis=1)
    return pos2, w1b, w2b.reshape(L, G * H, G * LANES)


def _importance_matrix(n_chunk):
    a, b = SEL_BLOCK // CMP_STRIDE, CMP_BLOCK // CMP_STRIDE
    overlap = np.convolve(np.ones(a), np.ones(b))
    n_cmp = n_chunk - (b - 1)
    m = np.zeros((MAX_SEL_BLOCKS, n_chunk), np.float32)
    for blk in range(n_chunk * CMP_STRIDE // SEL_BLOCK):
        for j, wgt in enumerate(overlap):
            i = a * blk + j - (b - 1)
            if 0 <= i < n_cmp:
                m[blk, i] = wgt
    return jnp.asarray(m, MXU_DTYPE)


def _rope_tables(seq, dim):
    inv_freq = 1.0 / (ROPE_THETA ** (jnp.arange(0, dim, 2, dtype=jnp.float32) / dim))
    ang = jnp.arange(seq, dtype=jnp.float32)[:, None] * inv_freq[None, :]
    ang = jnp.concatenate([ang, ang], axis=-1)
    return jnp.cos(ang), jnp.sin(ang)


def kernel(x, attn_norm, w_in, cmp_pos_k, cmp_w1_k, cmp_w2_k, cmp_pos_v, cmp_w1_v, cmp_w2_v, mla_q_norm,
           w_q_up, mla_kv_norm, w_kv_up, nsa_out_norm, mla_out_norm, w_out, mlp_norm, w_ff1, w_ff2, final_norm):
    B, S, D = x.shape
    depth = w_in.shape[0]
    assert S % SEL_BLOCK == 0 and S // SEL_BLOCK <= MAX_SEL_BLOCKS and S % CMP_STRIDE == 0
    n_chunk = S // CMP_STRIDE

    cos64, sin64 = _rope_tables(S, NSA_HEAD_DIM)
    cos_a, sin_a = jnp.concatenate([cos64, cos64], -1), jnp.concatenate([sin64, sin64], -1)
    cos32, sin32 = _rope_tables(S, MLA_ROPE_DIM)
    pad = jnp.zeros((S, LANES - MLA_NOPE_DIM - MLA_ROPE_DIM), jnp.float32)
    cos_b = jnp.concatenate([jnp.ones((S, MLA_NOPE_DIM), jnp.float32), cos32, pad], -1)
    sin_b = jnp.concatenate([jnp.zeros((S, MLA_NOPE_DIM), jnp.float32), sin32, pad], -1)
    tabs = (cos_a, sin_a, cos64.T, sin64.T, cos_b, sin_b, cos_b.T, sin_b.T)

    w_pm, w_fm = _pack_in_proj(w_in)
    wq2t = _pack_q_up(w_q_up)
    wk2, wv2t = _pack_kv_up(w_kv_up)
    posk, w1k, w2k = _pack_compress(cmp_pos_k, cmp_w1_k, cmp_w2_k)
    posv, w1v, w2v = _pack_compress(cmp_pos_v, cmp_w1_v, cmp_w2_v)
    w2vt = jnp.swapaxes(w2v, 1, 2)
    imp_mat = _importance_matrix(n_chunk)
    wo = w_out.astype(MXU_DTYPE)
    w1 = w_ff1.astype(MXU_DTYPE)
    w2 = w_ff2.astype(MXU_DTYPE)
    row = lambda v: v.reshape(1, -1)

    for l in range(depth):
        (qt_nsa, k_sel, k_win, k_c, v_c, vt_sel, vt_win, gates_t, qt_mla, k_mla, vt_mla) = _proj_call(
            x, row(attn_norm[l]), w_pm[l], w_fm[l], tabs,
            row(mla_q_norm[l]), wq2t[l], row(mla_kv_norm[l]), wk2[l], wv2t[l])
        k_cmp, vt_cmp = _compress_call(
            k_c.reshape(B, n_chunk, CMP_STRIDE * LANES), v_c.reshape(B, n_chunk, CMP_STRIDE * LANES),
            posk[l], posv[l], w1k[l], w1v[l], w2k[l], w2vt[l])
        o_a = _nsa_call(qt_nsa, k_cmp, vt_cmp, k_sel, vt_sel, k_win, vt_win, gates_t, imp_mat)
        o_b = _mla_call(qt_mla, k_mla, vt_mla)
        x = _mix_mlp_call(
            x.reshape(B * S, D), o_a.reshape(B * S, NSA_WIDTH), o_b.reshape(B * S, MLA_WIDTH),
            row(nsa_out_norm[l]), row(mla_out_norm[l]), wo[l, :NSA_WIDTH], wo[l, NSA_WIDTH:],
            row(mlp_norm[l]), w1[l], w2[l], row(final_norm), final=(l == depth - 1)).reshape(B, S, D)
    return x
```

```python
import functools

import numpy as np
import jax
import jax.numpy as jnp
from jax import lax
from jax.experimental import pallas as pl
from jax.experimental.pallas import tpu as pltpu

D_MODEL = 1024
NSA_HEADS = 8
NSA_KV_GROUPS = 2
NSA_REP = NSA_HEADS // NSA_KV_GROUPS
NSA_HEAD_DIM = 64
NSA_BRANCHES = 3
CMP_BLOCK = 32
CMP_STRIDE = 16
CMP_HIDDEN = 256
SEL_BLOCK = 64
SEL_BLOCK_LOG2 = 6
N_SELECT = 16
N_LOCAL = 2
WINDOW = 512
MLA_HEADS = 8
MLA_NOPE_DIM = 64
MLA_ROPE_DIM = 32
MLA_V_DIM = 64
Q_LORA_RANK = 256
KV_LORA_RANK = 128
D_FF = 4 * D_MODEL
ROPE_THETA = 10000.0
NORM_EPS = 1e-6
NEG_INF = -1e30
FORCE_BONUS = 1e4
LOG2_E = 1.4426950408889634
NSA_WIDTH = NSA_HEADS * NSA_HEAD_DIM
MLA_WIDTH = MLA_HEADS * MLA_V_DIM

LANES = 128
HALF = LANES // 2
V_ROWS = HALF + 16
MAX_SEL_BLOCKS = HALF
MXU_DTYPE = jnp.bfloat16
VMEM_LIMIT = 52 * 1024 * 1024
PROJ_ROWS = 512
ATTN_K_TILE = 256
ATTN_K_SPLIT = 2
NSA_Q_TILE = ATTN_K_TILE * ATTN_K_SPLIT
NSA_SUB_TILE = 128
WIN_V_TILE = 128
MLA_TILE = ATTN_K_TILE * ATTN_K_SPLIT
MLA_HEADS_PER_STEP = 8
MLP_ROWS = 512
MLP_FF_TILE = 2048

_N_ROPE_BLK = 3
_OFF_ROT = _N_ROPE_BLK * LANES
_OFF_VC = 2 * _OFF_ROT
_OFF_CQ = _OFF_VC + LANES
_OFF_CKV = _OFF_CQ + Q_LORA_RANK
_OFF_KRA = _OFF_CKV + KV_LORA_RANK
_OFF_KRB = _OFF_KRA + LANES
_PROJ_COLS = _OFF_KRB + LANES
_FT_QROT = NSA_WIDTH
_FT_VSEL = 2 * NSA_WIDTH
_FT_VWIN = _FT_VSEL + NSA_KV_GROUPS * NSA_HEAD_DIM
_FT_GATE = _FT_VWIN + NSA_KV_GROUPS * NSA_HEAD_DIM
_GATE_ROWS = 32
_FT_ROWS = _FT_GATE + _GATE_ROWS


def _dot(a, b):
    return jnp.dot(a, b, preferred_element_type=jnp.float32)


def _dot_nt(a, b):
    return lax.dot_general(a, b, (((1,), (1,)), ((), ())), preferred_element_type=jnp.float32)


def _rms(x, gain):
    return x * lax.rsqrt(jnp.mean(x * x, axis=-1, keepdims=True) + NORM_EPS) * gain


def _ones_row_block(n):
    return jnp.where(lax.broadcasted_iota(jnp.int32, (V_ROWS - HALF, n), 0) == 0, 1.0, 0.0)


def _flash_update_t(carry, s_t, v_t, scale):
    m, acc = carry
    c = scale * LOG2_E
    m_new = jnp.maximum(m, jnp.max(s_t, axis=0, keepdims=True))
    alpha = jnp.exp2((m - m_new) * c)
    p_t = jnp.exp2((s_t - m_new).astype(MXU_DTYPE) * c)
    return m_new, alpha * acc + _dot(v_t, p_t)


def _flash_init_t(n):
    return jnp.full((1, n), NEG_INF, jnp.float32), jnp.zeros((V_ROWS, n), jnp.float32)


def _flash_merge_t(states, scale):
    c = scale * LOG2_E
    m = states[0][0]
    for st in states[1:]:
        m = jnp.maximum(m, st[0])
    acc = jnp.exp2((states[0][0] - m) * c) * states[0][1]
    for st in states[1:]:
        acc = acc + jnp.exp2((st[0] - m) * c) * st[1]
    return m, acc


def _flash_finish_t(carry):
    acc = carry[1]
    return acc[:HALF] / jnp.maximum(acc[HALF:HALF + 1], 1e-30)


def _proj_kernel(x_ref, gain_ref, w_ref, wt_ref, cos_a_ref, sin_a_ref, cos_at_ref, sin_at_ref,
                 cos_b_ref, sin_b_ref, cos_bt_ref, sin_bt_ref, qn_ref, wq2t_ref, kvn_ref, wk2_ref, wv2t_ref,
                 q_nsa_ref, k_sel_ref, k_win_ref, kc_ref, vc_ref, vt_sel_ref, vt_win_ref, gate_ref,
                 q_mla_ref, k_mla_ref, vt_mla_ref, chunk_ref):
    rows = x_ref.shape[1]
    hn = _rms(x_ref[0], gain_ref[...]).astype(MXU_DTYPE)
    res = _dot(hn, w_ref[...])
    feat_t = _dot_nt(wt_ref[...], hn)
    ones_rows = _ones_row_block(rows)

    cos_at, sin_at = cos_at_ref[...], sin_at_ref[...]
    scale = NSA_HEAD_DIM ** -0.5
    for h in range(NSA_HEADS):
        lo = h * NSA_HEAD_DIM
        q_t = feat_t[lo:lo + NSA_HEAD_DIM] * cos_at + feat_t[_FT_QROT + lo:_FT_QROT + lo + NSA_HEAD_DIM] * sin_at
        q_nsa_ref[0, h] = (q_t * scale).astype(q_nsa_ref.dtype)
    for off, out_ref in ((_FT_VSEL, vt_sel_ref), (_FT_VWIN, vt_win_ref)):
        tile = out_ref.shape[4]
        for g in range(NSA_KV_GROUPS):
            lo = off + g * NSA_HEAD_DIM
            v_t = jnp.concatenate([feat_t[lo:lo + NSA_HEAD_DIM], ones_rows], axis=0)
            for j in range(rows // tile):
                out_ref[0, g, j] = v_t[:, j * tile:(j + 1) * tile].astype(out_ref.dtype)
    gate_ref[0] = jax.nn.sigmoid(feat_t[_FT_GATE:_FT_GATE + _GATE_ROWS])

    cos_a, sin_a = cos_a_ref[...], sin_a_ref[...]
    lane = lax.broadcasted_iota(jnp.int32, (rows, LANES), 1)
    lower = lane < HALF

    def rope_blk(j):
        return (res[:, j * LANES:(j + 1) * LANES] * cos_a
                + res[:, _OFF_ROT + j * LANES:_OFF_ROT + (j + 1) * LANES] * sin_a)

    def split_pair(blk):
        return jnp.where(lower, blk, 0.0), jnp.where(lower, pltpu.roll(blk, HALF, 1), 0.0)

    for out_ref, val in ((kc_ref, rope_blk(0)), (vc_ref, res[:, _OFF_VC:_OFF_VC + LANES])):
        chunk_ref[...] = val
        for p in range(CMP_STRIDE):
            out_ref[0, p] = chunk_ref[pl.ds(p, rows // CMP_STRIDE, stride=CMP_STRIDE), :]
    pos = pl.program_id(1) * rows + lax.broadcasted_iota(jnp.int32, (rows, LANES), 0)
    onehot = jnp.where((lane - HALF) == jnp.right_shift(pos, SEL_BLOCK_LOG2), 1.0, 0.0)
    a, b = split_pair(rope_blk(1))
    k_sel_ref[0, 0] = (a + onehot).astype(k_sel_ref.dtype)
    k_sel_ref[0, 1] = (b + onehot).astype(k_sel_ref.dtype)
    a, b = split_pair(rope_blk(2))
    k_win_ref[0, 0] = a.astype(k_win_ref.dtype)
    k_win_ref[0, 1] = b.astype(k_win_ref.dtype)

    cos_b, sin_b = cos_b_ref[...], sin_b_ref[...]
    cos_bt, sin_bt = cos_bt_ref[...], sin_bt_ref[...]
    cqn = _rms(res[:, _OFF_CQ:_OFF_CQ + Q_LORA_RANK], qn_ref[...]).astype(MXU_DTYPE)
    yq_t = _dot_nt(wq2t_ref[...], cqn)
    rot_off = MLA_HEADS * LANES
    for h in range(MLA_HEADS):
        q_t = (yq_t[h * LANES:(h + 1) * LANES] * cos_bt
               + yq_t[rot_off + h * LANES:rot_off + (h + 1) * LANES] * sin_bt)
        q_mla_ref[0, h] = q_t.astype(q_mla_ref.dtype)
    ckvn = _rms(res[:, _OFF_CKV:_OFF_CKV + KV_LORA_RANK], kvn_ref[...]).astype(MXU_DTYPE)
    yk = _dot(ckvn, wk2_ref[...])
    k_pe = res[:, _OFF_KRA:_OFF_KRA + LANES] * cos_b + res[:, _OFF_KRB:_OFF_KRB + LANES] * sin_b
    for h in range(MLA_HEADS):
        k_mla_ref[0, h] = (yk[:, h * LANES:(h + 1) * LANES] + k_pe).astype(k_mla_ref.dtype)
    yv_t = _dot_nt(wv2t_ref[...], ckvn)
    tk = vt_mla_ref.shape[4]
    for h in range(MLA_HEADS):
        v_t = jnp.concatenate([yv_t[h * MLA_V_DIM:(h + 1) * MLA_V_DIM], ones_rows], axis=0)
        for j in range(rows // tk):
            vt_mla_ref[0, h, j] = v_t[:, j * tk:(j + 1) * tk].astype(vt_mla_ref.dtype)


def _proj_call(x, gain, w_pm, w_fm, tabs, qn, wq2t, kvn, wk2, wv2t):
    B, S, D = x.shape
    rows = min(PROJ_ROWS, S)
    ns = S // rows
    tk = min(ATTN_K_TILE, S)
    tw = min(WIN_V_TILE, S)
    const = lambda b, i: (0, 0)
    tab = lambda t: (pl.BlockSpec((rows, t.shape[1]), lambda b, i: (i, 0)) if t.shape[0] == S
                     else pl.BlockSpec((t.shape[0], rows), lambda b, i: (0, i)))
    hm = lambda n: pl.BlockSpec((1, n, rows, LANES), lambda b, i: (b, 0, i, 0))
    fm = lambda n, f: pl.BlockSpec((1, n, f, rows), lambda b, i: (b, 0, 0, i))
    tiles = lambda n, t: pl.BlockSpec((1, n, rows // t, V_ROWS, t), lambda b, i: (b, 0, i, 0, 0))
    chunks = pl.BlockSpec((1, CMP_STRIDE, rows // CMP_STRIDE, LANES), lambda b, i: (b, 0, i, 0))
    sds = jax.ShapeDtypeStruct
    G = NSA_KV_GROUPS
    return pl.pallas_call(
        _proj_kernel,
        grid=(B, ns),
        in_specs=[
            pl.BlockSpec((1, rows, D), lambda b, i: (b, i, 0)),
            pl.BlockSpec((1, D), const),
            pl.BlockSpec(w_pm.shape, const),
            pl.BlockSpec(w_fm.shape, const),
            *[tab(t) for t in tabs],
            pl.BlockSpec((1, Q_LORA_RANK), const),
            pl.BlockSpec(wq2t.shape, const),
            pl.BlockSpec((1, KV_LORA_RANK), const),
            pl.BlockSpec(wk2.shape, const),
            pl.BlockSpec(wv2t.shape, const),
        ],
        out_specs=[fm(NSA_HEADS, NSA_HEAD_DIM), hm(G), hm(G), chunks, chunks, tiles(G, tk), tiles(G, tw),
                   pl.BlockSpec((1, _GATE_ROWS, rows), lambda b, i: (b, 0, i)),
                   fm(MLA_HEADS, LANES), hm(MLA_HEADS), tiles(MLA_HEADS, tk)],
        out_shape=[
            sds((B, NSA_HEADS, NSA_HEAD_DIM, S), MXU_DTYPE),
            sds((B, G, S, LANES), MXU_DTYPE),
            sds((B, G, S, LANES), MXU_DTYPE),
            sds((B, CMP_STRIDE, S // CMP_STRIDE, LANES), jnp.float32),
            sds((B, CMP_STRIDE, S // CMP_STRIDE, LANES), jnp.float32),
            sds((B, G, S // tk, V_ROWS, tk), MXU_DTYPE),
            sds((B, G, S // tw, V_ROWS, tw), MXU_DTYPE),
            sds((B, _GATE_ROWS, S), jnp.float32),
            sds((B, MLA_HEADS, LANES, S), MXU_DTYPE),
            sds((B, MLA_HEADS, S, LANES), MXU_DTYPE),
            sds((B, MLA_HEADS, S // tk, V_ROWS, tk), MXU_DTYPE),
        ],
        scratch_shapes=[pltpu.VMEM((rows, LANES), jnp.float32)],
        compiler_params=pltpu.CompilerParams(
            dimension_semantics=("parallel", "parallel"), vmem_limit_bytes=VMEM_LIMIT),
        name="proj",
    )(x, gain, w_pm, w_fm, *tabs, qn, wq2t, kvn, wk2, wv2t)


def _compress_kernel(xk_ref, xv_ref, posk_ref, posv_ref, w1k_ref, w1v_ref, w2k_ref, w2vt_ref,
                     kcmp_ref, vtcmp_ref):
    n_chunk = xk_ref.shape[2]

    def hidden(x_ref, pos_ref, w1_ref):
        x = jnp.concatenate([x_ref[0, p] for p in range(CMP_STRIDE)], axis=1)
        h_lo = _dot((x + pos_ref[0:1, :]).astype(MXU_DTYPE), w1_ref[0])
        h_hi = _dot((x + pos_ref[1:2, :]).astype(MXU_DTYPE), w1_ref[1])
        hid = h_lo + pltpu.roll(h_hi, n_chunk - 1, 0)
        return (hid * jax.nn.sigmoid(hid)).astype(MXU_DTYPE)

    k_out = _dot(hidden(xk_ref, posk_ref, w1k_ref), w2k_ref[...])
    v_out_t = _dot_nt(w2vt_ref[...], hidden(xv_ref, posv_ref, w1v_ref))
    for g in range(NSA_KV_GROUPS):
        kcmp_ref[0, g] = k_out[:, g * LANES:(g + 1) * LANES].astype(kcmp_ref.dtype)
        vtcmp_ref[0, g] = v_out_t[g * LANES:(g + 1) * LANES].astype(vtcmp_ref.dtype)


def _compress_call(xk, xv, posk, posv, w1k, w1v, w2k, w2vt):
    B, _, n_chunk, _ = xk.shape
    xs = pl.BlockSpec((1, CMP_STRIDE, n_chunk, LANES), lambda b: (b, 0, 0, 0))
    c2 = lambda a: pl.BlockSpec(a.shape, lambda b: (0,) * a.ndim)
    G = NSA_KV_GROUPS
    return pl.pallas_call(
        _compress_kernel,
        grid=(B,),
        in_specs=[xs, xs, c2(posk), c2(posv), c2(w1k), c2(w1v), c2(w2k), c2(w2vt)],
        out_specs=[pl.BlockSpec((1, G, n_chunk, LANES), lambda b: (b, 0, 0, 0)),
                   pl.BlockSpec((1, G, LANES, n_chunk), lambda b: (b, 0, 0, 0))],
        out_shape=[jax.ShapeDtypeStruct((B, G, n_chunk, LANES), MXU_DTYPE),
                   jax.ShapeDtypeStruct((B, G, LANES, n_chunk), MXU_DTYPE)],
        compiler_params=pltpu.CompilerParams(
            dimension_semantics=("parallel",), vmem_limit_bytes=VMEM_LIMIT),
        name="compress",
    )(xk, xv, posk, posv, w1k, w1v, w2k, w2vt)


def _nsa_kernel(qt_ref, kcmp_ref, vtcmp_ref, ksel_ref, vtsel_ref, kwin_ref, vtwin_ref, gate_ref,
                imp_ref, cbias_ref, wbias_ref, o_ref, *, n_sel, win_keys, sub):
    g = pl.program_id(1)
    tq = qt_ref.shape[3]
    n_sub = tq // sub
    n_chunk = kcmp_ref.shape[2]
    tk = vtsel_ref.shape[4]
    tw = vtwin_ref.shape[4]
    lanes = NSA_REP * sub
    tile0 = pl.program_id(2) * tq
    m_t = imp_ref[...]
    blk8 = lax.broadcasted_iota(jnp.int32, (8, sub), 0)
    lane_pos = jnp.bitwise_and(lax.broadcasted_iota(jnp.int32, (1, lanes), 1), sub - 1)

    def front(u):
        s0 = tile0 + u * sub
        q_t = jnp.concatenate([qt_ref[0, r, :, u * sub:(u + 1) * sub] for r in range(NSA_REP)], axis=1)
        q_t0 = jnp.concatenate([q_t, jnp.zeros_like(q_t)], axis=0)
        t_row = s0 + lane_pos

        c_off = pl.multiple_of(n_chunk - s0 // CMP_STRIDE, 8)
        s = _dot(kcmp_ref[0, 0], q_t0) + cbias_ref[pl.ds(c_off, n_chunk), :]
        e = jnp.exp(s - jnp.max(s, axis=0, keepdims=True))
        any_valid = t_row >= CMP_BLOCK - 1
        inv = jnp.where(any_valid, 1.0 / jnp.maximum(jnp.sum(e, axis=0, keepdims=True), 1e-30), 0.0)
        p_cmp = e * inv
        o_cmp = _dot(vtcmp_ref[0, 0], p_cmp.astype(MXU_DTYPE))[:HALF]

        p_sum = p_cmp[:, 0:sub]
        for r in range(1, NSA_REP):
            p_sum = p_sum + p_cmp[:, r * sub:(r + 1) * sub]
        hi = p_sum.astype(MXU_DTYPE)
        rem = p_sum - hi.astype(jnp.float32)
        mid = rem.astype(MXU_DTYPE)
        lo = (rem - mid.astype(jnp.float32)).astype(MXU_DTYPE)
        imp_t = _dot(m_t, hi) + _dot(m_t, mid) + _dot(m_t, lo)

        blk = lax.broadcasted_iota(jnp.int32, (MAX_SEL_BLOCKS, sub), 0)
        cur = jnp.right_shift(s0 + lax.broadcasted_iota(jnp.int32, (MAX_SEL_BLOCKS, sub), 1), SEL_BLOCK_LOG2)
        valid = blk <= cur
        forced = valid & ((blk == 0) | (blk > cur - N_LOCAL))
        score = jnp.where(forced, FORCE_BONUS, jnp.where(valid, imp_t, -1.0))
        n_grp = MAX_SEL_BLOCKS // 8
        grp = [score[8 * a:8 * (a + 1)] for a in range(n_grp)]
        def count_group(rank, src):
            rank = list(rank)
            for j in range(8 * src, 8 * src + 8):
                row = score[j:j + 1, :]
                for a in range(n_grp):
                    if a > src:
                        beats = jnp.where(row >= grp[a], 1.0, 0.0)
                    elif a < src:
                        beats = jnp.where(row > grp[a], 1.0, 0.0)
                    else:
                        beats = jnp.where(blk8 > j % 8, jnp.where(row >= grp[a], 1.0, 0.0),
                                          jnp.where(row > grp[a], 1.0, 0.0))
                    rank[a] = rank[a] + beats
            return tuple(rank)

        rank = tuple(jnp.zeros((8, sub), jnp.float32) for _ in range(n_grp))
        for src in range(n_grp):
            rank = count_group(rank, src)
        bias_t = jnp.where(jnp.concatenate(rank, axis=0) < n_sel, 0.0, NEG_INF).astype(MXU_DTYPE)
        q_aug = jnp.concatenate([q_t, jnp.concatenate([bias_t] * NSA_REP, axis=1)], axis=0)

        w_start = pl.multiple_of(jnp.maximum(s0 - WINDOW, 0), sub)
        w_off = pl.multiple_of(WINDOW - (s0 - w_start), sub)
        s = _dot(kwin_ref[0, 0, pl.ds(w_start, win_keys), :], q_t0) + wbias_ref[pl.ds(w_off, win_keys), :]
        e = jnp.exp((s - jnp.max(s, axis=0, keepdims=True)).astype(MXU_DTYPE))
        w_tile = w_start // tw
        v_t = jnp.concatenate([vtwin_ref[0, 0, w_tile + j] for j in range(win_keys // tw)], axis=1)
        o_win = _flash_finish_t((None, _dot(v_t, e)))
        return q_aug, t_row, o_cmp, o_win

    fronts = [front(u) for u in range(n_sub)]

    split = ATTN_K_SPLIT
    chains = [(u, j) for u in range(n_sub) for j in range(split)]

    def sel_step(it, state, diagonal):
        active = [c for c, (u, j) in enumerate(chains) if not (diagonal and j * tk >= (u + 1) * sub)]
        n_keys = {c: min(tk, (chains[c][0] + 1) * sub - chains[c][1] * tk) if diagonal else tk for c in active}
        s_ts = {}
        for c in active:
            u, j = chains[c]
            start = pl.multiple_of((it * split + j) * tk, tk)
            s_ts[c] = _dot(ksel_ref[0, 0, pl.ds(start, n_keys[c]), :], fronts[u][0])
        if diagonal:
            for c in active:
                u, j = chains[c]
                if j * tk + n_keys[c] > u * sub:
                    key = (it * split + j) * tk + lax.broadcasted_iota(jnp.int32, (n_keys[c], lanes), 0)
                    s_ts[c] = jnp.where(key <= fronts[u][1], s_ts[c], NEG_INF)
        out = list(state)
        for c in active:
            v_t = vtsel_ref[0, 0, it * split + chains[c][1]]
            out[c] = _flash_update_t(state[c], s_ts[c], v_t[:, :n_keys[c]], 1.0)
        return tuple(out)

    n_full = pl.program_id(2)
    carry = lax.fori_loop(0, n_full, lambda it, c: sel_step(it, c, False),
                          tuple(_flash_init_t(lanes) for _ in chains))
    carry = sel_step(n_full, carry, True)

    for u in range(n_sub):
        _, _, o_cmp, o_win = fronts[u]
        o_sel = _flash_finish_t(_flash_merge_t(carry[u * split:(u + 1) * split], 1.0))
        gates = gate_ref[0, :, u * sub:(u + 1) * sub]

        def gate(r, br, gates=gates):
            r0 = r * NSA_BRANCHES + br
            r1 = (NSA_REP + r) * NSA_BRANCHES + br
            return jnp.where(g == 0, gates[r0:r0 + 1], gates[r1:r1 + 1])

        heads = []
        for r in range(NSA_REP):
            sl = slice(r * sub, (r + 1) * sub)
            heads.append(gate(r, 0) * o_cmp[:, sl] + gate(r, 1) * o_sel[:, sl] + gate(r, 2) * o_win[:, sl])
        for pair in range(NSA_REP // 2):
            o_ref[0, u * sub:(u + 1) * sub, pair * LANES:(pair + 1) * LANES] = jnp.concatenate(
                [heads[2 * pair], heads[2 * pair + 1]], axis=0).T


def _mask_tables(n_chunk, sub, win_keys):
    pos = (np.arange(NSA_REP * sub) % sub)[None, :]
    r = np.arange(2 * n_chunk)[:, None]
    cmp_ok = CMP_STRIDE * (r - n_chunk) + CMP_BLOCK - 1 <= pos
    r = np.arange(WINDOW + win_keys)[:, None]
    win_ok = (r <= pos + WINDOW) & (r > pos)
    to_bias = lambda ok: jnp.asarray(np.where(ok, 0.0, NEG_INF), jnp.float32)
    return to_bias(cmp_ok), to_bias(win_ok)


def _nsa_call(qt_nsa, k_cmp, vt_cmp, k_sel, vt_sel, k_win, vt_win, gates_t, imp_mat):
    B, _, _, S = qt_nsa.shape
    n_chunk = k_cmp.shape[2]
    tq = min(NSA_Q_TILE, S)
    sub = min(NSA_SUB_TILE, tq)
    tk, tw = vt_sel.shape[4], vt_win.shape[4]
    assert ATTN_K_SPLIT * tk == tq and tq % sub == 0 and sub == LANES and sub % tw == 0
    n_sel = min(N_SELECT, S // SEL_BLOCK)
    win_keys = min(WINDOW + sub, S)
    cmp_bias, win_bias = _mask_tables(n_chunk, sub, win_keys)
    const = lambda a: pl.BlockSpec(a.shape, lambda b, g, i: (0, 0))
    kv = lambda n: pl.BlockSpec((1, 1, n, LANES), lambda b, g, i: (b, g, 0, 0))
    vt = lambda a: pl.BlockSpec((1, 1) + a.shape[2:], lambda b, g, i: (b, g) + (0,) * (a.ndim - 2))
    return pl.pallas_call(
        functools.partial(_nsa_kernel, n_sel=n_sel, win_keys=win_keys, sub=sub),
        grid=(B, NSA_KV_GROUPS, S // tq),
        in_specs=[
            pl.BlockSpec((1, NSA_REP, NSA_HEAD_DIM, tq), lambda b, g, i: (b, g, 0, i)),
            kv(n_chunk), vt(vt_cmp), kv(S), vt(vt_sel), kv(S), vt(vt_win),
            pl.BlockSpec((1, _GATE_ROWS, tq), lambda b, g, i: (b, 0, i)),
            const(imp_mat), const(cmp_bias), const(win_bias),
        ],
        out_specs=pl.BlockSpec((1, tq, NSA_REP * NSA_HEAD_DIM), lambda b, g, i: (b, i, g)),
        out_shape=jax.ShapeDtypeStruct((B, S, NSA_WIDTH), jnp.float32),
        compiler_params=pltpu.CompilerParams(
            dimension_semantics=("parallel", "parallel", "arbitrary"), vmem_limit_bytes=VMEM_LIMIT),
        name="nsa",
    )(qt_nsa, k_cmp, vt_cmp, k_sel, vt_sel, k_win, vt_win, gates_t, imp_mat, cmp_bias, win_bias)


def _mla_kernel(qt_ref, k_ref, vt_ref, o_ref):
    n_heads = qt_ref.shape[1]
    tq = qt_ref.shape[3]
    tk = vt_ref.shape[4]
    q0 = pl.program_id(2) * tq
    scale = (MLA_NOPE_DIM + MLA_ROPE_DIM) ** -0.5
    t_row = q0 + lax.broadcasted_iota(jnp.int32, (1, tq), 1)
    qts = [qt_ref[0, h] for h in range(n_heads)]
    split = tq // tk
    chains = [(h, j) for h in range(n_heads) for j in range(split)]

    def step(it, state, causal):
        lo = [j * tk if causal else 0 for _, j in chains]
        s_ts = []
        for c, (h, j) in enumerate(chains):
            start = pl.multiple_of((it * split + j) * tk, tk)
            s_ts.append(_dot(k_ref[0, h, pl.ds(start, tk), :], qts[h][:, lo[c]:]))
        if causal:
            for c, (h, j) in enumerate(chains):
                key = (it * split + j) * tk + lax.broadcasted_iota(jnp.int32, (tk, tq - lo[c]), 0)
                s_ts[c] = jnp.where(key <= t_row[:, lo[c]:], s_ts[c], NEG_INF)
        out = []
        for c, (h, j) in enumerate(chains):
            m, acc = state[c]
            m_new, acc_new = _flash_update_t((m[:, lo[c]:], acc[:, lo[c]:]), s_ts[c],
                                             vt_ref[0, h, it * split + j], scale)
            if lo[c]:
                m_new = jnp.concatenate([m[:, :lo[c]], m_new], axis=1)
                acc_new = jnp.concatenate([acc[:, :lo[c]], acc_new], axis=1)
            out.append((m_new, acc_new))
        return tuple(out)

    n_full = pl.program_id(2)
    carry = lax.fori_loop(0, n_full, lambda it, c: step(it, c, False),
                          tuple(_flash_init_t(tq) for _ in chains))
    carry = step(n_full, carry, True)
    outs = [_flash_finish_t(_flash_merge_t(carry[h * split:(h + 1) * split], scale)) for h in range(n_heads)]
    for p in range(n_heads // 2):
        o_ref[0, :, p * LANES:(p + 1) * LANES] = jnp.concatenate([outs[2 * p], outs[2 * p + 1]], axis=0).T


def _mla_call(qt_mla, k_mla, vt_mla):
    B, _, _, S = qt_mla.shape
    tq = min(MLA_TILE, S)
    tk = vt_mla.shape[4]
    nh = MLA_HEADS_PER_STEP
    assert tq % tk == 0 and MLA_HEADS % nh == 0 and nh % 2 == 0
    return pl.pallas_call(
        _mla_kernel,
        grid=(B, MLA_HEADS // nh, S // tq),
        in_specs=[
            pl.BlockSpec((1, nh, LANES, tq), lambda b, p, i: (b, p, 0, i)),
            pl.BlockSpec((1, nh, S, LANES), lambda b, p, i: (b, p, 0, 0)),
            pl.BlockSpec((1, nh, S // tk, V_ROWS, tk), lambda b, p, i: (b, p, 0, 0, 0)),
        ],
        out_specs=pl.BlockSpec((1, tq, nh * MLA_V_DIM), lambda b, p, i: (b, i, p)),
        out_shape=jax.ShapeDtypeStruct((B, S, MLA_WIDTH), jnp.float32),
        compiler_params=pltpu.CompilerParams(
            dimension_semantics=("parallel", "parallel", "arbitrary"), vmem_limit_bytes=VMEM_LIMIT),
        name="mla",
    )(qt_mla, k_mla, vt_mla)


def _mix_mlp_kernel(x_ref, oa_ref, ob_ref, na_ref, nb_ref, woa_ref, wob_ref, mn_ref, w1_ref, w2_ref,
                    fn_ref, o_ref, x1_ref, hn_ref, acc_ref, *, final):
    f = pl.program_id(1)

    @pl.when(f == 0)
    def _():
        na = _rms(oa_ref[...], na_ref[...]).astype(MXU_DTYPE)
        nb = _rms(ob_ref[...], nb_ref[...]).astype(MXU_DTYPE)
        x1 = x_ref[...] + (_dot(na, woa_ref[...]) + _dot(nb, wob_ref[...]))
        x1_ref[...] = x1
        hn_ref[...] = _rms(x1, mn_ref[...]).astype(MXU_DTYPE)
        acc_ref[...] = jnp.zeros_like(acc_ref)

    a = jnp.square(jnp.maximum(_dot(hn_ref[...], w1_ref[...]), 0.0))
    acc_ref[...] += _dot(a.astype(MXU_DTYPE), w2_ref[...])

    @pl.when(f == pl.num_programs(1) - 1)
    def _():
        y = x1_ref[...] + acc_ref[...]
        o_ref[...] = _rms(y, fn_ref[...]) if final else y


def _mix_mlp_call(x, o_a, o_b, na, nb, wo_a, wo_b, mn, w1, w2, fn, final):
    T, D = x.shape
    rows = min(MLP_ROWS, T)
    tf = MLP_FF_TILE
    row = lambda w: pl.BlockSpec((rows, w), lambda i, f: (i, 0))
    const = lambda a: pl.BlockSpec(a.shape, lambda i, f: (0, 0))
    return pl.pallas_call(
        functools.partial(_mix_mlp_kernel, final=final),
        grid=(T // rows, D_FF // tf),
        in_specs=[row(D), row(NSA_WIDTH), row(MLA_WIDTH), const(na), const(nb), const(wo_a), const(wo_b),
                  const(mn),
                  pl.BlockSpec((D, tf), lambda i, f: (0, f)),
                  pl.BlockSpec((tf, D), lambda i, f: (f, 0)),
                  const(fn)],
        out_specs=row(D),
        out_shape=jax.ShapeDtypeStruct((T, D), jnp.float32),
        scratch_shapes=[pltpu.VMEM((rows, D), jnp.float32), pltpu.VMEM((rows, D), MXU_DTYPE),
                        pltpu.VMEM((rows, D), jnp.float32)],
        compiler_params=pltpu.CompilerParams(
            dimension_semantics=("parallel", "arbitrary"), vmem_limit_bytes=VMEM_LIMIT),
        name="mix_mlp",
    )(x, o_a, o_b, na, nb, wo_a, wo_b, mn, w1, w2, fn)


def _rot_cols(w, dim):
    lead = w.shape[:-1]
    w4 = w.reshape(lead + (w.shape[-1] // dim, 2, dim // 2))
    return jnp.concatenate([-w4[..., 1, :], w4[..., 0, :]], axis=-1).reshape(w.shape)


def _pad_cols(w, left, total):
    pad = [(0, 0)] * (w.ndim - 1) + [(left, total - left - w.shape[-1])]
    return jnp.pad(w, pad)


def _pack_in_proj(w_in):
    sizes = (NSA_WIDTH,) + (NSA_KV_GROUPS * NSA_HEAD_DIM,) * 6 + (
        NSA_BRANCHES * NSA_HEADS, Q_LORA_RANK, KV_LORA_RANK, MLA_ROPE_DIM)
    offs = np.cumsum((0,) + sizes)
    q_a, k_c, v_c, k_s, v_s, k_w, v_w, g_a, c_q, c_kv, k_r = (
        w_in[..., offs[i]:offs[i + 1]] for i in range(len(sizes)))
    roped = jnp.concatenate([k_c, k_s, k_w], axis=-1)
    w_pm = jnp.concatenate(
        [roped, _rot_cols(roped, NSA_HEAD_DIM), v_c, c_q, c_kv,
         _pad_cols(k_r, HALF, LANES), _pad_cols(_rot_cols(k_r, MLA_ROPE_DIM), HALF, LANES)], axis=-1)
    w_fm = jnp.concatenate([q_a, _rot_cols(q_a, NSA_HEAD_DIM), v_s, v_w, _pad_cols(g_a, 0, _GATE_ROWS)], axis=-1)
    assert w_pm.shape[-1] == _PROJ_COLS and w_fm.shape[-1] == _FT_ROWS
    return w_pm.astype(MXU_DTYPE), jnp.swapaxes(w_fm, 1, 2).astype(MXU_DTYPE)


def _pack_q_up(w_q_up):
    L, R, _ = w_q_up.shape
    w = w_q_up.reshape(L, R, MLA_HEADS, MLA_NOPE_DIM + MLA_ROPE_DIM)
    a = _pad_cols(w, 0, LANES)
    b = _pad_cols(_rot_cols(w[..., MLA_NOPE_DIM:], MLA_ROPE_DIM), MLA_NOPE_DIM, LANES)
    packed = jnp.concatenate([a.reshape(L, R, -1), b.reshape(L, R, -1)], axis=-1)
    return jnp.swapaxes(packed, 1, 2).astype(MXU_DTYPE)


def _pack_kv_up(w_kv_up):
    L, R, _ = w_kv_up.shape
    w = w_kv_up.reshape(L, R, MLA_HEADS, MLA_NOPE_DIM + MLA_V_DIM)
    k = _pad_cols(w[..., :MLA_NOPE_DIM], 0, LANES).reshape(L, R, -1)
    v_t = jnp.swapaxes(w[..., MLA_NOPE_DIM:].reshape(L, R, -1), 1, 2)
    return k.astype(MXU_DTYPE), v_t.astype(MXU_DTYPE)


def _pack_compress(pos, w1, w2):
    L = pos.shape[0]
    G, dk, H = NSA_KV_GROUPS, NSA_HEAD_DIM, CMP_HIDDEN
    pos2 = jnp.concatenate([pos] * G, axis=-1).reshape(L, 2, CMP_STRIDE * G * dk)
    on_diag = lambda w, g, axis: jnp.stack([w if g2 == g else jnp.zeros_like(w) for g2 in range(G)], axis=axis)
    w1r = w1.astype(MXU_DTYPE).reshape(L, 2, CMP_STRIDE, dk, H)
    w1b = jnp.stack([on_diag(w1r, g, 4) for g in range(G)], axis=3)
    w1b = w1b.reshape(L, 2, CMP_STRIDE * G * dk, G * H)
    w2p = _pad_cols(w2.astype(MXU_DTYPE), 0, LANES)
    w2b = jnp.stack([on_diag(w2p, g, 2) for g in range(G)], axis=1)
    return pos2, w1b, w2b.reshape(L, G * H, G * LANES)


def _importance_matrix(n_chunk):
    a, b = SEL_BLOCK // CMP_STRIDE, CMP_BLOCK // CMP_STRIDE
    overlap = np.convolve(np.ones(a), np.ones(b))
    n_cmp = n_chunk - (b - 1)
    m = np.zeros((MAX_SEL_BLOCKS, n_chunk), np.float32)
    for blk in range(n_chunk * CMP_STRIDE // SEL_BLOCK):
        for j, wgt in enumerate(overlap):
            i = a * blk + j - (b - 1)
            if 0 <= i < n_cmp:
                m[blk, i] = wgt
    return jnp.asarray(m, MXU_DTYPE)


def _rope_tables(seq, dim):
    inv_freq = 1.0 / (ROPE_THETA ** (jnp.arange(0, dim, 2, dtype=jnp.float32) / dim))
    ang = jnp.arange(seq, dtype=jnp.float32)[:, None] * inv_freq[None, :]
    ang = jnp.concatenate([ang, ang], axis=-1)
    return jnp.cos(ang), jnp.sin(ang)


def kernel(x, attn_norm, w_in, cmp_pos_k, cmp_w1_k, cmp_w2_k, cmp_pos_v, cmp_w1_v, cmp_w2_v, mla_q_norm,
           w_q_up, mla_kv_norm, w_kv_up, nsa_out_norm, mla_out_norm, w_out, mlp_norm, w_ff1, w_ff2, final_norm):
    B, S, D = x.shape
    depth = w_in.shape[0]
    assert S % SEL_BLOCK == 0 and S // SEL_BLOCK <= MAX_SEL_BLOCKS and S % CMP_STRIDE == 0
    n_chunk = S // CMP_STRIDE

    cos64, sin64 = _rope_tables(S, NSA_HEAD_DIM)
    cos_a, sin_a = jnp.concatenate([cos64, cos64], -1), jnp.concatenate([sin64, sin64], -1)
    cos32, sin32 = _rope_tables(S, MLA_ROPE_DIM)
    pad = jnp.zeros((S, LANES - MLA_NOPE_DIM - MLA_ROPE_DIM), jnp.float32)
    cos_b = jnp.concatenate([jnp.ones((S, MLA_NOPE_DIM), jnp.float32), cos32, pad], -1)
    sin_b = jnp.concatenate([jnp.zeros((S, MLA_NOPE_DIM), jnp.float32), sin32, pad], -1)
    tabs = (cos_a, sin_a, cos64.T, sin64.T, cos_b, sin_b, cos_b.T, sin_b.T)

    w_pm, w_fm = _pack_in_proj(w_in)
    wq2t = _pack_q_up(w_q_up)
    wk2, wv2t = _pack_kv_up(w_kv_up)
    posk, w1k, w2k = _pack_compress(cmp_pos_k, cmp_w1_k, cmp_w2_k)
    posv, w1v, w2v = _pack_compress(cmp_pos_v, cmp_w1_v, cmp_w2_v)
    w2vt = jnp.swapaxes(w2v, 1, 2)
    imp_mat = _importance_matrix(n_chunk)
    wo = w_out.astype(MXU_DTYPE)
    w1 = w_ff1.astype(MXU_DTYPE)
    w2 = w_ff2.astype(MXU_DTYPE)
    row = lambda v: v.reshape(1, -1)

    for l in range(depth):
        (qt_nsa, k_sel, k_win, k_c, v_c, vt_sel, vt_win, gates_t, qt_mla, k_mla, vt_mla) = _proj_call(
            x, row(attn_norm[l]), w_pm[l], w_fm[l], tabs,
            row(mla_q_norm[l]), wq2t[l], row(mla_kv_norm[l]), wk2[l], wv2t[l])
        k_cmp, vt_cmp = _compress_call(
            k_c, v_c, posk[l], posv[l], w1k[l], w1v[l], w2k[l], w2vt[l])
        o_a = _nsa_call(qt_nsa, k_cmp, vt_cmp, k_sel, vt_sel, k_win, vt_win, gates_t, imp_mat)
        o_b = _mla_call(qt_mla, k_mla, vt_mla)
        x = _mix_mlp_call(
            x.reshape(B * S, D), o_a.reshape(B * S, NSA_WIDTH), o_b.reshape(B * S, MLA_WIDTH),
            row(nsa_out_norm[l]), row(mla_out_norm[l]), wo[l, :NSA_WIDTH], wo[l, NSA_WIDTH:],
            row(mlp_norm[l]), w1[l], w2[l], row(final_norm), final=(l == depth - 1)).reshape(B, S, D)
    return x
```

```python
import functools

import numpy as np
import jax
import jax.numpy as jnp
from jax import lax
from jax.experimental import pallas as pl
from jax.experimental.pallas import tpu as pltpu

D_MODEL = 1024
NSA_HEADS = 8
NSA_KV_GROUPS = 2
NSA_REP = NSA_HEADS // NSA_KV_GROUPS
NSA_HEAD_DIM = 64
NSA_BRANCHES = 3
CMP_BLOCK = 32
CMP_STRIDE = 16
CMP_HIDDEN = 256
SEL_BLOCK = 64
SEL_BLOCK_LOG2 = 6
N_SELECT = 16
N_LOCAL = 2
WINDOW = 512
MLA_HEADS = 8
MLA_NOPE_DIM = 64
MLA_ROPE_DIM = 32
MLA_V_DIM = 64
Q_LORA_RANK = 256
KV_LORA_RANK = 128
D_FF = 4 * D_MODEL
ROPE_THETA = 10000.0
NORM_EPS = 1e-6
NEG_INF = -1e30
TAKEN = -2.0
LOG2_E = 1.4426950408889634
NSA_WIDTH = NSA_HEADS * NSA_HEAD_DIM
MLA_WIDTH = MLA_HEADS * MLA_V_DIM

LANES = 128
HALF = LANES // 2
V_ROWS = HALF + 16
MAX_SEL_BLOCKS = HALF
MXU_DTYPE = jnp.bfloat16
VMEM_LIMIT = 52 * 1024 * 1024
PROJ_ROWS = 512
ATTN_K_TILE = 256
ATTN_K_SPLIT = 2
NSA_Q_TILE = ATTN_K_TILE * ATTN_K_SPLIT
NSA_SUB_TILE = 128
WIN_V_TILE = 128
MLA_TILE = ATTN_K_TILE * ATTN_K_SPLIT
MLA_HEADS_PER_STEP = 8
MLP_ROWS = 512
MLP_FF_TILE = 2048

_N_ROPE_BLK = 3
_OFF_ROT = _N_ROPE_BLK * LANES
_OFF_VC = 2 * _OFF_ROT
_OFF_CQ = _OFF_VC + LANES
_OFF_CKV = _OFF_CQ + Q_LORA_RANK
_OFF_KRA = _OFF_CKV + KV_LORA_RANK
_OFF_KRB = _OFF_KRA + LANES
_PROJ_COLS = _OFF_KRB + LANES
_FT_QROT = NSA_WIDTH
_FT_VSEL = 2 * NSA_WIDTH
_FT_VWIN = _FT_VSEL + NSA_KV_GROUPS * NSA_HEAD_DIM
_FT_GATE = _FT_VWIN + NSA_KV_GROUPS * NSA_HEAD_DIM
_GATE_ROWS = 32
_FT_ROWS = _FT_GATE + _GATE_ROWS


def _dot(a, b):
    return jnp.dot(a, b, preferred_element_type=jnp.float32)


def _dot_nt(a, b):
    return lax.dot_general(a, b, (((1,), (1,)), ((), ())), preferred_element_type=jnp.float32)


def _rms(x, gain):
    return x * lax.rsqrt(jnp.mean(x * x, axis=-1, keepdims=True) + NORM_EPS) * gain


def _ones_row_block(n):
    return jnp.where(lax.broadcasted_iota(jnp.int32, (V_ROWS - HALF, n), 0) == 0, 1.0, 0.0)


def _flash_update_t(carry, s_t, v_t, scale):
    m, acc = carry
    c = scale * LOG2_E
    m_new = jnp.maximum(m, jnp.max(s_t, axis=0, keepdims=True))
    alpha = jnp.exp2((m - m_new) * c)
    p_t = jnp.exp2((s_t - m_new).astype(MXU_DTYPE) * c)
    return m_new, alpha * acc + _dot(v_t, p_t)


def _flash_init_t(n):
    return jnp.full((1, n), NEG_INF, jnp.float32), jnp.zeros((V_ROWS, n), jnp.float32)


def _flash_merge_t(states, scale):
    c = scale * LOG2_E
    m = states[0][0]
    for st in states[1:]:
        m = jnp.maximum(m, st[0])
    acc = jnp.exp2((states[0][0] - m) * c) * states[0][1]
    for st in states[1:]:
        acc = acc + jnp.exp2((st[0] - m) * c) * st[1]
    return m, acc


def _flash_finish_t(carry):
    acc = carry[1]
    return acc[:HALF] / jnp.maximum(acc[HALF:HALF + 1], 1e-30)


def _proj_kernel(x_ref, gain_ref, w_ref, wt_ref, cos_a_ref, sin_a_ref, cos_at_ref, sin_at_ref,
                 cos_b_ref, sin_b_ref, cos_bt_ref, sin_bt_ref, qn_ref, wq2t_ref, kvn_ref, wk2_ref, wv2t_ref,
                 q_nsa_ref, k_sel_ref, k_win_ref, kc_ref, vc_ref, vt_sel_ref, vt_win_ref, gate_ref,
                 q_mla_ref, k_mla_ref, vt_mla_ref, chunk_ref):
    rows = x_ref.shape[1]
    hn = _rms(x_ref[0], gain_ref[...]).astype(MXU_DTYPE)
    res = _dot(hn, w_ref[...])
    feat_t = _dot_nt(wt_ref[...], hn)
    ones_rows = _ones_row_block(rows)

    cos_at, sin_at = cos_at_ref[...], sin_at_ref[...]
    scale = NSA_HEAD_DIM ** -0.5
    for h in range(NSA_HEADS):
        lo = h * NSA_HEAD_DIM
        q_t = feat_t[lo:lo + NSA_HEAD_DIM] * cos_at + feat_t[_FT_QROT + lo:_FT_QROT + lo + NSA_HEAD_DIM] * sin_at
        q_nsa_ref[0, h] = (q_t * scale).astype(q_nsa_ref.dtype)
    for off, out_ref in ((_FT_VSEL, vt_sel_ref), (_FT_VWIN, vt_win_ref)):
        tile = out_ref.shape[4]
        for g in range(NSA_KV_GROUPS):
            lo = off + g * NSA_HEAD_DIM
            v_t = jnp.concatenate([feat_t[lo:lo + NSA_HEAD_DIM], ones_rows], axis=0)
            for j in range(rows // tile):
                out_ref[0, g, j] = v_t[:, j * tile:(j + 1) * tile].astype(out_ref.dtype)
    gate_ref[0] = jax.nn.sigmoid(feat_t[_FT_GATE:_FT_GATE + _GATE_ROWS])

    cos_a, sin_a = cos_a_ref[...], sin_a_ref[...]
    lane = lax.broadcasted_iota(jnp.int32, (rows, LANES), 1)
    lower = lane < HALF

    def rope_blk(j):
        return (res[:, j * LANES:(j + 1) * LANES] * cos_a
                + res[:, _OFF_ROT + j * LANES:_OFF_ROT + (j + 1) * LANES] * sin_a)

    def split_pair(blk):
        return jnp.where(lower, blk, 0.0), jnp.where(lower, pltpu.roll(blk, HALF, 1), 0.0)

    for out_ref, val in ((kc_ref, rope_blk(0)), (vc_ref, res[:, _OFF_VC:_OFF_VC + LANES])):
        chunk_ref[...] = val
        for p in range(CMP_STRIDE):
            out_ref[0, p] = chunk_ref[pl.ds(p, rows // CMP_STRIDE, stride=CMP_STRIDE), :]
    pos = pl.program_id(1) * rows + lax.broadcasted_iota(jnp.int32, (rows, LANES), 0)
    onehot = jnp.where((lane - HALF) == jnp.right_shift(pos, SEL_BLOCK_LOG2), 1.0, 0.0)
    a, b = split_pair(rope_blk(1))
    k_sel_ref[0, 0] = (a + onehot).astype(k_sel_ref.dtype)
    k_sel_ref[0, 1] = (b + onehot).astype(k_sel_ref.dtype)
    a, b = split_pair(rope_blk(2))
    k_win_ref[0, 0] = a.astype(k_win_ref.dtype)
    k_win_ref[0, 1] = b.astype(k_win_ref.dtype)

    cos_b, sin_b = cos_b_ref[...], sin_b_ref[...]
    cos_bt, sin_bt = cos_bt_ref[...], sin_bt_ref[...]
    cqn = _rms(res[:, _OFF_CQ:_OFF_CQ + Q_LORA_RANK], qn_ref[...]).astype(MXU_DTYPE)
    yq_t = _dot_nt(wq2t_ref[...], cqn)
    rot_off = MLA_HEADS * LANES
    for h in range(MLA_HEADS):
        q_t = (yq_t[h * LANES:(h + 1) * LANES] * cos_bt
               + yq_t[rot_off + h * LANES:rot_off + (h + 1) * LANES] * sin_bt)
        q_mla_ref[0, h] = q_t.astype(q_mla_ref.dtype)
    ckvn = _rms(res[:, _OFF_CKV:_OFF_CKV + KV_LORA_RANK], kvn_ref[...]).astype(MXU_DTYPE)
    yk = _dot(ckvn, wk2_ref[...])
    k_pe = res[:, _OFF_KRA:_OFF_KRA + LANES] * cos_b + res[:, _OFF_KRB:_OFF_KRB + LANES] * sin_b
    for h in range(MLA_HEADS):
        k_mla_ref[0, h] = (yk[:, h * LANES:(h + 1) * LANES] + k_pe).astype(k_mla_ref.dtype)
    yv_t = _dot_nt(wv2t_ref[...], ckvn)
    tk = vt_mla_ref.shape[4]
    for h in range(MLA_HEADS):
        v_t = jnp.concatenate([yv_t[h * MLA_V_DIM:(h + 1) * MLA_V_DIM], ones_rows], axis=0)
        for j in range(rows // tk):
            vt_mla_ref[0, h, j] = v_t[:, j * tk:(j + 1) * tk].astype(vt_mla_ref.dtype)


def _proj_call(x, gain, w_pm, w_fm, tabs, qn, wq2t, kvn, wk2, wv2t):
    B, S, D = x.shape
    rows = min(PROJ_ROWS, S)
    ns = S // rows
    tk = min(ATTN_K_TILE, S)
    tw = min(WIN_V_TILE, S)
    const = lambda b, i: (0, 0)
    tab = lambda t: (pl.BlockSpec((rows, t.shape[1]), lambda b, i: (i, 0)) if t.shape[0] == S
                     else pl.BlockSpec((t.shape[0], rows), lambda b, i: (0, i)))
    hm = lambda n: pl.BlockSpec((1, n, rows, LANES), lambda b, i: (b, 0, i, 0))
    fm = lambda n, f: pl.BlockSpec((1, n, f, rows), lambda b, i: (b, 0, 0, i))
    tiles = lambda n, t: pl.BlockSpec((1, n, rows // t, V_ROWS, t), lambda b, i: (b, 0, i, 0, 0))
    chunks = pl.BlockSpec((1, CMP_STRIDE, rows // CMP_STRIDE, LANES), lambda b, i: (b, 0, i, 0))
    sds = jax.ShapeDtypeStruct
    G = NSA_KV_GROUPS
    return pl.pallas_call(
        _proj_kernel,
        grid=(B, ns),
        in_specs=[
            pl.BlockSpec((1, rows, D), lambda b, i: (b, i, 0)),
            pl.BlockSpec((1, D), const),
            pl.BlockSpec(w_pm.shape, const),
            pl.BlockSpec(w_fm.shape, const),
            *[tab(t) for t in tabs],
            pl.BlockSpec((1, Q_LORA_RANK), const),
            pl.BlockSpec(wq2t.shape, const),
            pl.BlockSpec((1, KV_LORA_RANK), const),
            pl.BlockSpec(wk2.shape, const),
            pl.BlockSpec(wv2t.shape, const),
        ],
        out_specs=[fm(NSA_HEADS, NSA_HEAD_DIM), hm(G), hm(G), chunks, chunks, tiles(G, tk), tiles(G, tw),
                   pl.BlockSpec((1, _GATE_ROWS, rows), lambda b, i: (b, 0, i)),
                   fm(MLA_HEADS, LANES), hm(MLA_HEADS), tiles(MLA_HEADS, tk)],
        out_shape=[
            sds((B, NSA_HEADS, NSA_HEAD_DIM, S), MXU_DTYPE),
            sds((B, G, S, LANES), MXU_DTYPE),
            sds((B, G, S, LANES), MXU_DTYPE),
            sds((B, CMP_STRIDE, S // CMP_STRIDE, LANES), jnp.float32),
            sds((B, CMP_STRIDE, S // CMP_STRIDE, LANES), jnp.float32),
            sds((B, G, S // tk, V_ROWS, tk), MXU_DTYPE),
            sds((B, G, S // tw, V_ROWS, tw), MXU_DTYPE),
            sds((B, _GATE_ROWS, S), jnp.float32),
            sds((B, MLA_HEADS, LANES, S), MXU_DTYPE),
            sds((B, MLA_HEADS, S, LANES), MXU_DTYPE),
            sds((B, MLA_HEADS, S // tk, V_ROWS, tk), MXU_DTYPE),
        ],
        scratch_shapes=[pltpu.VMEM((rows, LANES), jnp.float32)],
        compiler_params=pltpu.CompilerParams(
            dimension_semantics=("parallel", "parallel"), vmem_limit_bytes=VMEM_LIMIT),
        name="proj",
    )(x, gain, w_pm, w_fm, *tabs, qn, wq2t, kvn, wk2, wv2t)


def _compress_kernel(xk_ref, xv_ref, posk_ref, posv_ref, w1k_ref, w1v_ref, w2k_ref, w2vt_ref,
                     kcmp_ref, vtcmp_ref):
    n_chunk = xk_ref.shape[2]

    def hidden(x_ref, pos_ref, w1_ref):
        x = jnp.concatenate([x_ref[0, p] for p in range(CMP_STRIDE)], axis=1)
        h_lo = _dot((x + pos_ref[0:1, :]).astype(MXU_DTYPE), w1_ref[0])
        h_hi = _dot((x + pos_ref[1:2, :]).astype(MXU_DTYPE), w1_ref[1])
        hid = h_lo + pltpu.roll(h_hi, n_chunk - 1, 0)
        return (hid * jax.nn.sigmoid(hid)).astype(MXU_DTYPE)

    k_out = _dot(hidden(xk_ref, posk_ref, w1k_ref), w2k_ref[...])
    v_out_t = _dot_nt(w2vt_ref[...], hidden(xv_ref, posv_ref, w1v_ref))
    for g in range(NSA_KV_GROUPS):
        kcmp_ref[0, g] = k_out[:, g * LANES:(g + 1) * LANES].astype(kcmp_ref.dtype)
        vtcmp_ref[0, g] = v_out_t[g * LANES:(g + 1) * LANES].astype(vtcmp_ref.dtype)


def _compress_call(xk, xv, posk, posv, w1k, w1v, w2k, w2vt):
    B, _, n_chunk, _ = xk.shape
    xs = pl.BlockSpec((1, CMP_STRIDE, n_chunk, LANES), lambda b: (b, 0, 0, 0))
    c2 = lambda a: pl.BlockSpec(a.shape, lambda b: (0,) * a.ndim)
    G = NSA_KV_GROUPS
    return pl.pallas_call(
        _compress_kernel,
        grid=(B,),
        in_specs=[xs, xs, c2(posk), c2(posv), c2(w1k), c2(w1v), c2(w2k), c2(w2vt)],
        out_specs=[pl.BlockSpec((1, G, n_chunk, LANES), lambda b: (b, 0, 0, 0)),
                   pl.BlockSpec((1, G, LANES, n_chunk), lambda b: (b, 0, 0, 0))],
        out_shape=[jax.ShapeDtypeStruct((B, G, n_chunk, LANES), MXU_DTYPE),
                   jax.ShapeDtypeStruct((B, G, LANES, n_chunk), MXU_DTYPE)],
        compiler_params=pltpu.CompilerParams(
            dimension_semantics=("parallel",), vmem_limit_bytes=VMEM_LIMIT),
        name="compress",
    )(xk, xv, posk, posv, w1k, w1v, w2k, w2vt)


def _nsa_kernel(qt_ref, kcmp_ref, vtcmp_ref, ksel_ref, vtsel_ref, kwin_ref, vtwin_ref, gate_ref,
                imp_ref, cbias_ref, wbias_ref, o_ref, *, n_sel, win_keys, sub):
    g = pl.program_id(1)
    tq = qt_ref.shape[3]
    n_sub = tq // sub
    n_chunk = kcmp_ref.shape[2]
    tk = vtsel_ref.shape[4]
    tw = vtwin_ref.shape[4]
    lanes = NSA_REP * sub
    tile0 = pl.program_id(2) * tq
    m_t = imp_ref[...]
    lane_pos = jnp.bitwise_and(lax.broadcasted_iota(jnp.int32, (1, lanes), 1), sub - 1)

    def front(u):
        s0 = tile0 + u * sub
        q_t = jnp.concatenate([qt_ref[0, r, :, u * sub:(u + 1) * sub] for r in range(NSA_REP)], axis=1)
        q_t0 = jnp.concatenate([q_t, jnp.zeros_like(q_t)], axis=0)
        t_row = s0 + lane_pos

        c_off = pl.multiple_of(n_chunk - s0 // CMP_STRIDE, 8)
        s = _dot(kcmp_ref[0, 0], q_t0) + cbias_ref[pl.ds(c_off, n_chunk), :]
        e = jnp.exp(s - jnp.max(s, axis=0, keepdims=True))
        any_valid = t_row >= CMP_BLOCK - 1
        inv = jnp.where(any_valid, 1.0 / jnp.maximum(jnp.sum(e, axis=0, keepdims=True), 1e-30), 0.0)
        p_cmp = e * inv
        o_cmp = _dot(vtcmp_ref[0, 0], p_cmp.astype(MXU_DTYPE))[:HALF]

        p_sum = p_cmp[:, 0:sub]
        for r in range(1, NSA_REP):
            p_sum = p_sum + p_cmp[:, r * sub:(r + 1) * sub]
        hi = p_sum.astype(MXU_DTYPE)
        rem = p_sum - hi.astype(jnp.float32)
        mid = rem.astype(MXU_DTYPE)
        lo = (rem - mid.astype(jnp.float32)).astype(MXU_DTYPE)
        imp_t = _dot(m_t, hi) + _dot(m_t, mid) + _dot(m_t, lo)

        blk = lax.broadcasted_iota(jnp.int32, (MAX_SEL_BLOCKS, sub), 0)
        cur = jnp.right_shift(s0 + lax.broadcasted_iota(jnp.int32, (MAX_SEL_BLOCKS, sub), 1), SEL_BLOCK_LOG2)
        valid = blk <= cur
        forced = valid & ((blk == 0) | (blk > cur - N_LOCAL))
        blk_f = blk.astype(jnp.float32)
        bias_t = jnp.where(forced, 0.0, NEG_INF)
        left = jnp.where(forced, TAKEN, jnp.where(valid, imp_t, -1.0))
        for _ in range(n_sel - (N_LOCAL + 1)):
            top = jnp.max(left, axis=0, keepdims=True)
            first = jnp.min(jnp.where(left == top, blk_f, float(MAX_SEL_BLOCKS)), axis=0, keepdims=True)
            hit = blk_f == first
            bias_t = jnp.where(hit, 0.0, bias_t)
            left = jnp.where(hit, TAKEN, left)
        bias_t = bias_t.astype(MXU_DTYPE)
        q_aug = jnp.concatenate([q_t, jnp.concatenate([bias_t] * NSA_REP, axis=1)], axis=0)

        w_start = pl.multiple_of(jnp.maximum(s0 - WINDOW, 0), sub)
        w_off = pl.multiple_of(WINDOW - (s0 - w_start), sub)
        s = _dot(kwin_ref[0, 0, pl.ds(w_start, win_keys), :], q_t0) + wbias_ref[pl.ds(w_off, win_keys), :]
        e = jnp.exp((s - jnp.max(s, axis=0, keepdims=True)).astype(MXU_DTYPE))
        w_tile = w_start // tw
        v_t = jnp.concatenate([vtwin_ref[0, 0, w_tile + j] for j in range(win_keys // tw)], axis=1)
        o_win = _flash_finish_t((None, _dot(v_t, e)))
        return q_aug, t_row, o_cmp, o_win

    fronts = [front(u) for u in range(n_sub)]

    split = ATTN_K_SPLIT
    chains = [(u, j) for u in range(n_sub) for j in range(split)]

    def sel_step(it, state, diagonal):
        active = [c for c, (u, j) in enumerate(chains) if not (diagonal and j * tk >= (u + 1) * sub)]
        n_keys = {c: min(tk, (chains[c][0] + 1) * sub - chains[c][1] * tk) if diagonal else tk for c in active}
        s_ts = {}
        for c in active:
            u, j = chains[c]
            start = pl.multiple_of((it * split + j) * tk, tk)
            s_ts[c] = _dot(ksel_ref[0, 0, pl.ds(start, n_keys[c]), :], fronts[u][0])
        if diagonal:
            for c in active:
                u, j = chains[c]
                if j * tk + n_keys[c] > u * sub:
                    key = (it * split + j) * tk + lax.broadcasted_iota(jnp.int32, (n_keys[c], lanes), 0)
                    s_ts[c] = jnp.where(key <= fronts[u][1], s_ts[c], NEG_INF)
        out = list(state)
        for c in active:
            v_t = vtsel_ref[0, 0, it * split + chains[c][1]]
            out[c] = _flash_update_t(state[c], s_ts[c], v_t[:, :n_keys[c]], 1.0)
        return tuple(out)

    n_full = pl.program_id(2)
    carry = lax.fori_loop(0, n_full, lambda it, c: sel_step(it, c, False),
                          tuple(_flash_init_t(lanes) for _ in chains))
    carry = sel_step(n_full, carry, True)

    for u in range(n_sub):
        _, _, o_cmp, o_win = fronts[u]
        o_sel = _flash_finish_t(_flash_merge_t(carry[u * split:(u + 1) * split], 1.0))
        gates = gate_ref[0, :, u * sub:(u + 1) * sub]

        def gate(r, br, gates=gates):
            r0 = r * NSA_BRANCHES + br
            r1 = (NSA_REP + r) * NSA_BRANCHES + br
            return jnp.where(g == 0, gates[r0:r0 + 1], gates[r1:r1 + 1])

        heads = []
        for r in range(NSA_REP):
            sl = slice(r * sub, (r + 1) * sub)
            heads.append(gate(r, 0) * o_cmp[:, sl] + gate(r, 1) * o_sel[:, sl] + gate(r, 2) * o_win[:, sl])
        for pair in range(NSA_REP // 2):
            o_ref[0, u * sub:(u + 1) * sub, pair * LANES:(pair + 1) * LANES] = jnp.concatenate(
                [heads[2 * pair], heads[2 * pair + 1]], axis=0).T


def _mask_tables(n_chunk, sub, win_keys):
    pos = (np.arange(NSA_REP * sub) % sub)[None, :]
    r = np.arange(2 * n_chunk)[:, None]
    cmp_ok = CMP_STRIDE * (r - n_chunk) + CMP_BLOCK - 1 <= pos
    r = np.arange(WINDOW + win_keys)[:, None]
    win_ok = (r <= pos + WINDOW) & (r > pos)
    to_bias = lambda ok: jnp.asarray(np.where(ok, 0.0, NEG_INF), jnp.float32)
    return to_bias(cmp_ok), to_bias(win_ok)


def _nsa_call(qt_nsa, k_cmp, vt_cmp, k_sel, vt_sel, k_win, vt_win, gates_t, imp_mat):
    B, _, _, S = qt_nsa.shape
    n_chunk = k_cmp.shape[2]
    tq = min(NSA_Q_TILE, S)
    sub = min(NSA_SUB_TILE, tq)
    tk, tw = vt_sel.shape[4], vt_win.shape[4]
    assert ATTN_K_SPLIT * tk == tq and tq % sub == 0 and sub == LANES and sub % tw == 0
    n_sel = min(N_SELECT, S // SEL_BLOCK)
    win_keys = min(WINDOW + sub, S)
    cmp_bias, win_bias = _mask_tables(n_chunk, sub, win_keys)
    const = lambda a: pl.BlockSpec(a.shape, lambda b, g, i: (0, 0))
    kv = lambda n: pl.BlockSpec((1, 1, n, LANES), lambda b, g, i: (b, g, 0, 0))
    vt = lambda a: pl.BlockSpec((1, 1) + a.shape[2:], lambda b, g, i: (b, g) + (0,) * (a.ndim - 2))
    return pl.pallas_call(
        functools.partial(_nsa_kernel, n_sel=n_sel, win_keys=win_keys, sub=sub),
        grid=(B, NSA_KV_GROUPS, S // tq),
        in_specs=[
            pl.BlockSpec((1, NSA_REP, NSA_HEAD_DIM, tq), lambda b, g, i: (b, g, 0, i)),
            kv(n_chunk), vt(vt_cmp), kv(S), vt(vt_sel), kv(S), vt(vt_win),
            pl.BlockSpec((1, _GATE_ROWS, tq), lambda b, g, i: (b, 0, i)),
            const(imp_mat), const(cmp_bias), const(win_bias),
        ],
        out_specs=pl.BlockSpec((1, tq, NSA_REP * NSA_HEAD_DIM), lambda b, g, i: (b, i, g)),
        out_shape=jax.ShapeDtypeStruct((B, S, NSA_WIDTH), jnp.float32),
        compiler_params=pltpu.CompilerParams(
            dimension_semantics=("parallel", "parallel", "arbitrary"), vmem_limit_bytes=VMEM_LIMIT),
        name="nsa",
    )(qt_nsa, k_cmp, vt_cmp, k_sel, vt_sel, k_win, vt_win, gates_t, imp_mat, cmp_bias, win_bias)


def _mla_kernel(qt_ref, k_ref, vt_ref, o_ref):
    n_heads = qt_ref.shape[1]
    tq = qt_ref.shape[3]
    tk = vt_ref.shape[4]
    q0 = pl.program_id(2) * tq
    scale = (MLA_NOPE_DIM + MLA_ROPE_DIM) ** -0.5
    t_row = q0 + lax.broadcasted_iota(jnp.int32, (1, tq), 1)
    qts = [qt_ref[0, h] for h in range(n_heads)]
    split = tq // tk
    chains = [(h, j) for h in range(n_heads) for j in range(split)]

    def step(it, state, causal):
        lo = [j * tk if causal else 0 for _, j in chains]

        def scores(c):
            h, j = chains[c]
            start = pl.multiple_of((it * split + j) * tk, tk)
            s_t = _dot(k_ref[0, h, pl.ds(start, tk), :], qts[h][:, lo[c]:])
            if causal:
                key = (it * split + j) * tk + lax.broadcasted_iota(jnp.int32, (tk, tq - lo[c]), 0)
                s_t = jnp.where(key <= t_row[:, lo[c]:], s_t, NEG_INF)
            return s_t

        def update(c, s_t):
            h, j = chains[c]
            m, acc = state[c]
            m_new, acc_new = _flash_update_t((m[:, lo[c]:], acc[:, lo[c]:]), s_t,
                                             vt_ref[0, h, it * split + j], scale)
            if lo[c]:
                m_new = jnp.concatenate([m[:, :lo[c]], m_new], axis=1)
                acc_new = jnp.concatenate([acc[:, :lo[c]], acc_new], axis=1)
            return m_new, acc_new

        s_ts = [scores(c) for c in range(len(chains))]
        return tuple(update(c, s_ts[c]) for c in range(len(chains)))

    n_full = pl.program_id(2)
    carry = lax.fori_loop(0, n_full, lambda it, c: step(it, c, False),
                          tuple(_flash_init_t(tq) for _ in chains))
    carry = step(n_full, carry, True)
    outs = [_flash_finish_t(_flash_merge_t(carry[h * split:(h + 1) * split], scale)) for h in range(n_heads)]
    for p in range(n_heads // 2):
        o_ref[0, :, p * LANES:(p + 1) * LANES] = jnp.concatenate([outs[2 * p], outs[2 * p + 1]], axis=0).T


def _mla_call(qt_mla, k_mla, vt_mla):
    B, _, _, S = qt_mla.shape
    tq = min(MLA_TILE, S)
    tk = vt_mla.shape[4]
    nh = MLA_HEADS_PER_STEP
    assert tq % tk == 0 and MLA_HEADS % nh == 0 and nh % 2 == 0
    return pl.pallas_call(
        _mla_kernel,
        grid=(B, MLA_HEADS // nh, S // tq),
        in_specs=[
            pl.BlockSpec((1, nh, LANES, tq), lambda b, p, i: (b, p, 0, i)),
            pl.BlockSpec((1, nh, S, LANES), lambda b, p, i: (b, p, 0, 0)),
            pl.BlockSpec((1, nh, S // tk, V_ROWS, tk), lambda b, p, i: (b, p, 0, 0, 0)),
        ],
        out_specs=pl.BlockSpec((1, tq, nh * MLA_V_DIM), lambda b, p, i: (b, i, p)),
        out_shape=jax.ShapeDtypeStruct((B, S, MLA_WIDTH), jnp.float32),
        compiler_params=pltpu.CompilerParams(
            dimension_semantics=("parallel", "parallel", "arbitrary"), vmem_limit_bytes=VMEM_LIMIT),
        name="mla",
    )(qt_mla, k_mla, vt_mla)


def _mix_mlp_kernel(x_ref, oa_ref, ob_ref, na_ref, nb_ref, woa_ref, wob_ref, mn_ref, w1_ref, w2_ref,
                    fn_ref, o_ref, x1_ref, hn_ref, acc_ref, *, final):
    f = pl.program_id(1)

    @pl.when(f == 0)
    def _():
        na = _rms(oa_ref[...], na_ref[...]).astype(MXU_DTYPE)
        nb = _rms(ob_ref[...], nb_ref[...]).astype(MXU_DTYPE)
        x1 = x_ref[...] + (_dot(na, woa_ref[...]) + _dot(nb, wob_ref[...]))
        x1_ref[...] = x1
        hn_ref[...] = _rms(x1, mn_ref[...]).astype(MXU_DTYPE)
        acc_ref[...] = jnp.zeros_like(acc_ref)

    a = jnp.square(jnp.maximum(_dot(hn_ref[...], w1_ref[...]), 0.0))
    acc_ref[...] += _dot(a.astype(MXU_DTYPE), w2_ref[...])

    @pl.when(f == pl.num_programs(1) - 1)
    def _():
        y = x1_ref[...] + acc_ref[...]
        o_ref[...] = _rms(y, fn_ref[...]) if final else y


def _mix_mlp_call(x, o_a, o_b, na, nb, wo_a, wo_b, mn, w1, w2, fn, final):
    T, D = x.shape
    rows = min(MLP_ROWS, T)
    tf = MLP_FF_TILE
    row = lambda w: pl.BlockSpec((rows, w), lambda i, f: (i, 0))
    const = lambda a: pl.BlockSpec(a.shape, lambda i, f: (0, 0))
    return pl.pallas_call(
        functools.partial(_mix_mlp_kernel, final=final),
        grid=(T // rows, D_FF // tf),
        in_specs=[row(D), row(NSA_WIDTH), row(MLA_WIDTH), const(na), const(nb), const(wo_a), const(wo_b),
                  const(mn),
                  pl.BlockSpec((D, tf), lambda i, f: (0, f)),
                  pl.BlockSpec((tf, D), lambda i, f: (f, 0)),
                  const(fn)],
        out_specs=row(D),
        out_shape=jax.ShapeDtypeStruct((T, D), jnp.float32),
        scratch_shapes=[pltpu.VMEM((rows, D), jnp.float32), pltpu.VMEM((rows, D), MXU_DTYPE),
                        pltpu.VMEM((rows, D), jnp.float32)],
        compiler_params=pltpu.CompilerParams(
            dimension_semantics=("parallel", "arbitrary"), vmem_limit_bytes=VMEM_LIMIT),
        name="mix_mlp",
    )(x, o_a, o_b, na, nb, wo_a, wo_b, mn, w1, w2, fn)


def _rot_cols(w, dim):
    lead = w.shape[:-1]
    w4 = w.reshape(lead + (w.shape[-1] // dim, 2, dim // 2))
    return jnp.concatenate([-w4[..., 1, :], w4[..., 0, :]], axis=-1).reshape(w.shape)


def _pad_cols(w, left, total):
    pad = [(0, 0)] * (w.ndim - 1) + [(left, total - left - w.shape[-1])]
    return jnp.pad(w, pad)


def _pack_in_proj(w_in):
    sizes = (NSA_WIDTH,) + (NSA_KV_GROUPS * NSA_HEAD_DIM,) * 6 + (
        NSA_BRANCHES * NSA_HEADS, Q_LORA_RANK, KV_LORA_RANK, MLA_ROPE_DIM)
    offs = np.cumsum((0,) + sizes)
    q_a, k_c, v_c, k_s, v_s, k_w, v_w, g_a, c_q, c_kv, k_r = (
        w_in[..., offs[i]:offs[i + 1]] for i in range(len(sizes)))
    roped = jnp.concatenate([k_c, k_s, k_w], axis=-1)
    w_pm = jnp.concatenate(
        [roped, _rot_cols(roped, NSA_HEAD_DIM), v_c, c_q, c_kv,
         _pad_cols(k_r, HALF, LANES), _pad_cols(_rot_cols(k_r, MLA_ROPE_DIM), HALF, LANES)], axis=-1)
    w_fm = jnp.concatenate([q_a, _rot_cols(q_a, NSA_HEAD_DIM), v_s, v_w, _pad_cols(g_a, 0, _GATE_ROWS)], axis=-1)
    assert w_pm.shape[-1] == _PROJ_COLS and w_fm.shape[-1] == _FT_ROWS
    return w_pm.astype(MXU_DTYPE), jnp.swapaxes(w_fm, 1, 2).astype(MXU_DTYPE)


def _pack_q_up(w_q_up):
    L, R, _ = w_q_up.shape
    w = w_q_up.reshape(L, R, MLA_HEADS, MLA_NOPE_DIM + MLA_ROPE_DIM)
    a = _pad_cols(w, 0, LANES)
    b = _pad_cols(_rot_cols(w[..., MLA_NOPE_DIM:], MLA_ROPE_DIM), MLA_NOPE_DIM, LANES)
    packed = jnp.concatenate([a.reshape(L, R, -1), b.reshape(L, R, -1)], axis=-1)
    return jnp.swapaxes(packed, 1, 2).astype(MXU_DTYPE)


def _pack_kv_up(w_kv_up):
    L, R, _ = w_kv_up.shape
    w = w_kv_up.reshape(L, R, MLA_HEADS, MLA_NOPE_DIM + MLA_V_DIM)
    k = _pad_cols(w[..., :MLA_NOPE_DIM], 0, LANES).reshape(L, R, -1)
    v_t = jnp.swapaxes(w[..., MLA_NOPE_DIM:].reshape(L, R, -1), 1, 2)
    return k.astype(MXU_DTYPE), v_t.astype(MXU_DTYPE)


def _pack_compress(pos, w1, w2):
    L = pos.shape[0]
    G, dk, H = NSA_KV_GROUPS, NSA_HEAD_DIM, CMP_HIDDEN
    pos2 = jnp.concatenate([pos] * G, axis=-1).reshape(L, 2, CMP_STRIDE * G * dk)
    on_diag = lambda w, g, axis: jnp.stack([w if g2 == g else jnp.zeros_like(w) for g2 in range(G)], axis=axis)
    w1r = w1.astype(MXU_DTYPE).reshape(L, 2, CMP_STRIDE, dk, H)
    w1b = jnp.stack([on_diag(w1r, g, 4) for g in range(G)], axis=3)
    w1b = w1b.reshape(L, 2, CMP_STRIDE * G * dk, G * H)
    w2p = _pad_cols(w2.astype(MXU_DTYPE), 0, LANES)
    w2b = jnp.stack([on_diag(w2p, g, 2) for g in range(G)], axis=1)
    return pos2, w1b, w2b.reshape(L, G * H, G * LANES)


def _importance_matrix(n_chunk):
    a, b = SEL_BLOCK // CMP_STRIDE, CMP_BLOCK // CMP_STRIDE
    overlap = np.convolve(np.ones(a), np.ones(b))
    n_cmp = n_chunk - (b - 1)
    m = np.zeros((MAX_SEL_BLOCKS, n_chunk), np.float32)
    for blk in range(n_chunk * CMP_STRIDE // SEL_BLOCK):
        for j, wgt in enumerate(overlap):
            i = a * blk + j - (b - 1)
            if 0 <= i < n_cmp:
                m[blk, i] = wgt
    return jnp.asarray(m, MXU_DTYPE)


def _rope_tables(seq, dim):
    inv_freq = 1.0 / (ROPE_THETA ** (jnp.arange(0, dim, 2, dtype=jnp.float32) / dim))
    ang = jnp.arange(seq, dtype=jnp.float32)[:, None] * inv_freq[None, :]
    ang = jnp.concatenate([ang, ang], axis=-1)
    return jnp.cos(ang), jnp.sin(ang)


def kernel(x, attn_norm, w_in, cmp_pos_k, cmp_w1_k, cmp_w2_k, cmp_pos_v, cmp_w1_v, cmp_w2_v, mla_q_norm,
           w_q_up, mla_kv_norm, w_kv_up, nsa_out_norm, mla_out_norm, w_out, mlp_norm, w_ff1, w_ff2, final_norm):
    B, S, D = x.shape
    depth = w_in.shape[0]
    assert S % SEL_BLOCK == 0 and S // SEL_BLOCK <= MAX_SEL_BLOCKS and S % CMP_STRIDE == 0
    n_chunk = S // CMP_STRIDE

    cos64, sin64 = _rope_tables(S, NSA_HEAD_DIM)
    cos_a, sin_a = jnp.concatenate([cos64, cos64], -1), jnp.concatenate([sin64, sin64], -1)
    cos32, sin32 = _rope_tables(S, MLA_ROPE_DIM)
    pad = jnp.zeros((S, LANES - MLA_NOPE_DIM - MLA_ROPE_DIM), jnp.float32)
    cos_b = jnp.concatenate([jnp.ones((S, MLA_NOPE_DIM), jnp.float32), cos32, pad], -1)
    sin_b = jnp.concatenate([jnp.zeros((S, MLA_NOPE_DIM), jnp.float32), sin32, pad], -1)
    tabs = (cos_a, sin_a, cos64.T, sin64.T, cos_b, sin_b, cos_b.T, sin_b.T)

    w_pm, w_fm = _pack_in_proj(w_in)
    wq2t = _pack_q_up(w_q_up)
    wk2, wv2t = _pack_kv_up(w_kv_up)
    posk, w1k, w2k = _pack_compress(cmp_pos_k, cmp_w1_k, cmp_w2_k)
    posv, w1v, w2v = _pack_compress(cmp_pos_v, cmp_w1_v, cmp_w2_v)
    w2vt = jnp.swapaxes(w2v, 1, 2)
    imp_mat = _importance_matrix(n_chunk)
    wo = w_out.astype(MXU_DTYPE)
    w1 = w_ff1.astype(MXU_DTYPE)
    w2 = w_ff2.astype(MXU_DTYPE)
    row = lambda v: v.reshape(1, -1)

    for l in range(depth):
        (qt_nsa, k_sel, k_win, k_c, v_c, vt_sel, vt_win, gates_t, qt_mla, k_mla, vt_mla) = _proj_call(
            x, row(attn_norm[l]), w_pm[l], w_fm[l], tabs,
            row(mla_q_norm[l]), wq2t[l], row(mla_kv_norm[l]), wk2[l], wv2t[l])
        k_cmp, vt_cmp = _compress_call(
            k_c, v_c, posk[l], posv[l], w1k[l], w1v[l], w2k[l], w2vt[l])
        o_a = _nsa_call(qt_nsa, k_cmp, vt_cmp, k_sel, vt_sel, k_win, vt_win, gates_t, imp_mat)
        o_b = _mla_call(qt_mla, k_mla, vt_mla)
        x = _mix_mlp_call(
            x.reshape(B * S, D), o_a.reshape(B * S, NSA_WIDTH), o_b.reshape(B * S, MLA_WIDTH),
            row(nsa_out_norm[l]), row(mla_out_norm[l]), wo[l, :NSA_WIDTH], wo[l, NSA_WIDTH:],
            row(mlp_norm[l]), w1[l], w2[l], row(final_norm), final=(l == depth - 1)).reshape(B, S, D)
    return x
```

```python
import functools

import numpy as np
import jax
import jax.numpy as jnp
from jax import lax
from jax.experimental import pallas as pl
from jax.experimental.pallas import tpu as pltpu

D_MODEL = 1024
NSA_HEADS = 8
NSA_KV_GROUPS = 2
NSA_REP = NSA_HEADS // NSA_KV_GROUPS
NSA_HEAD_DIM = 64
NSA_BRANCHES = 3
CMP_BLOCK = 32
CMP_STRIDE = 16
CMP_HIDDEN = 256
SEL_BLOCK = 64
SEL_BLOCK_LOG2 = 6
N_SELECT = 16
N_LOCAL = 2
WINDOW = 512
MLA_HEADS = 8
MLA_NOPE_DIM = 64
MLA_ROPE_DIM = 32
MLA_V_DIM = 64
Q_LORA_RANK = 256
KV_LORA_RANK = 128
D_FF = 4 * D_MODEL
ROPE_THETA = 10000.0
NORM_EPS = 1e-6
NEG_INF = -1e30
SOFTMAX_FLOOR = 1e-30
NOT_CAUSAL = -1.0
TAKEN = -2.0
LOG2_E = 1.4426950408889634
NSA_WIDTH = NSA_HEADS * NSA_HEAD_DIM
MLA_WIDTH = MLA_HEADS * MLA_V_DIM

LANES = 128
HALF = LANES // 2
BF16_SUBLANES = 16
V_ROWS = HALF + BF16_SUBLANES
MAX_SEL_BLOCKS = HALF
MXU_DTYPE = jnp.bfloat16
VMEM_LIMIT = 52 * 1024 * 1024
PROJ_ROWS = 512
ATTN_K_TILE = 256
ATTN_K_SPLIT = 2
NSA_Q_TILE = ATTN_K_TILE * ATTN_K_SPLIT
NSA_SUB_TILE = 128
WIN_V_TILE = 128
MLA_TILE = ATTN_K_TILE * ATTN_K_SPLIT
MLA_HEADS_PER_STEP = 8
MLP_ROWS = 512
MLP_FF_TILE = 2048

_N_ROPE_BLK = 3
_OFF_ROT = _N_ROPE_BLK * LANES
_OFF_VC = 2 * _OFF_ROT
_OFF_CQ = _OFF_VC + LANES
_OFF_CKV = _OFF_CQ + Q_LORA_RANK
_OFF_KRA = _OFF_CKV + KV_LORA_RANK
_OFF_KRB = _OFF_KRA + LANES
_PROJ_COLS = _OFF_KRB + LANES
_FT_QROT = NSA_WIDTH
_FT_VSEL = 2 * NSA_WIDTH
_FT_VWIN = _FT_VSEL + NSA_KV_GROUPS * NSA_HEAD_DIM
_FT_GATE = _FT_VWIN + NSA_KV_GROUPS * NSA_HEAD_DIM
_GATE_ROWS = 32
_FT_ROWS = _FT_GATE + _GATE_ROWS


def _dot(a, b):
    return jnp.dot(a, b, preferred_element_type=jnp.float32)


def _dot_nt(a, b):
    return lax.dot_general(a, b, (((1,), (1,)), ((), ())), preferred_element_type=jnp.float32)


def _rms(x, gain):
    return x * lax.rsqrt(jnp.mean(x * x, axis=-1, keepdims=True) + NORM_EPS) * gain


def _ones_row_block(n):
    return jnp.where(lax.broadcasted_iota(jnp.int32, (V_ROWS - HALF, n), 0) == 0, 1.0, 0.0)


def _flash_update_t(carry, s_t, v_t, scale):
    m, acc = carry
    c = scale * LOG2_E
    m_new = jnp.maximum(m, jnp.max(s_t, axis=0, keepdims=True))
    alpha = jnp.exp2((m - m_new) * c)
    p_t = jnp.exp2((s_t - m_new).astype(MXU_DTYPE) * c)
    return m_new, alpha * acc + _dot(v_t, p_t)


def _flash_init_t(n):
    return jnp.full((1, n), NEG_INF, jnp.float32), jnp.zeros((V_ROWS, n), jnp.float32)


def _flash_merge_t(states, scale):
    c = scale * LOG2_E
    m = states[0][0]
    for st in states[1:]:
        m = jnp.maximum(m, st[0])
    acc = jnp.exp2((states[0][0] - m) * c) * states[0][1]
    for st in states[1:]:
        acc = acc + jnp.exp2((st[0] - m) * c) * st[1]
    return m, acc


def _flash_finish_t(carry):
    acc = carry[1]
    return acc[:HALF] / jnp.maximum(acc[HALF:HALF + 1], SOFTMAX_FLOOR)


def _proj_kernel(x_ref, gain_ref, w_ref, wt_ref, cos_a_ref, sin_a_ref, cos_at_ref, sin_at_ref,
                 cos_b_ref, sin_b_ref, cos_rt_ref, sin_rt_ref, qn_ref, wq2t_ref, kvn_ref, wk2_ref, wv2t_ref,
                 q_nsa_ref, k_sel_ref, k_win_ref, kc_ref, vc_ref, vt_sel_ref, vt_win_ref, gate_ref,
                 q_mla_ref, k_mla_ref, vt_mla_ref, chunk_ref):
    rows = x_ref.shape[1]
    hn = _rms(x_ref[0], gain_ref[...]).astype(MXU_DTYPE)
    res = _dot(hn, w_ref[...])
    feat_t = _dot_nt(wt_ref[...], hn)
    ones_rows = _ones_row_block(rows)

    cos_at, sin_at = cos_at_ref[...], sin_at_ref[...]
    scale = NSA_HEAD_DIM ** -0.5
    for h in range(NSA_HEADS):
        lo = h * NSA_HEAD_DIM
        q_t = feat_t[lo:lo + NSA_HEAD_DIM] * cos_at + feat_t[_FT_QROT + lo:_FT_QROT + lo + NSA_HEAD_DIM] * sin_at
        q_nsa_ref[0, h] = (q_t * scale).astype(q_nsa_ref.dtype)
    for off, out_ref in ((_FT_VSEL, vt_sel_ref), (_FT_VWIN, vt_win_ref)):
        tile = out_ref.shape[4]
        for g in range(NSA_KV_GROUPS):
            lo = off + g * NSA_HEAD_DIM
            v_t = jnp.concatenate([feat_t[lo:lo + NSA_HEAD_DIM], ones_rows], axis=0)
            for j in range(rows // tile):
                out_ref[0, g, j] = v_t[:, j * tile:(j + 1) * tile].astype(out_ref.dtype)
    gate_ref[0] = jax.nn.sigmoid(feat_t[_FT_GATE:_FT_GATE + _GATE_ROWS])

    cos_a, sin_a = cos_a_ref[...], sin_a_ref[...]
    lane = lax.broadcasted_iota(jnp.int32, (rows, LANES), 1)
    lower = lane < HALF

    def rope_blk(j):
        return (res[:, j * LANES:(j + 1) * LANES] * cos_a
                + res[:, _OFF_ROT + j * LANES:_OFF_ROT + (j + 1) * LANES] * sin_a)

    def split_pair(blk):
        return jnp.where(lower, blk, 0.0), jnp.where(lower, pltpu.roll(blk, HALF, 1), 0.0)

    for out_ref, val in ((kc_ref, rope_blk(0)), (vc_ref, res[:, _OFF_VC:_OFF_VC + LANES])):
        chunk_ref[...] = val
        for p in range(CMP_STRIDE):
            out_ref[0, p] = chunk_ref[pl.ds(p, rows // CMP_STRIDE, stride=CMP_STRIDE), :]
    pos = pl.program_id(1) * rows + lax.broadcasted_iota(jnp.int32, (rows, LANES), 0)
    onehot = jnp.where((lane - HALF) == jnp.right_shift(pos, SEL_BLOCK_LOG2), 1.0, 0.0)
    a, b = split_pair(rope_blk(1))
    k_sel_ref[0, 0] = (a + onehot).astype(k_sel_ref.dtype)
    k_sel_ref[0, 1] = (b + onehot).astype(k_sel_ref.dtype)
    a, b = split_pair(rope_blk(2))
    k_win_ref[0, 0] = a.astype(k_win_ref.dtype)
    k_win_ref[0, 1] = b.astype(k_win_ref.dtype)

    cos_b, sin_b = cos_b_ref[...], sin_b_ref[...]
    cos_rt, sin_rt = cos_rt_ref[...], sin_rt_ref[...]
    cqn = _rms(res[:, _OFF_CQ:_OFF_CQ + Q_LORA_RANK], qn_ref[...]).astype(MXU_DTYPE)
    yq_t = _dot_nt(wq2t_ref[...], cqn)
    d_qk = MLA_NOPE_DIM + MLA_ROPE_DIM
    rot_off = MLA_HEADS * d_qk
    q_pad = jnp.zeros((LANES - d_qk, rows), jnp.float32)
    for h in range(MLA_HEADS):
        nope = yq_t[h * d_qk:h * d_qk + MLA_NOPE_DIM]
        rope = (yq_t[h * d_qk + MLA_NOPE_DIM:(h + 1) * d_qk] * cos_rt
                + yq_t[rot_off + h * MLA_ROPE_DIM:rot_off + (h + 1) * MLA_ROPE_DIM] * sin_rt)
        q_mla_ref[0, h] = jnp.concatenate([nope, rope, q_pad], axis=0).astype(q_mla_ref.dtype)
    ckvn = _rms(res[:, _OFF_CKV:_OFF_CKV + KV_LORA_RANK], kvn_ref[...]).astype(MXU_DTYPE)
    yk = _dot(ckvn, wk2_ref[...])
    k_pe = res[:, _OFF_KRA:_OFF_KRA + LANES] * cos_b + res[:, _OFF_KRB:_OFF_KRB + LANES] * sin_b
    for h in range(MLA_HEADS):
        k_mla_ref[0, h] = (yk[:, h * LANES:(h + 1) * LANES] + k_pe).astype(k_mla_ref.dtype)
    yv_t = _dot_nt(wv2t_ref[...], ckvn)
    tk = vt_mla_ref.shape[4]
    for h in range(MLA_HEADS):
        v_t = jnp.concatenate([yv_t[h * MLA_V_DIM:(h + 1) * MLA_V_DIM], ones_rows], axis=0)
        for j in range(rows // tk):
            vt_mla_ref[0, h, j] = v_t[:, j * tk:(j + 1) * tk].astype(vt_mla_ref.dtype)


def _proj_call(x, gain, w_pm, w_fm, tabs, qn, wq2t, kvn, wk2, wv2t):
    B, S, D = x.shape
    rows = min(PROJ_ROWS, S)
    ns = S // rows
    tk = min(ATTN_K_TILE, S)
    tw = min(WIN_V_TILE, S)
    const = lambda b, i: (0, 0)
    tab = lambda t: (pl.BlockSpec((rows, t.shape[1]), lambda b, i: (i, 0)) if t.shape[0] == S
                     else pl.BlockSpec((t.shape[0], rows), lambda b, i: (0, i)))
    hm = lambda n: pl.BlockSpec((1, n, rows, LANES), lambda b, i: (b, 0, i, 0))
    fm = lambda n, f: pl.BlockSpec((1, n, f, rows), lambda b, i: (b, 0, 0, i))
    tiles = lambda n, t: pl.BlockSpec((1, n, rows // t, V_ROWS, t), lambda b, i: (b, 0, i, 0, 0))
    chunks = pl.BlockSpec((1, CMP_STRIDE, rows // CMP_STRIDE, LANES), lambda b, i: (b, 0, i, 0))
    sds = jax.ShapeDtypeStruct
    G = NSA_KV_GROUPS
    return pl.pallas_call(
        _proj_kernel,
        grid=(B, ns),
        in_specs=[
            pl.BlockSpec((1, rows, D), lambda b, i: (b, i, 0)),
            pl.BlockSpec((1, D), const),
            pl.BlockSpec(w_pm.shape, const),
            pl.BlockSpec(w_fm.shape, const),
            *[tab(t) for t in tabs],
            pl.BlockSpec((1, Q_LORA_RANK), const),
            pl.BlockSpec(wq2t.shape, const),
            pl.BlockSpec((1, KV_LORA_RANK), const),
            pl.BlockSpec(wk2.shape, const),
            pl.BlockSpec(wv2t.shape, const),
        ],
        out_specs=[fm(NSA_HEADS, NSA_HEAD_DIM), hm(G), hm(G), chunks, chunks, tiles(G, tk), tiles(G, tw),
                   pl.BlockSpec((1, _GATE_ROWS, rows), lambda b, i: (b, 0, i)),
                   fm(MLA_HEADS, LANES), hm(MLA_HEADS), tiles(MLA_HEADS, tk)],
        out_shape=[
            sds((B, NSA_HEADS, NSA_HEAD_DIM, S), MXU_DTYPE),
            sds((B, G, S, LANES), MXU_DTYPE),
            sds((B, G, S, LANES), MXU_DTYPE),
            sds((B, CMP_STRIDE, S // CMP_STRIDE, LANES), jnp.float32),
            sds((B, CMP_STRIDE, S // CMP_STRIDE, LANES), jnp.float32),
            sds((B, G, S // tk, V_ROWS, tk), MXU_DTYPE),
            sds((B, G, S // tw, V_ROWS, tw), MXU_DTYPE),
            sds((B, _GATE_ROWS, S), jnp.float32),
            sds((B, MLA_HEADS, LANES, S), MXU_DTYPE),
            sds((B, MLA_HEADS, S, LANES), MXU_DTYPE),
            sds((B, MLA_HEADS, S // tk, V_ROWS, tk), MXU_DTYPE),
        ],
        scratch_shapes=[pltpu.VMEM((rows, LANES), jnp.float32)],
        compiler_params=pltpu.CompilerParams(
            dimension_semantics=("parallel", "parallel"), vmem_limit_bytes=VMEM_LIMIT),
        name="proj",
    )(x, gain, w_pm, w_fm, *tabs, qn, wq2t, kvn, wk2, wv2t)


def _compress_kernel(xk_ref, xv_ref, posk_ref, posv_ref, w1k_ref, w1v_ref, w2k_ref, w2vt_ref,
                     kcmp_ref, vtcmp_ref):
    n_chunk = xk_ref.shape[2]

    def hidden(x_ref, pos_ref, w1_ref):
        x = jnp.concatenate([x_ref[0, p] for p in range(CMP_STRIDE)], axis=1)
        h_lo = _dot((x + pos_ref[0:1, :]).astype(MXU_DTYPE), w1_ref[0])
        h_hi = _dot((x + pos_ref[1:2, :]).astype(MXU_DTYPE), w1_ref[1])
        hid = h_lo + pltpu.roll(h_hi, n_chunk - 1, 0)
        return (hid * jax.nn.sigmoid(hid)).astype(MXU_DTYPE)

    k_out = _dot(hidden(xk_ref, posk_ref, w1k_ref), w2k_ref[...])
    v_out_t = _dot_nt(w2vt_ref[...], hidden(xv_ref, posv_ref, w1v_ref))
    for g in range(NSA_KV_GROUPS):
        kcmp_ref[0, g] = k_out[:, g * LANES:(g + 1) * LANES].astype(kcmp_ref.dtype)
        vtcmp_ref[0, g] = v_out_t[g * LANES:(g + 1) * LANES].astype(vtcmp_ref.dtype)


def _compress_call(xk, xv, posk, posv, w1k, w1v, w2k, w2vt):
    B, _, n_chunk, _ = xk.shape
    xs = pl.BlockSpec((1, CMP_STRIDE, n_chunk, LANES), lambda b: (b, 0, 0, 0))
    c2 = lambda a: pl.BlockSpec(a.shape, lambda b: (0,) * a.ndim)
    G = NSA_KV_GROUPS
    return pl.pallas_call(
        _compress_kernel,
        grid=(B,),
        in_specs=[xs, xs, c2(posk), c2(posv), c2(w1k), c2(w1v), c2(w2k), c2(w2vt)],
        out_specs=[pl.BlockSpec((1, G, n_chunk, LANES), lambda b: (b, 0, 0, 0)),
                   pl.BlockSpec((1, G, LANES, n_chunk), lambda b: (b, 0, 0, 0))],
        out_shape=[jax.ShapeDtypeStruct((B, G, n_chunk, LANES), MXU_DTYPE),
                   jax.ShapeDtypeStruct((B, G, LANES, n_chunk), MXU_DTYPE)],
        compiler_params=pltpu.CompilerParams(
            dimension_semantics=("parallel",), vmem_limit_bytes=VMEM_LIMIT),
        name="compress",
    )(xk, xv, posk, posv, w1k, w1v, w2k, w2vt)


def _nsa_kernel(qt_ref, kcmp_ref, vtcmp_ref, ksel_ref, vtsel_ref, kwin_ref, vtwin_ref, gate_ref,
                imp_ref, cbias_ref, wbias_ref, o_ref, *, n_sel, win_keys, sub):
    g = pl.program_id(1)
    tq = qt_ref.shape[3]
    n_sub = tq // sub
    n_chunk = kcmp_ref.shape[2]
    tk = vtsel_ref.shape[4]
    tw = vtwin_ref.shape[4]
    lanes = NSA_REP * sub
    tile0 = pl.program_id(2) * tq
    m_t = imp_ref[...]
    lane_pos = jnp.bitwise_and(lax.broadcasted_iota(jnp.int32, (1, lanes), 1), sub - 1)

    def front(u):
        s0 = tile0 + u * sub
        q_t = jnp.concatenate([qt_ref[0, r, :, u * sub:(u + 1) * sub] for r in range(NSA_REP)], axis=1)
        q_t0 = jnp.concatenate([q_t, jnp.zeros_like(q_t)], axis=0)
        t_row = s0 + lane_pos

        c_off = pl.multiple_of(n_chunk - s0 // CMP_STRIDE, 8)
        s = _dot(kcmp_ref[0, 0], q_t0) + cbias_ref[pl.ds(c_off, n_chunk), :]
        e = jnp.exp(s - jnp.max(s, axis=0, keepdims=True))
        any_valid = t_row >= CMP_BLOCK - 1
        inv = jnp.where(any_valid, 1.0 / jnp.maximum(jnp.sum(e, axis=0, keepdims=True), SOFTMAX_FLOOR), 0.0)
        p_cmp = e * inv
        o_cmp = _dot(vtcmp_ref[0, 0], p_cmp.astype(MXU_DTYPE))[:HALF]

        p_sum = p_cmp[:, 0:sub]
        for r in range(1, NSA_REP):
            p_sum = p_sum + p_cmp[:, r * sub:(r + 1) * sub]
        hi = p_sum.astype(MXU_DTYPE)
        rem = p_sum - hi.astype(jnp.float32)
        mid = rem.astype(MXU_DTYPE)
        lo = (rem - mid.astype(jnp.float32)).astype(MXU_DTYPE)
        imp_t = _dot(m_t, hi) + _dot(m_t, mid) + _dot(m_t, lo)

        blk = lax.broadcasted_iota(jnp.int32, (MAX_SEL_BLOCKS, sub), 0)
        cur = jnp.right_shift(s0 + lax.broadcasted_iota(jnp.int32, (MAX_SEL_BLOCKS, sub), 1), SEL_BLOCK_LOG2)
        valid = blk <= cur
        forced = valid & ((blk == 0) | (blk > cur - N_LOCAL))
        blk_f = blk.astype(jnp.float32)
        bias_t = jnp.where(forced, 0.0, NEG_INF)
        left = jnp.where(forced, TAKEN, jnp.where(valid, imp_t, NOT_CAUSAL))
        for _ in range(n_sel - (N_LOCAL + 1)):
            top = jnp.max(left, axis=0, keepdims=True)
            first = jnp.min(jnp.where(left == top, blk_f, float(MAX_SEL_BLOCKS)), axis=0, keepdims=True)
            hit = blk_f == first
            bias_t = jnp.where(hit, 0.0, bias_t)
            left = jnp.where(hit, TAKEN, left)
        bias_t = bias_t.astype(MXU_DTYPE)
        q_aug = jnp.concatenate([q_t, jnp.concatenate([bias_t] * NSA_REP, axis=1)], axis=0)

        w_start = pl.multiple_of(jnp.maximum(s0 - WINDOW, 0), sub)
        w_off = pl.multiple_of(WINDOW - (s0 - w_start), sub)
        s = _dot(kwin_ref[0, 0, pl.ds(w_start, win_keys), :], q_t0) + wbias_ref[pl.ds(w_off, win_keys), :]
        e = jnp.exp((s - jnp.max(s, axis=0, keepdims=True)).astype(MXU_DTYPE))
        w_tile = w_start // tw
        v_t = jnp.concatenate([vtwin_ref[0, 0, w_tile + j] for j in range(win_keys // tw)], axis=1)
        o_win = _flash_finish_t((None, _dot(v_t, e)))
        return q_aug, t_row, o_cmp, o_win

    fronts = [front(u) for u in range(n_sub)]

    split = ATTN_K_SPLIT
    chains = [(u, j) for u in range(n_sub) for j in range(split)]

    def sel_step(it, state, diagonal):
        active = [c for c, (u, j) in enumerate(chains) if not (diagonal and j * tk >= (u + 1) * sub)]
        n_keys = {c: min(tk, (chains[c][0] + 1) * sub - chains[c][1] * tk) if diagonal else tk for c in active}
        s_ts = {}
        for c in active:
            u, j = chains[c]
            start = pl.multiple_of((it * split + j) * tk, tk)
            s_ts[c] = _dot(ksel_ref[0, 0, pl.ds(start, n_keys[c]), :], fronts[u][0])
        if diagonal:
            for c in active:
                u, j = chains[c]
                if j * tk + n_keys[c] > u * sub:
                    key = (it * split + j) * tk + lax.broadcasted_iota(jnp.int32, (n_keys[c], lanes), 0)
                    s_ts[c] = jnp.where(key <= fronts[u][1], s_ts[c], NEG_INF)
        out = list(state)
        for c in active:
            v_t = vtsel_ref[0, 0, it * split + chains[c][1]]
            out[c] = _flash_update_t(state[c], s_ts[c], v_t[:, :n_keys[c]], 1.0)
        return tuple(out)

    n_full = pl.program_id(2)
    carry = lax.fori_loop(0, n_full, lambda it, c: sel_step(it, c, False),
                          tuple(_flash_init_t(lanes) for _ in chains))
    carry = sel_step(n_full, carry, True)

    for u in range(n_sub):
        _, _, o_cmp, o_win = fronts[u]
        o_sel = _flash_finish_t(_flash_merge_t(carry[u * split:(u + 1) * split], 1.0))
        gates = gate_ref[0, :, u * sub:(u + 1) * sub]

        def gate(r, br, gates=gates):
            r0 = r * NSA_BRANCHES + br
            r1 = (NSA_REP + r) * NSA_BRANCHES + br
            return jnp.where(g == 0, gates[r0:r0 + 1], gates[r1:r1 + 1])

        heads = []
        for r in range(NSA_REP):
            sl = slice(r * sub, (r + 1) * sub)
            heads.append(gate(r, 0) * o_cmp[:, sl] + gate(r, 1) * o_sel[:, sl] + gate(r, 2) * o_win[:, sl])
        for pair in range(NSA_REP // 2):
            o_ref[0, u * sub:(u + 1) * sub, pair * LANES:(pair + 1) * LANES] = jnp.concatenate(
                [heads[2 * pair], heads[2 * pair + 1]], axis=0).T


def _mask_tables(n_chunk, sub, win_keys):
    pos = (np.arange(NSA_REP * sub) % sub)[None, :]
    r = np.arange(2 * n_chunk)[:, None]
    cmp_ok = CMP_STRIDE * (r - n_chunk) + CMP_BLOCK - 1 <= pos
    r = np.arange(WINDOW + win_keys)[:, None]
    win_ok = (r <= pos + WINDOW) & (r > pos)
    to_bias = lambda ok: jnp.asarray(np.where(ok, 0.0, NEG_INF), jnp.float32)
    return to_bias(cmp_ok), to_bias(win_ok)


def _nsa_call(qt_nsa, k_cmp, vt_cmp, k_sel, vt_sel, k_win, vt_win, gates_t, imp_mat):
    B, _, _, S = qt_nsa.shape
    n_chunk = k_cmp.shape[2]
    tq = min(NSA_Q_TILE, S)
    sub = min(NSA_SUB_TILE, tq)
    tk, tw = vt_sel.shape[4], vt_win.shape[4]
    assert ATTN_K_SPLIT * tk == tq and tq % sub == 0 and sub == LANES and sub % tw == 0
    n_sel = min(N_SELECT, S // SEL_BLOCK)
    win_keys = min(WINDOW + sub, S)
    cmp_bias, win_bias = _mask_tables(n_chunk, sub, win_keys)
    const = lambda a: pl.BlockSpec(a.shape, lambda b, g, i: (0, 0))
    kv = lambda n: pl.BlockSpec((1, 1, n, LANES), lambda b, g, i: (b, g, 0, 0))
    vt = lambda a: pl.BlockSpec((1, 1) + a.shape[2:], lambda b, g, i: (b, g) + (0,) * (a.ndim - 2))
    return pl.pallas_call(
        functools.partial(_nsa_kernel, n_sel=n_sel, win_keys=win_keys, sub=sub),
        grid=(B, NSA_KV_GROUPS, S // tq),
        in_specs=[
            pl.BlockSpec((1, NSA_REP, NSA_HEAD_DIM, tq), lambda b, g, i: (b, g, 0, i)),
            kv(n_chunk), vt(vt_cmp), kv(S), vt(vt_sel), kv(S), vt(vt_win),
            pl.BlockSpec((1, _GATE_ROWS, tq), lambda b, g, i: (b, 0, i)),
            const(imp_mat), const(cmp_bias), const(win_bias),
        ],
        out_specs=pl.BlockSpec((1, tq, NSA_REP * NSA_HEAD_DIM), lambda b, g, i: (b, i, g)),
        out_shape=jax.ShapeDtypeStruct((B, S, NSA_WIDTH), jnp.float32),
        compiler_params=pltpu.CompilerParams(
            dimension_semantics=("parallel", "parallel", "arbitrary"), vmem_limit_bytes=VMEM_LIMIT),
        name="nsa",
    )(qt_nsa, k_cmp, vt_cmp, k_sel, vt_sel, k_win, vt_win, gates_t, imp_mat, cmp_bias, win_bias)


def _mla_kernel(qt_ref, k_ref, vt_ref, o_ref):
    n_heads = qt_ref.shape[1]
    tq = qt_ref.shape[3]
    tk = vt_ref.shape[4]
    q0 = pl.program_id(2) * tq
    scale = (MLA_NOPE_DIM + MLA_ROPE_DIM) ** -0.5
    t_row = q0 + lax.broadcasted_iota(jnp.int32, (1, tq), 1)
    qts = [qt_ref[0, h] for h in range(n_heads)]
    split = tq // tk
    chains = [(h, j) for h in range(n_heads) for j in range(split)]

    def step(it, state, causal):
        lo = [j * tk if causal else 0 for _, j in chains]

        def scores(c):
            h, j = chains[c]
            start = pl.multiple_of((it * split + j) * tk, tk)
            s_t = _dot(k_ref[0, h, pl.ds(start, tk), :], qts[h][:, lo[c]:])
            if causal:
                key = (it * split + j) * tk + lax.broadcasted_iota(jnp.int32, (tk, tq - lo[c]), 0)
                s_t = jnp.where(key <= t_row[:, lo[c]:], s_t, NEG_INF)
            return s_t

        def update(c, s_t):
            h, j = chains[c]
            m, acc = state[c]
            m_new, acc_new = _flash_update_t((m[:, lo[c]:], acc[:, lo[c]:]), s_t,
                                             vt_ref[0, h, it * split + j], scale)
            if lo[c]:
                m_new = jnp.concatenate([m[:, :lo[c]], m_new], axis=1)
                acc_new = jnp.concatenate([acc[:, :lo[c]], acc_new], axis=1)
            return m_new, acc_new

        s_ts = [scores(c) for c in range(len(chains))]
        return tuple(update(c, s_ts[c]) for c in range(len(chains)))

    n_full = pl.program_id(2)
    carry = lax.fori_loop(0, n_full, lambda it, c: step(it, c, False),
                          tuple(_flash_init_t(tq) for _ in chains))
    carry = step(n_full, carry, True)
    outs = [_flash_finish_t(_flash_merge_t(carry[h * split:(h + 1) * split], scale)) for h in range(n_heads)]
    for p in range(n_heads // 2):
        o_ref[0, :, p * LANES:(p + 1) * LANES] = jnp.concatenate([outs[2 * p], outs[2 * p + 1]], axis=0).T


def _mla_call(qt_mla, k_mla, vt_mla):
    B, _, _, S = qt_mla.shape
    tq = min(MLA_TILE, S)
    tk = vt_mla.shape[4]
    nh = MLA_HEADS_PER_STEP
    assert tq % tk == 0 and MLA_HEADS % nh == 0 and nh % 2 == 0
    return pl.pallas_call(
        _mla_kernel,
        grid=(B, MLA_HEADS // nh, S // tq),
        in_specs=[
            pl.BlockSpec((1, nh, LANES, tq), lambda b, p, i: (b, p, 0, i)),
            pl.BlockSpec((1, nh, S, LANES), lambda b, p, i: (b, p, 0, 0)),
            pl.BlockSpec((1, nh, S // tk, V_ROWS, tk), lambda b, p, i: (b, p, 0, 0, 0)),
        ],
        out_specs=pl.BlockSpec((1, tq, nh * MLA_V_DIM), lambda b, p, i: (b, i, p)),
        out_shape=jax.ShapeDtypeStruct((B, S, MLA_WIDTH), jnp.float32),
        compiler_params=pltpu.CompilerParams(
            dimension_semantics=("parallel", "parallel", "arbitrary"), vmem_limit_bytes=VMEM_LIMIT),
        name="mla",
    )(qt_mla, k_mla, vt_mla)


def _mix_mlp_kernel(x_ref, oa_ref, ob_ref, na_ref, nb_ref, woa_ref, wob_ref, mn_ref, w1_ref, w2_ref,
                    fn_ref, o_ref, x1_ref, hn_ref, acc_ref, *, final):
    f = pl.program_id(1)

    @pl.when(f == 0)
    def _():
        na = _rms(oa_ref[...], na_ref[...]).astype(MXU_DTYPE)
        nb = _rms(ob_ref[...], nb_ref[...]).astype(MXU_DTYPE)
        x1 = x_ref[...] + (_dot(na, woa_ref[...]) + _dot(nb, wob_ref[...]))
        x1_ref[...] = x1
        hn_ref[...] = _rms(x1, mn_ref[...]).astype(MXU_DTYPE)
        acc_ref[...] = jnp.zeros_like(acc_ref)

    a = jnp.square(jnp.maximum(_dot(hn_ref[...], w1_ref[...]), 0.0))
    acc_ref[...] += _dot(a.astype(MXU_DTYPE), w2_ref[...])

    @pl.when(f == pl.num_programs(1) - 1)
    def _():
        y = x1_ref[...] + acc_ref[...]
        o_ref[...] = _rms(y, fn_ref[...]) if final else y


def _mix_mlp_call(x, o_a, o_b, na, nb, wo_a, wo_b, mn, w1, w2, fn, final):
    T, D = x.shape
    rows = min(MLP_ROWS, T)
    tf = MLP_FF_TILE
    row = lambda w: pl.BlockSpec((rows, w), lambda i, f: (i, 0))
    const = lambda a: pl.BlockSpec(a.shape, lambda i, f: (0, 0))
    return pl.pallas_call(
        functools.partial(_mix_mlp_kernel, final=final),
        grid=(T // rows, D_FF // tf),
        in_specs=[row(D), row(NSA_WIDTH), row(MLA_WIDTH), const(na), const(nb), const(wo_a), const(wo_b),
                  const(mn),
                  pl.BlockSpec((D, tf), lambda i, f: (0, f)),
                  pl.BlockSpec((tf, D), lambda i, f: (f, 0)),
                  const(fn)],
        out_specs=row(D),
        out_shape=jax.ShapeDtypeStruct((T, D), jnp.float32),
        scratch_shapes=[pltpu.VMEM((rows, D), jnp.float32), pltpu.VMEM((rows, D), MXU_DTYPE),
                        pltpu.VMEM((rows, D), jnp.float32)],
        compiler_params=pltpu.CompilerParams(
            dimension_semantics=("parallel", "arbitrary"), vmem_limit_bytes=VMEM_LIMIT),
        name="mix_mlp",
    )(x, o_a, o_b, na, nb, wo_a, wo_b, mn, w1, w2, fn)


def _rot_cols(w, dim):
    lead = w.shape[:-1]
    w4 = w.reshape(lead + (w.shape[-1] // dim, 2, dim // 2))
    return jnp.concatenate([-w4[..., 1, :], w4[..., 0, :]], axis=-1).reshape(w.shape)


def _pad_cols(w, left, total):
    pad = [(0, 0)] * (w.ndim - 1) + [(left, total - left - w.shape[-1])]
    return jnp.pad(w, pad)


def _pack_in_proj(w_in):
    sizes = (NSA_WIDTH,) + (NSA_KV_GROUPS * NSA_HEAD_DIM,) * 6 + (
        NSA_BRANCHES * NSA_HEADS, Q_LORA_RANK, KV_LORA_RANK, MLA_ROPE_DIM)
    offs = np.cumsum((0,) + sizes)
    q_a, k_c, v_c, k_s, v_s, k_w, v_w, g_a, c_q, c_kv, k_r = (
        w_in[..., offs[i]:offs[i + 1]] for i in range(len(sizes)))
    roped = jnp.concatenate([k_c, k_s, k_w], axis=-1)
    w_pm = jnp.concatenate(
        [roped, _rot_cols(roped, NSA_HEAD_DIM), v_c, c_q, c_kv,
         _pad_cols(k_r, HALF, LANES), _pad_cols(_rot_cols(k_r, MLA_ROPE_DIM), HALF, LANES)], axis=-1)
    w_fm = jnp.concatenate([q_a, _rot_cols(q_a, NSA_HEAD_DIM), v_s, v_w, _pad_cols(g_a, 0, _GATE_ROWS)], axis=-1)
    assert w_pm.shape[-1] == _PROJ_COLS and w_fm.shape[-1] == _FT_ROWS
    return w_pm.astype(MXU_DTYPE), jnp.swapaxes(w_fm, 1, 2).astype(MXU_DTYPE)


def _pack_q_up(w_q_up):
    L, R, _ = w_q_up.shape
    w = w_q_up.reshape(L, R, MLA_HEADS, MLA_NOPE_DIM + MLA_ROPE_DIM)
    rot = _rot_cols(w[..., MLA_NOPE_DIM:], MLA_ROPE_DIM)
    packed = jnp.concatenate([w.reshape(L, R, -1), rot.reshape(L, R, -1)], axis=-1)
    return jnp.swapaxes(packed, 1, 2).astype(MXU_DTYPE)


def _pack_kv_up(w_kv_up):
    L, R, _ = w_kv_up.shape
    w = w_kv_up.reshape(L, R, MLA_HEADS, MLA_NOPE_DIM + MLA_V_DIM)
    k = _pad_cols(w[..., :MLA_NOPE_DIM], 0, LANES).reshape(L, R, -1)
    v_t = jnp.swapaxes(w[..., MLA_NOPE_DIM:].reshape(L, R, -1), 1, 2)
    return k.astype(MXU_DTYPE), v_t.astype(MXU_DTYPE)


def _pack_compress(pos, w1, w2):
    L = pos.shape[0]
    G, dk, H = NSA_KV_GROUPS, NSA_HEAD_DIM, CMP_HIDDEN
    pos2 = jnp.concatenate([pos] * G, axis=-1).reshape(L, 2, CMP_STRIDE * G * dk)
    on_diag = lambda w, g, axis: jnp.stack([w if g2 == g else jnp.zeros_like(w) for g2 in range(G)], axis=axis)
    w1r = w1.astype(MXU_DTYPE).reshape(L, 2, CMP_STRIDE, dk, H)
    w1b = jnp.stack([on_diag(w1r, g, 4) for g in range(G)], axis=3)
    w1b = w1b.reshape(L, 2, CMP_STRIDE * G * dk, G * H)
    w2p = _pad_cols(w2.astype(MXU_DTYPE), 0, LANES)
    w2b = jnp.stack([on_diag(w2p, g, 2) for g in range(G)], axis=1)
    return pos2, w1b, w2b.reshape(L, G * H, G * LANES)


def _importance_matrix(n_chunk):
    a, b = SEL_BLOCK // CMP_STRIDE, CMP_BLOCK // CMP_STRIDE
    overlap = np.convolve(np.ones(a), np.ones(b))
    n_cmp = n_chunk - (b - 1)
    m = np.zeros((MAX_SEL_BLOCKS, n_chunk), np.float32)
    for blk in range(n_chunk * CMP_STRIDE // SEL_BLOCK):
        for j, wgt in enumerate(overlap):
            i = a * blk + j - (b - 1)
            if 0 <= i < n_cmp:
                m[blk, i] = wgt
    return jnp.asarray(m, MXU_DTYPE)


def _rope_tables(seq, dim):
    inv_freq = 1.0 / (ROPE_THETA ** (jnp.arange(0, dim, 2, dtype=jnp.float32) / dim))
    ang = jnp.arange(seq, dtype=jnp.float32)[:, None] * inv_freq[None, :]
    ang = jnp.concatenate([ang, ang], axis=-1)
    return jnp.cos(ang), jnp.sin(ang)


def kernel(x, attn_norm, w_in, cmp_pos_k, cmp_w1_k, cmp_w2_k, cmp_pos_v, cmp_w1_v, cmp_w2_v, mla_q_norm,
           w_q_up, mla_kv_norm, w_kv_up, nsa_out_norm, mla_out_norm, w_out, mlp_norm, w_ff1, w_ff2, final_norm):
    B, S, D = x.shape
    depth = w_in.shape[0]
    assert S % SEL_BLOCK == 0 and S // SEL_BLOCK <= MAX_SEL_BLOCKS and S % CMP_STRIDE == 0
    n_chunk = S // CMP_STRIDE

    cos64, sin64 = _rope_tables(S, NSA_HEAD_DIM)
    cos_a, sin_a = jnp.concatenate([cos64, cos64], -1), jnp.concatenate([sin64, sin64], -1)
    cos32, sin32 = _rope_tables(S, MLA_ROPE_DIM)
    pad = jnp.zeros((S, LANES - MLA_NOPE_DIM - MLA_ROPE_DIM), jnp.float32)
    cos_b = jnp.concatenate([jnp.ones((S, MLA_NOPE_DIM), jnp.float32), cos32, pad], -1)
    sin_b = jnp.concatenate([jnp.zeros((S, MLA_NOPE_DIM), jnp.float32), sin32, pad], -1)
    tabs = (cos_a, sin_a, cos64.T, sin64.T, cos_b, sin_b, cos32.T, sin32.T)

    w_pm, w_fm = _pack_in_proj(w_in)
    wq2t = _pack_q_up(w_q_up)
    wk2, wv2t = _pack_kv_up(w_kv_up)
    posk, w1k, w2k = _pack_compress(cmp_pos_k, cmp_w1_k, cmp_w2_k)
    posv, w1v, w2v = _pack_compress(cmp_pos_v, cmp_w1_v, cmp_w2_v)
    w2vt = jnp.swapaxes(w2v, 1, 2)
    imp_mat = _importance_matrix(n_chunk)
    wo = w_out.astype(MXU_DTYPE)
    w1 = w_ff1.astype(MXU_DTYPE)
    w2 = w_ff2.astype(MXU_DTYPE)
    row = lambda v: v.reshape(1, -1)

    for l in range(depth):
        (qt_nsa, k_sel, k_win, k_c, v_c, vt_sel, vt_win, gates_t, qt_mla, k_mla, vt_mla) = _proj_call(
            x, row(attn_norm[l]), w_pm[l], w_fm[l], tabs,
            row(mla_q_norm[l]), wq2t[l], row(mla_kv_norm[l]), wk2[l], wv2t[l])
        k_cmp, vt_cmp = _compress_call(
            k_c, v_c, posk[l], posv[l], w1k[l], w1v[l], w2k[l], w2vt[l])
        o_a = _nsa_call(qt_nsa, k_cmp, vt_cmp, k_sel, vt_sel, k_win, vt_win, gates_t, imp_mat)
        o_b = _mla_call(qt_mla, k_mla, vt_mla)
        x = _mix_mlp_call(
            x.reshape(B * S, D), o_a.reshape(B * S, NSA_WIDTH), o_b.reshape(B * S, MLA_WIDTH),
            row(nsa_out_norm[l]), row(mla_out_norm[l]), wo[l, :NSA_WIDTH], wo[l, NSA_WIDTH:],
            row(mlp_norm[l]), w1[l], w2[l], row(final_norm), final=(l == depth - 1)).reshape(B, S, D)
    return x
```

```python
import functools

import numpy as np
import jax
import jax.numpy as jnp
from jax import lax
from jax.experimental import pallas as pl
from jax.experimental.pallas import tpu as pltpu

D_MODEL = 1024
NSA_HEADS = 8
NSA_KV_GROUPS = 2
NSA_REP = NSA_HEADS // NSA_KV_GROUPS
NSA_HEAD_DIM = 64
NSA_BRANCHES = 3
CMP_BLOCK = 32
CMP_STRIDE = 16
CMP_HIDDEN = 256
SEL_BLOCK = 64
SEL_BLOCK_LOG2 = 6
N_SELECT = 16
N_LOCAL = 2
WINDOW = 512
MLA_HEADS = 8
MLA_NOPE_DIM = 64
MLA_ROPE_DIM = 32
MLA_V_DIM = 64
Q_LORA_RANK = 256
KV_LORA_RANK = 128
D_FF = 4 * D_MODEL
ROPE_THETA = 10000.0
NORM_EPS = 1e-6
NEG_INF = -1e30
SOFTMAX_FLOOR = 1e-30
NOT_CAUSAL = -1.0
TAKEN = -2.0
LOG2_E = 1.4426950408889634
NSA_WIDTH = NSA_HEADS * NSA_HEAD_DIM
MLA_WIDTH = MLA_HEADS * MLA_V_DIM

LANES = 128
HALF = LANES // 2
BF16_SUBLANES = 16
V_ROWS = HALF + BF16_SUBLANES
MAX_SEL_BLOCKS = HALF
MXU_DTYPE = jnp.bfloat16
VMEM_LIMIT = 52 * 1024 * 1024
PROJ_ROWS = 512
ATTN_K_TILE = 256
ATTN_K_SPLIT = 2
NSA_Q_TILE = ATTN_K_TILE * ATTN_K_SPLIT
NSA_GROUPS_PER_STEP = 2
NSA_SUB_TILE = 128
WIN_V_TILE = 128
MLA_TILE = ATTN_K_TILE * ATTN_K_SPLIT
MLA_HEADS_PER_STEP = 8
MLP_ROWS = 512
MLP_FF_TILE = 2048

_N_ROPE_BLK = 3
_OFF_ROT = _N_ROPE_BLK * LANES
_OFF_VC = 2 * _OFF_ROT
_OFF_CQ = _OFF_VC + LANES
_OFF_CKV = _OFF_CQ + Q_LORA_RANK
_OFF_KRA = _OFF_CKV + KV_LORA_RANK
_OFF_KRB = _OFF_KRA + LANES
_PROJ_COLS = _OFF_KRB + LANES
_FT_QROT = NSA_WIDTH
_FT_VSEL = 2 * NSA_WIDTH
_FT_VWIN = _FT_VSEL + NSA_KV_GROUPS * NSA_HEAD_DIM
_FT_GATE = _FT_VWIN + NSA_KV_GROUPS * NSA_HEAD_DIM
_GATE_ROWS = 32
_FT_ROWS = _FT_GATE + _GATE_ROWS


def _dot(a, b):
    return jnp.dot(a, b, preferred_element_type=jnp.float32)


def _dot_nt(a, b):
    return lax.dot_general(a, b, (((1,), (1,)), ((), ())), preferred_element_type=jnp.float32)


def _rms(x, gain):
    return x * lax.rsqrt(jnp.mean(x * x, axis=-1, keepdims=True) + NORM_EPS) * gain


def _ones_row_block(n):
    return jnp.where(lax.broadcasted_iota(jnp.int32, (V_ROWS - HALF, n), 0) == 0, 1.0, 0.0)


def _flash_update_t(carry, s_t, v_t, scale):
    m, acc = carry
    c = scale * LOG2_E
    m_new = jnp.maximum(m, jnp.max(s_t, axis=0, keepdims=True))
    alpha = jnp.exp2((m - m_new) * c)
    p_t = jnp.exp2((s_t - m_new).astype(MXU_DTYPE) * c)
    return m_new, alpha * acc + _dot(v_t, p_t)


def _flash_init_t(n):
    return jnp.full((1, n), NEG_INF, jnp.float32), jnp.zeros((V_ROWS, n), jnp.float32)


def _flash_merge_t(states, scale):
    c = scale * LOG2_E
    m = states[0][0]
    for st in states[1:]:
        m = jnp.maximum(m, st[0])
    acc = jnp.exp2((states[0][0] - m) * c) * states[0][1]
    for st in states[1:]:
        acc = acc + jnp.exp2((st[0] - m) * c) * st[1]
    return m, acc


def _flash_finish_t(carry):
    acc = carry[1]
    return acc[:HALF] / jnp.maximum(acc[HALF:HALF + 1], SOFTMAX_FLOOR)


def _proj_kernel(x_ref, gain_ref, w_ref, wt_ref, cos_a_ref, sin_a_ref, cos_at_ref, sin_at_ref,
                 cos_b_ref, sin_b_ref, cos_rt_ref, sin_rt_ref, qn_ref, wq2t_ref, kvn_ref, wk2_ref, wv2t_ref,
                 q_nsa_ref, k_sel_ref, k_win_ref, kc_ref, vc_ref, vt_sel_ref, vt_win_ref, gate_ref,
                 q_mla_ref, k_mla_ref, vt_mla_ref, chunk_ref):
    rows = x_ref.shape[1]
    hn = _rms(x_ref[0], gain_ref[...]).astype(MXU_DTYPE)
    res = _dot(hn, w_ref[...])
    feat_t = _dot_nt(wt_ref[...], hn)
    ones_rows = _ones_row_block(rows)

    cos_at, sin_at = cos_at_ref[...], sin_at_ref[...]
    scale = NSA_HEAD_DIM ** -0.5
    for h in range(NSA_HEADS):
        lo = h * NSA_HEAD_DIM
        q_t = feat_t[lo:lo + NSA_HEAD_DIM] * cos_at + feat_t[_FT_QROT + lo:_FT_QROT + lo + NSA_HEAD_DIM] * sin_at
        q_nsa_ref[0, h] = (q_t * scale).astype(q_nsa_ref.dtype)
    for off, out_ref in ((_FT_VSEL, vt_sel_ref), (_FT_VWIN, vt_win_ref)):
        tile = out_ref.shape[4]
        for g in range(NSA_KV_GROUPS):
            lo = off + g * NSA_HEAD_DIM
            v_t = jnp.concatenate([feat_t[lo:lo + NSA_HEAD_DIM], ones_rows], axis=0)
            for j in range(rows // tile):
                out_ref[0, g, j] = v_t[:, j * tile:(j + 1) * tile].astype(out_ref.dtype)
    gate_ref[0] = jax.nn.sigmoid(feat_t[_FT_GATE:_FT_GATE + _GATE_ROWS])

    cos_a, sin_a = cos_a_ref[...], sin_a_ref[...]
    lane = lax.broadcasted_iota(jnp.int32, (rows, LANES), 1)
    lower = lane < HALF

    def rope_blk(j):
        return (res[:, j * LANES:(j + 1) * LANES] * cos_a
                + res[:, _OFF_ROT + j * LANES:_OFF_ROT + (j + 1) * LANES] * sin_a)

    def split_pair(blk):
        return jnp.where(lower, blk, 0.0), jnp.where(lower, pltpu.roll(blk, HALF, 1), 0.0)

    for out_ref, val in ((kc_ref, rope_blk(0)), (vc_ref, res[:, _OFF_VC:_OFF_VC + LANES])):
        chunk_ref[...] = val
        for p in range(CMP_STRIDE):
            out_ref[0, p] = chunk_ref[pl.ds(p, rows // CMP_STRIDE, stride=CMP_STRIDE), :]
    pos = pl.program_id(1) * rows + lax.broadcasted_iota(jnp.int32, (rows, LANES), 0)
    onehot = jnp.where((lane - HALF) == jnp.right_shift(pos, SEL_BLOCK_LOG2), 1.0, 0.0)
    a, b = split_pair(rope_blk(1))
    k_sel_ref[0, 0] = (a + onehot).astype(k_sel_ref.dtype)
    k_sel_ref[0, 1] = (b + onehot).astype(k_sel_ref.dtype)
    a, b = split_pair(rope_blk(2))
    k_win_ref[0, 0] = a.astype(k_win_ref.dtype)
    k_win_ref[0, 1] = b.astype(k_win_ref.dtype)

    cos_b, sin_b = cos_b_ref[...], sin_b_ref[...]
    cos_rt, sin_rt = cos_rt_ref[...], sin_rt_ref[...]
    cqn = _rms(res[:, _OFF_CQ:_OFF_CQ + Q_LORA_RANK], qn_ref[...]).astype(MXU_DTYPE)
    yq_t = _dot_nt(wq2t_ref[...], cqn)
    d_qk = MLA_NOPE_DIM + MLA_ROPE_DIM
    rot_off = MLA_HEADS * d_qk
    q_pad = jnp.zeros((LANES - d_qk, rows), jnp.float32)
    for h in range(MLA_HEADS):
        nope = yq_t[h * d_qk:h * d_qk + MLA_NOPE_DIM]
        rope = (yq_t[h * d_qk + MLA_NOPE_DIM:(h + 1) * d_qk] * cos_rt
                + yq_t[rot_off + h * MLA_ROPE_DIM:rot_off + (h + 1) * MLA_ROPE_DIM] * sin_rt)
        q_mla_ref[0, h] = jnp.concatenate([nope, rope, q_pad], axis=0).astype(q_mla_ref.dtype)
    ckvn = _rms(res[:, _OFF_CKV:_OFF_CKV + KV_LORA_RANK], kvn_ref[...]).astype(MXU_DTYPE)
    yk = _dot(ckvn, wk2_ref[...])
    k_pe = res[:, _OFF_KRA:_OFF_KRA + LANES] * cos_b + res[:, _OFF_KRB:_OFF_KRB + LANES] * sin_b
    for h in range(MLA_HEADS):
        k_mla_ref[0, h] = (yk[:, h * LANES:(h + 1) * LANES] + k_pe).astype(k_mla_ref.dtype)
    yv_t = _dot_nt(wv2t_ref[...], ckvn)
    tk = vt_mla_ref.shape[4]
    for h in range(MLA_HEADS):
        v_t = jnp.concatenate([yv_t[h * MLA_V_DIM:(h + 1) * MLA_V_DIM], ones_rows], axis=0)
        for j in range(rows // tk):
            vt_mla_ref[0, h, j] = v_t[:, j * tk:(j + 1) * tk].astype(vt_mla_ref.dtype)


def _proj_call(x, gain, w_pm, w_fm, tabs, qn, wq2t, kvn, wk2, wv2t):
    B, S, D = x.shape
    rows = min(PROJ_ROWS, S)
    ns = S // rows
    tk = min(ATTN_K_TILE, S)
    tw = min(WIN_V_TILE, S)
    const = lambda b, i: (0, 0)
    tab = lambda t: (pl.BlockSpec((rows, t.shape[1]), lambda b, i: (i, 0)) if t.shape[0] == S
                     else pl.BlockSpec((t.shape[0], rows), lambda b, i: (0, i)))
    hm = lambda n: pl.BlockSpec((1, n, rows, LANES), lambda b, i: (b, 0, i, 0))
    fm = lambda n, f: pl.BlockSpec((1, n, f, rows), lambda b, i: (b, 0, 0, i))
    tiles = lambda n, t: pl.BlockSpec((1, n, rows // t, V_ROWS, t), lambda b, i: (b, 0, i, 0, 0))
    chunks = pl.BlockSpec((1, CMP_STRIDE, rows // CMP_STRIDE, LANES), lambda b, i: (b, 0, i, 0))
    sds = jax.ShapeDtypeStruct
    G = NSA_KV_GROUPS
    return pl.pallas_call(
        _proj_kernel,
        grid=(B, ns),
        in_specs=[
            pl.BlockSpec((1, rows, D), lambda b, i: (b, i, 0)),
            pl.BlockSpec((1, D), const),
            pl.BlockSpec(w_pm.shape, const),
            pl.BlockSpec(w_fm.shape, const),
            *[tab(t) for t in tabs],
            pl.BlockSpec((1, Q_LORA_RANK), const),
            pl.BlockSpec(wq2t.shape, const),
            pl.BlockSpec((1, KV_LORA_RANK), const),
            pl.BlockSpec(wk2.shape, const),
            pl.BlockSpec(wv2t.shape, const),
        ],
        out_specs=[fm(NSA_HEADS, NSA_HEAD_DIM), hm(G), hm(G), chunks, chunks, tiles(G, tk), tiles(G, tw),
                   pl.BlockSpec((1, _GATE_ROWS, rows), lambda b, i: (b, 0, i)),
                   fm(MLA_HEADS, LANES), hm(MLA_HEADS), tiles(MLA_HEADS, tk)],
        out_shape=[
            sds((B, NSA_HEADS, NSA_HEAD_DIM, S), MXU_DTYPE),
            sds((B, G, S, LANES), MXU_DTYPE),
            sds((B, G, S, LANES), MXU_DTYPE),
            sds((B, CMP_STRIDE, S // CMP_STRIDE, LANES), jnp.float32),
            sds((B, CMP_STRIDE, S // CMP_STRIDE, LANES), jnp.float32),
            sds((B, G, S // tk, V_ROWS, tk), MXU_DTYPE),
            sds((B, G, S // tw, V_ROWS, tw), MXU_DTYPE),
            sds((B, _GATE_ROWS, S), jnp.float32),
            sds((B, MLA_HEADS, LANES, S), MXU_DTYPE),
            sds((B, MLA_HEADS, S, LANES), MXU_DTYPE),
            sds((B, MLA_HEADS, S // tk, V_ROWS, tk), MXU_DTYPE),
        ],
        scratch_shapes=[pltpu.VMEM((rows, LANES), jnp.float32)],
        compiler_params=pltpu.CompilerParams(
            dimension_semantics=("parallel", "parallel"), vmem_limit_bytes=VMEM_LIMIT),
        name="proj",
    )(x, gain, w_pm, w_fm, *tabs, qn, wq2t, kvn, wk2, wv2t)


def _compress_kernel(xk_ref, xv_ref, posk_ref, posv_ref, w1k_ref, w1v_ref, w2k_ref, w2vt_ref,
                     kcmp_ref, vtcmp_ref):
    n_chunk = xk_ref.shape[2]

    def hidden(x_ref, pos_ref, w1_ref):
        x = jnp.concatenate([x_ref[0, p] for p in range(CMP_STRIDE)], axis=1)
        h_lo = _dot((x + pos_ref[0:1, :]).astype(MXU_DTYPE), w1_ref[0])
        h_hi = _dot((x + pos_ref[1:2, :]).astype(MXU_DTYPE), w1_ref[1])
        hid = h_lo + pltpu.roll(h_hi, n_chunk - 1, 0)
        return (hid * jax.nn.sigmoid(hid)).astype(MXU_DTYPE)

    k_out = _dot(hidden(xk_ref, posk_ref, w1k_ref), w2k_ref[...])
    v_out_t = _dot_nt(w2vt_ref[...], hidden(xv_ref, posv_ref, w1v_ref))
    for g in range(NSA_KV_GROUPS):
        kcmp_ref[0, g] = k_out[:, g * LANES:(g + 1) * LANES].astype(kcmp_ref.dtype)
        vtcmp_ref[0, g] = v_out_t[g * LANES:(g + 1) * LANES].astype(vtcmp_ref.dtype)


def _compress_call(xk, xv, posk, posv, w1k, w1v, w2k, w2vt):
    B, _, n_chunk, _ = xk.shape
    xs = pl.BlockSpec((1, CMP_STRIDE, n_chunk, LANES), lambda b: (b, 0, 0, 0))
    c2 = lambda a: pl.BlockSpec(a.shape, lambda b: (0,) * a.ndim)
    G = NSA_KV_GROUPS
    return pl.pallas_call(
        _compress_kernel,
        grid=(B,),
        in_specs=[xs, xs, c2(posk), c2(posv), c2(w1k), c2(w1v), c2(w2k), c2(w2vt)],
        out_specs=[pl.BlockSpec((1, G, n_chunk, LANES), lambda b: (b, 0, 0, 0)),
                   pl.BlockSpec((1, G, LANES, n_chunk), lambda b: (b, 0, 0, 0))],
        out_shape=[jax.ShapeDtypeStruct((B, G, n_chunk, LANES), MXU_DTYPE),
                   jax.ShapeDtypeStruct((B, G, LANES, n_chunk), MXU_DTYPE)],
        compiler_params=pltpu.CompilerParams(
            dimension_semantics=("parallel",), vmem_limit_bytes=VMEM_LIMIT),
        name="compress",
    )(xk, xv, posk, posv, w1k, w1v, w2k, w2vt)


def _nsa_kernel(qt_ref, kcmp_ref, vtcmp_ref, ksel_ref, vtsel_ref, kwin_ref, vtwin_ref, gate_ref,
                imp_ref, cbias_ref, wbias_ref, o_ref, *, n_sel, win_keys, sub):
    n_gb = kcmp_ref.shape[1]
    tq = qt_ref.shape[3]
    n_sub = tq // sub
    n_chunk = kcmp_ref.shape[2]
    tk = vtsel_ref.shape[4]
    tw = vtwin_ref.shape[4]
    lanes = NSA_REP * sub
    tile0 = pl.program_id(2) * tq
    m_t = imp_ref[...]
    lane_pos = jnp.bitwise_and(lax.broadcasted_iota(jnp.int32, (1, lanes), 1), sub - 1)

    def front(gi, u):
        s0 = tile0 + u * sub
        q_t = jnp.concatenate([qt_ref[0, gi * NSA_REP + r, :, u * sub:(u + 1) * sub] for r in range(NSA_REP)],
                              axis=1)
        q_t0 = jnp.concatenate([q_t, jnp.zeros_like(q_t)], axis=0)
        t_row = s0 + lane_pos

        c_off = pl.multiple_of(n_chunk - s0 // CMP_STRIDE, 8)
        s = _dot(kcmp_ref[0, gi], q_t0) + cbias_ref[pl.ds(c_off, n_chunk), :]
        e = jnp.exp(s - jnp.max(s, axis=0, keepdims=True))
        any_valid = t_row >= CMP_BLOCK - 1
        inv = jnp.where(any_valid, 1.0 / jnp.maximum(jnp.sum(e, axis=0, keepdims=True), SOFTMAX_FLOOR), 0.0)
        p_cmp = e * inv
        o_cmp = _dot(vtcmp_ref[0, gi], p_cmp.astype(MXU_DTYPE))[:HALF]

        p_sum = p_cmp[:, 0:sub]
        for r in range(1, NSA_REP):
            p_sum = p_sum + p_cmp[:, r * sub:(r + 1) * sub]
        hi = p_sum.astype(MXU_DTYPE)
        rem = p_sum - hi.astype(jnp.float32)
        mid = rem.astype(MXU_DTYPE)
        lo = (rem - mid.astype(jnp.float32)).astype(MXU_DTYPE)
        imp_t = _dot(m_t, hi) + _dot(m_t, mid) + _dot(m_t, lo)

        blk = lax.broadcasted_iota(jnp.int32, (MAX_SEL_BLOCKS, sub), 0)
        cur = jnp.right_shift(s0 + lax.broadcasted_iota(jnp.int32, (MAX_SEL_BLOCKS, sub), 1), SEL_BLOCK_LOG2)
        valid = blk <= cur
        forced = valid & ((blk == 0) | (blk > cur - N_LOCAL))
        blk_f = blk.astype(jnp.float32)
        bias_t = jnp.where(forced, 0.0, NEG_INF)
        left = jnp.where(forced, TAKEN, jnp.where(valid, imp_t, NOT_CAUSAL))
        for _ in range(n_sel - (N_LOCAL + 1)):
            top = jnp.max(left, axis=0, keepdims=True)
            first = jnp.min(jnp.where(left == top, blk_f, float(MAX_SEL_BLOCKS)), axis=0, keepdims=True)
            hit = blk_f == first
            bias_t = jnp.where(hit, 0.0, bias_t)
            left = jnp.where(hit, TAKEN, left)
        bias_t = bias_t.astype(MXU_DTYPE)
        q_aug = jnp.concatenate([q_t, jnp.concatenate([bias_t] * NSA_REP, axis=1)], axis=0)

        w_start = pl.multiple_of(jnp.maximum(s0 - WINDOW, 0), sub)
        w_off = pl.multiple_of(WINDOW - (s0 - w_start), sub)
        s = _dot(kwin_ref[0, gi, pl.ds(w_start, win_keys), :], q_t0) + wbias_ref[pl.ds(w_off, win_keys), :]
        e = jnp.exp((s - jnp.max(s, axis=0, keepdims=True)).astype(MXU_DTYPE))
        w_tile = w_start // tw
        v_t = jnp.concatenate([vtwin_ref[0, gi, w_tile + j] for j in range(win_keys // tw)], axis=1)
        o_win = _flash_finish_t((None, _dot(v_t, e)))
        return q_aug, t_row, o_cmp, o_win

    units = [(gi, u) for gi in range(n_gb) for u in range(n_sub)]
    fronts = [front(gi, u) for gi, u in units]

    split = ATTN_K_SPLIT
    chains = [(x, j) for x in range(len(units)) for j in range(split)]

    def sel_step(it, state, diagonal):
        sub_of = lambda c: units[chains[c][0]][1]
        active = [c for c, (x, j) in enumerate(chains) if not (diagonal and j * tk >= (sub_of(c) + 1) * sub)]
        n_keys = {c: min(tk, (sub_of(c) + 1) * sub - chains[c][1] * tk) if diagonal else tk for c in active}
        s_ts = {}
        for c in active:
            x, j = chains[c]
            start = pl.multiple_of((it * split + j) * tk, tk)
            s_ts[c] = _dot(ksel_ref[0, units[x][0], pl.ds(start, n_keys[c]), :], fronts[x][0])
        if diagonal:
            for c in active:
                x, j = chains[c]
                if j * tk + n_keys[c] > sub_of(c) * sub:
                    key = (it * split + j) * tk + lax.broadcasted_iota(jnp.int32, (n_keys[c], lanes), 0)
                    s_ts[c] = jnp.where(key <= fronts[x][1], s_ts[c], NEG_INF)
        out = list(state)
        for c in active:
            x, j = chains[c]
            v_t = vtsel_ref[0, units[x][0], it * split + j]
            out[c] = _flash_update_t(state[c], s_ts[c], v_t[:, :n_keys[c]], 1.0)
        return tuple(out)

    n_full = pl.program_id(2)
    carry = lax.fori_loop(0, n_full, lambda it, c: sel_step(it, c, False),
                          tuple(_flash_init_t(lanes) for _ in chains))
    carry = sel_step(n_full, carry, True)

    for x, (gi, u) in enumerate(units):
        _, _, o_cmp, o_win = fronts[x]
        o_sel = _flash_finish_t(_flash_merge_t(carry[x * split:(x + 1) * split], 1.0))
        gates = gate_ref[0, :, u * sub:(u + 1) * sub]

        def gate(r, br, gates=gates, gi=gi):
            rows = [((grp * NSA_REP + r) * NSA_BRANCHES + br) for grp in range(NSA_KV_GROUPS)]
            if n_gb == NSA_KV_GROUPS:
                return gates[rows[gi]:rows[gi] + 1]
            return jnp.where(pl.program_id(1) == 0, gates[rows[0]:rows[0] + 1], gates[rows[1]:rows[1] + 1])

        heads = []
        for r in range(NSA_REP):
            sl = slice(r * sub, (r + 1) * sub)
            heads.append(gate(r, 0) * o_cmp[:, sl] + gate(r, 1) * o_sel[:, sl] + gate(r, 2) * o_win[:, sl])
        for pair in range(NSA_REP // 2):
            col = (gi * (NSA_REP // 2) + pair) * LANES
            o_ref[0, u * sub:(u + 1) * sub, col:col + LANES] = jnp.concatenate(
                [heads[2 * pair], heads[2 * pair + 1]], axis=0).T


def _mask_tables(n_chunk, sub, win_keys):
    pos = (np.arange(NSA_REP * sub) % sub)[None, :]
    r = np.arange(2 * n_chunk)[:, None]
    cmp_ok = CMP_STRIDE * (r - n_chunk) + CMP_BLOCK - 1 <= pos
    r = np.arange(WINDOW + win_keys)[:, None]
    win_ok = (r <= pos + WINDOW) & (r > pos)
    to_bias = lambda ok: jnp.asarray(np.where(ok, 0.0, NEG_INF), jnp.float32)
    return to_bias(cmp_ok), to_bias(win_ok)


def _nsa_call(qt_nsa, k_cmp, vt_cmp, k_sel, vt_sel, k_win, vt_win, gates_t, imp_mat):
    B, _, _, S = qt_nsa.shape
    n_chunk = k_cmp.shape[2]
    tq = min(NSA_Q_TILE, S)
    sub = min(NSA_SUB_TILE, tq)
    tk, tw = vt_sel.shape[4], vt_win.shape[4]
    assert ATTN_K_SPLIT * tk == tq and tq % sub == 0 and sub == LANES and sub % tw == 0
    n_sel = min(N_SELECT, S // SEL_BLOCK)
    win_keys = min(WINDOW + sub, S)
    cmp_bias, win_bias = _mask_tables(n_chunk, sub, win_keys)
    gb = NSA_GROUPS_PER_STEP
    const = lambda a: pl.BlockSpec(a.shape, lambda b, g, i: (0, 0))
    kv = lambda n: pl.BlockSpec((1, gb, n, LANES), lambda b, g, i: (b, g, 0, 0))
    vt = lambda a: pl.BlockSpec((1, gb) + a.shape[2:], lambda b, g, i: (b, g) + (0,) * (a.ndim - 2))
    return pl.pallas_call(
        functools.partial(_nsa_kernel, n_sel=n_sel, win_keys=win_keys, sub=sub),
        grid=(B, NSA_KV_GROUPS // gb, S // tq),
        in_specs=[
            pl.BlockSpec((1, gb * NSA_REP, NSA_HEAD_DIM, tq), lambda b, g, i: (b, g, 0, i)),
            kv(n_chunk), vt(vt_cmp), kv(S), vt(vt_sel), kv(S), vt(vt_win),
            pl.BlockSpec((1, _GATE_ROWS, tq), lambda b, g, i: (b, 0, i)),
            const(imp_mat), const(cmp_bias), const(win_bias),
        ],
        out_specs=pl.BlockSpec((1, tq, gb * NSA_REP * NSA_HEAD_DIM), lambda b, g, i: (b, i, g)),
        out_shape=jax.ShapeDtypeStruct((B, S, NSA_WIDTH), jnp.float32),
        compiler_params=pltpu.CompilerParams(
            dimension_semantics=("parallel", "parallel", "arbitrary"), vmem_limit_bytes=VMEM_LIMIT),
        name="nsa",
    )(qt_nsa, k_cmp, vt_cmp, k_sel, vt_sel, k_win, vt_win, gates_t, imp_mat, cmp_bias, win_bias)


def _mla_kernel(qt_ref, k_ref, vt_ref, o_ref):
    n_heads = qt_ref.shape[1]
    tq = qt_ref.shape[3]
    tk = vt_ref.shape[4]
    q0 = pl.program_id(2) * tq
    scale = (MLA_NOPE_DIM + MLA_ROPE_DIM) ** -0.5
    t_row = q0 + lax.broadcasted_iota(jnp.int32, (1, tq), 1)
    qts = [qt_ref[0, h] for h in range(n_heads)]
    split = tq // tk
    chains = [(h, j) for h in range(n_heads) for j in range(split)]

    def step(it, state, causal):
        lo = [j * tk if causal else 0 for _, j in chains]

        def scores(c):
            h, j = chains[c]
            start = pl.multiple_of((it * split + j) * tk, tk)
            s_t = _dot(k_ref[0, h, pl.ds(start, tk), :], qts[h][:, lo[c]:])
            if causal:
                key = (it * split + j) * tk + lax.broadcasted_iota(jnp.int32, (tk, tq - lo[c]), 0)
                s_t = jnp.where(key <= t_row[:, lo[c]:], s_t, NEG_INF)
            return s_t

        def update(c, s_t):
            h, j = chains[c]
            m, acc = state[c]
            m_new, acc_new = _flash_update_t((m[:, lo[c]:], acc[:, lo[c]:]), s_t,
                                             vt_ref[0, h, it * split + j], scale)
            if lo[c]:
                m_new = jnp.concatenate([m[:, :lo[c]], m_new], axis=1)
                acc_new = jnp.concatenate([acc[:, :lo[c]], acc_new], axis=1)
            return m_new, acc_new

        s_ts = [scores(c) for c in range(len(chains))]
        return tuple(update(c, s_ts[c]) for c in range(len(chains)))

    n_full = pl.program_id(2)
    carry = lax.fori_loop(0, n_full, lambda it, c: step(it, c, False),
                          tuple(_flash_init_t(tq) for _ in chains))
    carry = step(n_full, carry, True)
    outs = [_flash_finish_t(_flash_merge_t(carry[h * split:(h + 1) * split], scale)) for h in range(n_heads)]
    for p in range(n_heads // 2):
        o_ref[0, :, p * LANES:(p + 1) * LANES] = jnp.concatenate([outs[2 * p], outs[2 * p + 1]], axis=0).T


def _mla_call(qt_mla, k_mla, vt_mla):
    B, _, _, S = qt_mla.shape
    tq = min(MLA_TILE, S)
    tk = vt_mla.shape[4]
    nh = MLA_HEADS_PER_STEP
    assert tq % tk == 0 and MLA_HEADS % nh == 0 and nh % 2 == 0
    return pl.pallas_call(
        _mla_kernel,
        grid=(B, MLA_HEADS // nh, S // tq),
        in_specs=[
            pl.BlockSpec((1, nh, LANES, tq), lambda b, p, i: (b, p, 0, i)),
            pl.BlockSpec((1, nh, S, LANES), lambda b, p, i: (b, p, 0, 0)),
            pl.BlockSpec((1, nh, S // tk, V_ROWS, tk), lambda b, p, i: (b, p, 0, 0, 0)),
        ],
        out_specs=pl.BlockSpec((1, tq, nh * MLA_V_DIM), lambda b, p, i: (b, i, p)),
        out_shape=jax.ShapeDtypeStruct((B, S, MLA_WIDTH), jnp.float32),
        compiler_params=pltpu.CompilerParams(
            dimension_semantics=("parallel", "parallel", "arbitrary"), vmem_limit_bytes=VMEM_LIMIT),
        name="mla",
    )(qt_mla, k_mla, vt_mla)


def _mix_mlp_kernel(x_ref, oa_ref, ob_ref, na_ref, nb_ref, woa_ref, wob_ref, mn_ref, w1_ref, w2_ref,
                    fn_ref, o_ref, x1_ref, hn_ref, acc_ref, *, final):
    f = pl.program_id(1)

    @pl.when(f == 0)
    def _():
        na = _rms(oa_ref[...], na_ref[...]).astype(MXU_DTYPE)
        nb = _rms(ob_ref[...], nb_ref[...]).astype(MXU_DTYPE)
        x1 = x_ref[...] + (_dot(na, woa_ref[...]) + _dot(nb, wob_ref[...]))
        x1_ref[...] = x1
        hn_ref[...] = _rms(x1, mn_ref[...]).astype(MXU_DTYPE)
        acc_ref[...] = jnp.zeros_like(acc_ref)

    a = jnp.square(jnp.maximum(_dot(hn_ref[...], w1_ref[...]), 0.0))
    acc_ref[...] += _dot(a.astype(MXU_DTYPE), w2_ref[...])

    @pl.when(f == pl.num_programs(1) - 1)
    def _():
        y = x1_ref[...] + acc_ref[...]
        o_ref[...] = _rms(y, fn_ref[...]) if final else y


def _mix_mlp_call(x, o_a, o_b, na, nb, wo_a, wo_b, mn, w1, w2, fn, final):
    T, D = x.shape
    rows = min(MLP_ROWS, T)
    tf = MLP_FF_TILE
    row = lambda w: pl.BlockSpec((rows, w), lambda i, f: (i, 0))
    const = lambda a: pl.BlockSpec(a.shape, lambda i, f: (0, 0))
    return pl.pallas_call(
        functools.partial(_mix_mlp_kernel, final=final),
        grid=(T // rows, D_FF // tf),
        in_specs=[row(D), row(NSA_WIDTH), row(MLA_WIDTH), const(na), const(nb), const(wo_a), const(wo_b),
                  const(mn),
                  pl.BlockSpec((D, tf), lambda i, f: (0, f)),
                  pl.BlockSpec((tf, D), lambda i, f: (f, 0)),
                  const(fn)],
        out_specs=row(D),
        out_shape=jax.ShapeDtypeStruct((T, D), jnp.float32),
        scratch_shapes=[pltpu.VMEM((rows, D), jnp.float32), pltpu.VMEM((rows, D), MXU_DTYPE),
                        pltpu.VMEM((rows, D), jnp.float32)],
        compiler_params=pltpu.CompilerParams(
            dimension_semantics=("parallel", "arbitrary"), vmem_limit_bytes=VMEM_LIMIT),
        name="mix_mlp",
    )(x, o_a, o_b, na, nb, wo_a, wo_b, mn, w1, w2, fn)


def _rot_cols(w, dim):
    lead = w.shape[:-1]
    w4 = w.reshape(lead + (w.shape[-1] // dim, 2, dim // 2))
    return jnp.concatenate([-w4[..., 1, :], w4[..., 0, :]], axis=-1).reshape(w.shape)


def _pad_cols(w, left, total):
    pad = [(0, 0)] * (w.ndim - 1) + [(left, total - left - w.shape[-1])]
    return jnp.pad(w, pad)


def _pack_in_proj(w_in):
    sizes = (NSA_WIDTH,) + (NSA_KV_GROUPS * NSA_HEAD_DIM,) * 6 + (
        NSA_BRANCHES * NSA_HEADS, Q_LORA_RANK, KV_LORA_RANK, MLA_ROPE_DIM)
    offs = np.cumsum((0,) + sizes)
    q_a, k_c, v_c, k_s, v_s, k_w, v_w, g_a, c_q, c_kv, k_r = (
        w_in[..., offs[i]:offs[i + 1]] for i in range(len(sizes)))
    roped = jnp.concatenate([k_c, k_s, k_w], axis=-1)
    w_pm = jnp.concatenate(
        [roped, _rot_cols(roped, NSA_HEAD_DIM), v_c, c_q, c_kv,
         _pad_cols(k_r, HALF, LANES), _pad_cols(_rot_cols(k_r, MLA_ROPE_DIM), HALF, LANES)], axis=-1)
    w_fm = jnp.concatenate([q_a, _rot_cols(q_a, NSA_HEAD_DIM), v_s, v_w, _pad_cols(g_a, 0, _GATE_ROWS)], axis=-1)
    assert w_pm.shape[-1] == _PROJ_COLS and w_fm.shape[-1] == _FT_ROWS
    return w_pm.astype(MXU_DTYPE), jnp.swapaxes(w_fm, 1, 2).astype(MXU_DTYPE)


def _pack_q_up(w_q_up):
    L, R, _ = w_q_up.shape
    w = w_q_up.reshape(L, R, MLA_HEADS, MLA_NOPE_DIM + MLA_ROPE_DIM)
    rot = _rot_cols(w[..., MLA_NOPE_DIM:], MLA_ROPE_DIM)
    packed = jnp.concatenate([w.reshape(L, R, -1), rot.reshape(L, R, -1)], axis=-1)
    return jnp.swapaxes(packed, 1, 2).astype(MXU_DTYPE)


def _pack_kv_up(w_kv_up):
    L, R, _ = w_kv_up.shape
    w = w_kv_up.reshape(L, R, MLA_HEADS, MLA_NOPE_DIM + MLA_V_DIM)
    k = _pad_cols(w[..., :MLA_NOPE_DIM], 0, LANES).reshape(L, R, -1)
    v_t = jnp.swapaxes(w[..., MLA_NOPE_DIM:].reshape(L, R, -1), 1, 2)
    return k.astype(MXU_DTYPE), v_t.astype(MXU_DTYPE)


def _pack_compress(pos, w1, w2):
    L = pos.shape[0]
    G, dk, H = NSA_KV_GROUPS, NSA_HEAD_DIM, CMP_HIDDEN
    pos2 = jnp.concatenate([pos] * G, axis=-1).reshape(L, 2, CMP_STRIDE * G * dk)
    on_diag = lambda w, g, axis: jnp.stack([w if g2 == g else jnp.zeros_like(w) for g2 in range(G)], axis=axis)
    w1r = w1.astype(MXU_DTYPE).reshape(L, 2, CMP_STRIDE, dk, H)
    w1b = jnp.stack([on_diag(w1r, g, 4) for g in range(G)], axis=3)
    w1b = w1b.reshape(L, 2, CMP_STRIDE * G * dk, G * H)
    w2p = _pad_cols(w2.astype(MXU_DTYPE), 0, LANES)
    w2b = jnp.stack([on_diag(w2p, g, 2) for g in range(G)], axis=1)
    return pos2, w1b, w2b.reshape(L, G * H, G * LANES)


def _importance_matrix(n_chunk):
    a, b = SEL_BLOCK // CMP_STRIDE, CMP_BLOCK // CMP_STRIDE
    overlap = np.convolve(np.ones(a), np.ones(b))
    n_cmp = n_chunk - (b - 1)
    m = np.zeros((MAX_SEL_BLOCKS, n_chunk), np.float32)
    for blk in range(n_chunk * CMP_STRIDE // SEL_BLOCK):
        for j, wgt in enumerate(overlap):
            i = a * blk + j - (b - 1)
            if 0 <= i < n_cmp:
                m[blk, i] = wgt
    return jnp.asarray(m, MXU_DTYPE)


def _rope_tables(seq, dim):
    inv_freq = 1.0 / (ROPE_THETA ** (jnp.arange(0, dim, 2, dtype=jnp.float32) / dim))
    ang = jnp.arange(seq, dtype=jnp.float32)[:, None] * inv_freq[None, :]
    ang = jnp.concatenate([ang, ang], axis=-1)
    return jnp.cos(ang), jnp.sin(ang)


def kernel(x, attn_norm, w_in, cmp_pos_k, cmp_w1_k, cmp_w2_k, cmp_pos_v, cmp_w1_v, cmp_w2_v, mla_q_norm,
           w_q_up, mla_kv_norm, w_kv_up, nsa_out_norm, mla_out_norm, w_out, mlp_norm, w_ff1, w_ff2, final_norm):
    B, S, D = x.shape
    depth = w_in.shape[0]
    assert S % SEL_BLOCK == 0 and S // SEL_BLOCK <= MAX_SEL_BLOCKS and S % CMP_STRIDE == 0
    n_chunk = S // CMP_STRIDE

    cos64, sin64 = _rope_tables(S, NSA_HEAD_DIM)
    cos_a, sin_a = jnp.concatenate([cos64, cos64], -1), jnp.concatenate([sin64, sin64], -1)
    cos32, sin32 = _rope_tables(S, MLA_ROPE_DIM)
    pad = jnp.zeros((S, LANES - MLA_NOPE_DIM - MLA_ROPE_DIM), jnp.float32)
    cos_b = jnp.concatenate([jnp.ones((S, MLA_NOPE_DIM), jnp.float32), cos32, pad], -1)
    sin_b = jnp.concatenate([jnp.zeros((S, MLA_NOPE_DIM), jnp.float32), sin32, pad], -1)
    tabs = (cos_a, sin_a, cos64.T, sin64.T, cos_b, sin_b, cos32.T, sin32.T)

    w_pm, w_fm = _pack_in_proj(w_in)
    wq2t = _pack_q_up(w_q_up)
    wk2, wv2t = _pack_kv_up(w_kv_up)
    posk, w1k, w2k = _pack_compress(cmp_pos_k, cmp_w1_k, cmp_w2_k)
    posv, w1v, w2v = _pack_compress(cmp_pos_v, cmp_w1_v, cmp_w2_v)
    w2vt = jnp.swapaxes(w2v, 1, 2)
    imp_mat = _importance_matrix(n_chunk)
    wo = w_out.astype(MXU_DTYPE)
    w1 = w_ff1.astype(MXU_DTYPE)
    w2 = w_ff2.astype(MXU_DTYPE)
    row = lambda v: v.reshape(1, -1)

    for l in range(depth):
        (qt_nsa, k_sel, k_win, k_c, v_c, vt_sel, vt_win, gates_t, qt_mla, k_mla, vt_mla) = _proj_call(
            x, row(attn_norm[l]), w_pm[l], w_fm[l], tabs,
            row(mla_q_norm[l]), wq2t[l], row(mla_kv_norm[l]), wk2[l], wv2t[l])
        k_cmp, vt_cmp = _compress_call(
            k_c, v_c, posk[l], posv[l], w1k[l], w1v[l], w2k[l], w2vt[l])
        o_a = _nsa_call(qt_nsa, k_cmp, vt_cmp, k_sel, vt_sel, k_win, vt_win, gates_t, imp_mat)
        o_b = _mla_call(qt_mla, k_mla, vt_mla)
        x = _mix_mlp_call(
            x.reshape(B * S, D), o_a.reshape(B * S, NSA_WIDTH), o_b.reshape(B * S, MLA_WIDTH),
            row(nsa_out_norm[l]), row(mla_out_norm[l]), wo[l, :NSA_WIDTH], wo[l, NSA_WIDTH:],
            row(mlp_norm[l]), w1[l], w2[l], row(final_norm), final=(l == depth - 1)).reshape(B, S, D)
    return x
```

```python
import functools

import numpy as np
import jax
import jax.numpy as jnp
from jax import lax
from jax.experimental import pallas as pl
from jax.experimental.pallas import tpu as pltpu

D_MODEL = 1024
NSA_HEADS = 8
NSA_KV_GROUPS = 2
NSA_REP = NSA_HEADS // NSA_KV_GROUPS
NSA_HEAD_DIM = 64
NSA_BRANCHES = 3
CMP_BLOCK = 32
CMP_STRIDE = 16
CMP_HIDDEN = 256
SEL_BLOCK = 64
SEL_BLOCK_LOG2 = 6
N_SELECT = 16
N_LOCAL = 2
WINDOW = 512
MLA_HEADS = 8
MLA_NOPE_DIM = 64
MLA_ROPE_DIM = 32
MLA_V_DIM = 64
Q_LORA_RANK = 256
KV_LORA_RANK = 128
D_FF = 4 * D_MODEL
ROPE_THETA = 10000.0
NORM_EPS = 1e-6
NEG_INF = -1e30
SOFTMAX_FLOOR = 1e-30
NOT_CAUSAL = -1.0
TAKEN = -2.0
LOG2_E = 1.4426950408889634
NSA_WIDTH = NSA_HEADS * NSA_HEAD_DIM
MLA_WIDTH = MLA_HEADS * MLA_V_DIM

LANES = 128
HALF = LANES // 2
BF16_SUBLANES = 16
V_ROWS = HALF + BF16_SUBLANES
MAX_SEL_BLOCKS = HALF
MXU_DTYPE = jnp.bfloat16
VMEM_LIMIT = 52 * 1024 * 1024
PROJ_ROWS = 1024
ATTN_K_TILE = 256
ATTN_K_SPLIT = 2
NSA_Q_TILE = ATTN_K_TILE * ATTN_K_SPLIT
NSA_GROUPS_PER_STEP = 2
NSA_SUB_TILE = 128
WIN_V_TILE = 128
MLA_TILE = ATTN_K_TILE * ATTN_K_SPLIT
MLA_HEADS_PER_STEP = 8
MLP_ROWS = 512
MLP_FF_TILE = 2048

_N_ROPE_BLK = 3
_OFF_ROT = _N_ROPE_BLK * LANES
_OFF_VC = 2 * _OFF_ROT
_OFF_CQ = _OFF_VC + LANES
_OFF_CKV = _OFF_CQ + Q_LORA_RANK
_OFF_KRA = _OFF_CKV + KV_LORA_RANK
_OFF_KRB = _OFF_KRA + LANES
_PROJ_COLS = _OFF_KRB + LANES
_FT_QROT = NSA_WIDTH
_FT_VSEL = 2 * NSA_WIDTH
_FT_VWIN = _FT_VSEL + NSA_KV_GROUPS * NSA_HEAD_DIM
_FT_GATE = _FT_VWIN + NSA_KV_GROUPS * NSA_HEAD_DIM
_GATE_ROWS = 32
_FT_ROWS = _FT_GATE + _GATE_ROWS


def _dot(a, b):
    return jnp.dot(a, b, preferred_element_type=jnp.float32)


def _dot_nt(a, b):
    return lax.dot_general(a, b, (((1,), (1,)), ((), ())), preferred_element_type=jnp.float32)


def _rms(x, gain):
    return x * lax.rsqrt(jnp.mean(x * x, axis=-1, keepdims=True) + NORM_EPS) * gain


def _ones_row_block(n):
    return jnp.where(lax.broadcasted_iota(jnp.int32, (V_ROWS - HALF, n), 0) == 0, 1.0, 0.0)


def _flash_update_t(carry, s_t, v_t, scale):
    m, acc = carry
    c = scale * LOG2_E
    m_new = jnp.maximum(m, jnp.max(s_t, axis=0, keepdims=True))
    alpha = jnp.exp2((m - m_new) * c)
    p_t = jnp.exp2((s_t - m_new).astype(MXU_DTYPE) * c)
    return m_new, alpha * acc + _dot(v_t, p_t)


def _flash_init_t(n):
    return jnp.full((1, n), NEG_INF, jnp.float32), jnp.zeros((V_ROWS, n), jnp.float32)


def _flash_merge_t(states, scale):
    c = scale * LOG2_E
    m = states[0][0]
    for st in states[1:]:
        m = jnp.maximum(m, st[0])
    acc = jnp.exp2((states[0][0] - m) * c) * states[0][1]
    for st in states[1:]:
        acc = acc + jnp.exp2((st[0] - m) * c) * st[1]
    return m, acc


def _flash_finish_t(carry):
    acc = carry[1]
    return acc[:HALF] / jnp.maximum(acc[HALF:HALF + 1], SOFTMAX_FLOOR)


def _proj_kernel(x_ref, gain_ref, w_ref, wt_ref, cos_a_ref, sin_a_ref, cos_at_ref, sin_at_ref,
                 cos_b_ref, sin_b_ref, cos_rt_ref, sin_rt_ref, qn_ref, wq2t_ref, kvn_ref, wk2_ref, wv2t_ref,
                 q_nsa_ref, k_sel_ref, k_win_ref, kc_ref, vc_ref, vt_sel_ref, vt_win_ref, gate_ref,
                 q_mla_ref, k_mla_ref, vt_mla_ref, chunk_ref):
    rows = x_ref.shape[1]
    hn = _rms(x_ref[0], gain_ref[...]).astype(MXU_DTYPE)
    res = _dot(hn, w_ref[...])
    feat_t = _dot_nt(wt_ref[...], hn)
    ones_rows = _ones_row_block(rows)

    cos_at, sin_at = cos_at_ref[...], sin_at_ref[...]
    scale = NSA_HEAD_DIM ** -0.5
    for h in range(NSA_HEADS):
        lo = h * NSA_HEAD_DIM
        q_t = feat_t[lo:lo + NSA_HEAD_DIM] * cos_at + feat_t[_FT_QROT + lo:_FT_QROT + lo + NSA_HEAD_DIM] * sin_at
        q_nsa_ref[0, h] = (q_t * scale).astype(q_nsa_ref.dtype)
    for off, out_ref in ((_FT_VSEL, vt_sel_ref), (_FT_VWIN, vt_win_ref)):
        tile = out_ref.shape[4]
        for g in range(NSA_KV_GROUPS):
            lo = off + g * NSA_HEAD_DIM
            v_t = jnp.concatenate([feat_t[lo:lo + NSA_HEAD_DIM], ones_rows], axis=0)
            for j in range(rows // tile):
                out_ref[0, g, j] = v_t[:, j * tile:(j + 1) * tile].astype(out_ref.dtype)
    gate_ref[0] = jax.nn.sigmoid(feat_t[_FT_GATE:_FT_GATE + _GATE_ROWS])

    cos_a, sin_a = cos_a_ref[...], sin_a_ref[...]
    lane = lax.broadcasted_iota(jnp.int32, (rows, LANES), 1)
    lower = lane < HALF

    def rope_blk(j):
        return (res[:, j * LANES:(j + 1) * LANES] * cos_a
                + res[:, _OFF_ROT + j * LANES:_OFF_ROT + (j + 1) * LANES] * sin_a)

    def split_pair(blk):
        return jnp.where(lower, blk, 0.0), jnp.where(lower, pltpu.roll(blk, HALF, 1), 0.0)

    for out_ref, val in ((kc_ref, rope_blk(0)), (vc_ref, res[:, _OFF_VC:_OFF_VC + LANES])):
        chunk_ref[...] = val
        for p in range(CMP_STRIDE):
            out_ref[0, p] = chunk_ref[pl.ds(p, rows // CMP_STRIDE, stride=CMP_STRIDE), :]
    pos = pl.program_id(1) * rows + lax.broadcasted_iota(jnp.int32, (rows, LANES), 0)
    onehot = jnp.where((lane - HALF) == jnp.right_shift(pos, SEL_BLOCK_LOG2), 1.0, 0.0)
    a, b = split_pair(rope_blk(1))
    k_sel_ref[0, 0] = (a + onehot).astype(k_sel_ref.dtype)
    k_sel_ref[0, 1] = (b + onehot).astype(k_sel_ref.dtype)
    a, b = split_pair(rope_blk(2))
    k_win_ref[0, 0] = a.astype(k_win_ref.dtype)
    k_win_ref[0, 1] = b.astype(k_win_ref.dtype)

    cos_b, sin_b = cos_b_ref[...], sin_b_ref[...]
    cos_rt, sin_rt = cos_rt_ref[...], sin_rt_ref[...]
    cqn = _rms(res[:, _OFF_CQ:_OFF_CQ + Q_LORA_RANK], qn_ref[...]).astype(MXU_DTYPE)
    yq_t = _dot_nt(wq2t_ref[...], cqn)
    d_qk = MLA_NOPE_DIM + MLA_ROPE_DIM
    rot_off = MLA_HEADS * d_qk
    q_pad = jnp.zeros((LANES - d_qk, rows), jnp.float32)
    for h in range(MLA_HEADS):
        nope = yq_t[h * d_qk:h * d_qk + MLA_NOPE_DIM]
        rope = (yq_t[h * d_qk + MLA_NOPE_DIM:(h + 1) * d_qk] * cos_rt
                + yq_t[rot_off + h * MLA_ROPE_DIM:rot_off + (h + 1) * MLA_ROPE_DIM] * sin_rt)
        q_mla_ref[0, h] = jnp.concatenate([nope, rope, q_pad], axis=0).astype(q_mla_ref.dtype)
    ckvn = _rms(res[:, _OFF_CKV:_OFF_CKV + KV_LORA_RANK], kvn_ref[...]).astype(MXU_DTYPE)
    yk = _dot(ckvn, wk2_ref[...])
    k_pe = res[:, _OFF_KRA:_OFF_KRA + LANES] * cos_b + res[:, _OFF_KRB:_OFF_KRB + LANES] * sin_b
    for h in range(MLA_HEADS):
        k_mla_ref[0, h] = (yk[:, h * LANES:(h + 1) * LANES] + k_pe).astype(k_mla_ref.dtype)
    yv_t = _dot_nt(wv2t_ref[...], ckvn)
    tk = vt_mla_ref.shape[4]
    for h in range(MLA_HEADS):
        v_t = jnp.concatenate([yv_t[h * MLA_V_DIM:(h + 1) * MLA_V_DIM], ones_rows], axis=0)
        for j in range(rows // tk):
            vt_mla_ref[0, h, j] = v_t[:, j * tk:(j + 1) * tk].astype(vt_mla_ref.dtype)


def _proj_call(x, gain, w_pm, w_fm, tabs, qn, wq2t, kvn, wk2, wv2t):
    B, S, D = x.shape
    rows = min(PROJ_ROWS, S)
    ns = S // rows
    tk = min(ATTN_K_TILE, S)
    tw = min(WIN_V_TILE, S)
    const = lambda b, i: (0, 0)
    tab = lambda t: (pl.BlockSpec((rows, t.shape[1]), lambda b, i: (i, 0)) if t.shape[0] == S
                     else pl.BlockSpec((t.shape[0], rows), lambda b, i: (0, i)))
    hm = lambda n: pl.BlockSpec((1, n, rows, LANES), lambda b, i: (b, 0, i, 0))
    fm = lambda n, f: pl.BlockSpec((1, n, f, rows), lambda b, i: (b, 0, 0, i))
    tiles = lambda n, t: pl.BlockSpec((1, n, rows // t, V_ROWS, t), lambda b, i: (b, 0, i, 0, 0))
    chunks = pl.BlockSpec((1, CMP_STRIDE, rows // CMP_STRIDE, LANES), lambda b, i: (b, 0, i, 0))
    sds = jax.ShapeDtypeStruct
    G = NSA_KV_GROUPS
    return pl.pallas_call(
        _proj_kernel,
        grid=(B, ns),
        in_specs=[
            pl.BlockSpec((1, rows, D), lambda b, i: (b, i, 0)),
            pl.BlockSpec((1, D), const),
            pl.BlockSpec(w_pm.shape, const),
            pl.BlockSpec(w_fm.shape, const),
            *[tab(t) for t in tabs],
            pl.BlockSpec((1, Q_LORA_RANK), const),
            pl.BlockSpec(wq2t.shape, const),
            pl.BlockSpec((1, KV_LORA_RANK), const),
            pl.BlockSpec(wk2.shape, const),
            pl.BlockSpec(wv2t.shape, const),
        ],
        out_specs=[fm(NSA_HEADS, NSA_HEAD_DIM), hm(G), hm(G), chunks, chunks, tiles(G, tk), tiles(G, tw),
                   pl.BlockSpec((1, _GATE_ROWS, rows), lambda b, i: (b, 0, i)),
                   fm(MLA_HEADS, LANES), hm(MLA_HEADS), tiles(MLA_HEADS, tk)],
        out_shape=[
            sds((B, NSA_HEADS, NSA_HEAD_DIM, S), MXU_DTYPE),
            sds((B, G, S, LANES), MXU_DTYPE),
            sds((B, G, S, LANES), MXU_DTYPE),
            sds((B, CMP_STRIDE, S // CMP_STRIDE, LANES), jnp.float32),
            sds((B, CMP_STRIDE, S // CMP_STRIDE, LANES), jnp.float32),
            sds((B, G, S // tk, V_ROWS, tk), MXU_DTYPE),
            sds((B, G, S // tw, V_ROWS, tw), MXU_DTYPE),
            sds((B, _GATE_ROWS, S), jnp.float32),
            sds((B, MLA_HEADS, LANES, S), MXU_DTYPE),
            sds((B, MLA_HEADS, S, LANES), MXU_DTYPE),
            sds((B, MLA_HEADS, S // tk, V_ROWS, tk), MXU_DTYPE),
        ],
        scratch_shapes=[pltpu.VMEM((rows, LANES), jnp.float32)],
        compiler_params=pltpu.CompilerParams(
            dimension_semantics=("parallel", "parallel"), vmem_limit_bytes=VMEM_LIMIT),
        name="proj",
    )(x, gain, w_pm, w_fm, *tabs, qn, wq2t, kvn, wk2, wv2t)


def _compress_kernel(xk_ref, xv_ref, posk_ref, posv_ref, w1k_ref, w1v_ref, w2k_ref, w2vt_ref,
                     kcmp_ref, vtcmp_ref):
    n_chunk = xk_ref.shape[2]

    def hidden(x_ref, pos_ref, w1_ref):
        x = jnp.concatenate([x_ref[0, p] for p in range(CMP_STRIDE)], axis=1)
        h_lo = _dot((x + pos_ref[0:1, :]).astype(MXU_DTYPE), w1_ref[0])
        h_hi = _dot((x + pos_ref[1:2, :]).astype(MXU_DTYPE), w1_ref[1])
        hid = h_lo + pltpu.roll(h_hi, n_chunk - 1, 0)
        return (hid * jax.nn.sigmoid(hid)).astype(MXU_DTYPE)

    k_out = _dot(hidden(xk_ref, posk_ref, w1k_ref), w2k_ref[...])
    v_out_t = _dot_nt(w2vt_ref[...], hidden(xv_ref, posv_ref, w1v_ref))
    for g in range(NSA_KV_GROUPS):
        kcmp_ref[0, g] = k_out[:, g * LANES:(g + 1) * LANES].astype(kcmp_ref.dtype)
        vtcmp_ref[0, g] = v_out_t[g * LANES:(g + 1) * LANES].astype(vtcmp_ref.dtype)


def _compress_call(xk, xv, posk, posv, w1k, w1v, w2k, w2vt):
    B, _, n_chunk, _ = xk.shape
    xs = pl.BlockSpec((1, CMP_STRIDE, n_chunk, LANES), lambda b: (b, 0, 0, 0))
    c2 = lambda a: pl.BlockSpec(a.shape, lambda b: (0,) * a.ndim)
    G = NSA_KV_GROUPS
    return pl.pallas_call(
        _compress_kernel,
        grid=(B,),
        in_specs=[xs, xs, c2(posk), c2(posv), c2(w1k), c2(w1v), c2(w2k), c2(w2vt)],
        out_specs=[pl.BlockSpec((1, G, n_chunk, LANES), lambda b: (b, 0, 0, 0)),
                   pl.BlockSpec((1, G, LANES, n_chunk), lambda b: (b, 0, 0, 0))],
        out_shape=[jax.ShapeDtypeStruct((B, G, n_chunk, LANES), MXU_DTYPE),
                   jax.ShapeDtypeStruct((B, G, LANES, n_chunk), MXU_DTYPE)],
        compiler_params=pltpu.CompilerParams(
            dimension_semantics=("parallel",), vmem_limit_bytes=VMEM_LIMIT),
        name="compress",
    )(xk, xv, posk, posv, w1k, w1v, w2k, w2vt)


def _nsa_kernel(qt_ref, kcmp_ref, vtcmp_ref, ksel_ref, vtsel_ref, kwin_ref, vtwin_ref, gate_ref,
                imp_ref, cbias_ref, wbias_ref, o_ref, *, n_sel, win_keys, sub):
    n_gb = kcmp_ref.shape[1]
    tq = qt_ref.shape[3]
    n_sub = tq // sub
    n_chunk = kcmp_ref.shape[2]
    tk = vtsel_ref.shape[4]
    tw = vtwin_ref.shape[4]
    lanes = NSA_REP * sub
    tile0 = pl.program_id(2) * tq
    m_t = imp_ref[...]
    lane_pos = jnp.bitwise_and(lax.broadcasted_iota(jnp.int32, (1, lanes), 1), sub - 1)

    def front(gi, u):
        s0 = tile0 + u * sub
        q_t = jnp.concatenate([qt_ref[0, gi * NSA_REP + r, :, u * sub:(u + 1) * sub] for r in range(NSA_REP)],
                              axis=1)
        q_t0 = jnp.concatenate([q_t, jnp.zeros_like(q_t)], axis=0)
        t_row = s0 + lane_pos

        c_off = pl.multiple_of(n_chunk - s0 // CMP_STRIDE, 8)
        s = _dot(kcmp_ref[0, gi], q_t0) + cbias_ref[pl.ds(c_off, n_chunk), :]
        e = jnp.exp(s - jnp.max(s, axis=0, keepdims=True))
        any_valid = t_row >= CMP_BLOCK - 1
        inv = jnp.where(any_valid, 1.0 / jnp.maximum(jnp.sum(e, axis=0, keepdims=True), SOFTMAX_FLOOR), 0.0)
        p_cmp = e * inv
        o_cmp = _dot(vtcmp_ref[0, gi], p_cmp.astype(MXU_DTYPE))[:HALF]

        p_sum = p_cmp[:, 0:sub]
        for r in range(1, NSA_REP):
            p_sum = p_sum + p_cmp[:, r * sub:(r + 1) * sub]
        hi = p_sum.astype(MXU_DTYPE)
        rem = p_sum - hi.astype(jnp.float32)
        mid = rem.astype(MXU_DTYPE)
        lo = (rem - mid.astype(jnp.float32)).astype(MXU_DTYPE)
        imp_t = _dot(m_t, hi) + _dot(m_t, mid) + _dot(m_t, lo)

        blk = lax.broadcasted_iota(jnp.int32, (MAX_SEL_BLOCKS, sub), 0)
        cur = jnp.right_shift(s0 + lax.broadcasted_iota(jnp.int32, (MAX_SEL_BLOCKS, sub), 1), SEL_BLOCK_LOG2)
        valid = blk <= cur
        forced = valid & ((blk == 0) | (blk > cur - N_LOCAL))
        blk_f = blk.astype(jnp.float32)
        bias_t = jnp.where(forced, 0.0, NEG_INF)
        left = jnp.where(forced, TAKEN, jnp.where(valid, imp_t, NOT_CAUSAL))
        for _ in range(n_sel - (N_LOCAL + 1)):
            top = jnp.max(left, axis=0, keepdims=True)
            first = jnp.min(jnp.where(left == top, blk_f, float(MAX_SEL_BLOCKS)), axis=0, keepdims=True)
            hit = blk_f == first
            bias_t = jnp.where(hit, 0.0, bias_t)
            left = jnp.where(hit, TAKEN, left)
        bias_t = bias_t.astype(MXU_DTYPE)
        q_aug = jnp.concatenate([q_t, jnp.concatenate([bias_t] * NSA_REP, axis=1)], axis=0)

        w_start = pl.multiple_of(jnp.maximum(s0 - WINDOW, 0), sub)
        w_off = pl.multiple_of(WINDOW - (s0 - w_start), sub)
        s = _dot(kwin_ref[0, gi, pl.ds(w_start, win_keys), :], q_t0) + wbias_ref[pl.ds(w_off, win_keys), :]
        e = jnp.exp((s - jnp.max(s, axis=0, keepdims=True)).astype(MXU_DTYPE))
        w_tile = w_start // tw
        v_t = jnp.concatenate([vtwin_ref[0, gi, w_tile + j] for j in range(win_keys // tw)], axis=1)
        o_win = _flash_finish_t((None, _dot(v_t, e)))
        return q_aug, t_row, o_cmp, o_win

    units = [(gi, u) for gi in range(n_gb) for u in range(n_sub)]
    fronts = [front(gi, u) for gi, u in units]

    split = ATTN_K_SPLIT
    chains = [(x, j) for x in range(len(units)) for j in range(split)]

    def sel_step(it, state, diagonal):
        sub_of = lambda c: units[chains[c][0]][1]
        active = [c for c, (x, j) in enumerate(chains) if not (diagonal and j * tk >= (sub_of(c) + 1) * sub)]
        n_keys = {c: min(tk, (sub_of(c) + 1) * sub - chains[c][1] * tk) if diagonal else tk for c in active}
        s_ts = {}
        for c in active:
            x, j = chains[c]
            start = pl.multiple_of((it * split + j) * tk, tk)
            s_ts[c] = _dot(ksel_ref[0, units[x][0], pl.ds(start, n_keys[c]), :], fronts[x][0])
        if diagonal:
            for c in active:
                x, j = chains[c]
                if j * tk + n_keys[c] > sub_of(c) * sub:
                    key = (it * split + j) * tk + lax.broadcasted_iota(jnp.int32, (n_keys[c], lanes), 0)
                    s_ts[c] = jnp.where(key <= fronts[x][1], s_ts[c], NEG_INF)
        out = list(state)
        for c in active:
            x, j = chains[c]
            v_t = vtsel_ref[0, units[x][0], it * split + j]
            out[c] = _flash_update_t(state[c], s_ts[c], v_t[:, :n_keys[c]], 1.0)
        return tuple(out)

    n_full = pl.program_id(2)
    carry = lax.fori_loop(0, n_full, lambda it, c: sel_step(it, c, False),
                          tuple(_flash_init_t(lanes) for _ in chains))
    carry = sel_step(n_full, carry, True)

    for x, (gi, u) in enumerate(units):
        _, _, o_cmp, o_win = fronts[x]
        o_sel = _flash_finish_t(_flash_merge_t(carry[x * split:(x + 1) * split], 1.0))
        gates = gate_ref[0, :, u * sub:(u + 1) * sub]

        def gate(r, br, gates=gates, gi=gi):
            rows = [((grp * NSA_REP + r) * NSA_BRANCHES + br) for grp in range(NSA_KV_GROUPS)]
            if n_gb == NSA_KV_GROUPS:
                return gates[rows[gi]:rows[gi] + 1]
            return jnp.where(pl.program_id(1) == 0, gates[rows[0]:rows[0] + 1], gates[rows[1]:rows[1] + 1])

        heads = []
        for r in range(NSA_REP):
            sl = slice(r * sub, (r + 1) * sub)
            heads.append(gate(r, 0) * o_cmp[:, sl] + gate(r, 1) * o_sel[:, sl] + gate(r, 2) * o_win[:, sl])
        for pair in range(NSA_REP // 2):
            col = (gi * (NSA_REP // 2) + pair) * LANES
            o_ref[0, u * sub:(u + 1) * sub, col:col + LANES] = jnp.concatenate(
                [heads[2 * pair], heads[2 * pair + 1]], axis=0).T


def _mask_tables(n_chunk, sub, win_keys):
    pos = (np.arange(NSA_REP * sub) % sub)[None, :]
    r = np.arange(2 * n_chunk)[:, None]
    cmp_ok = CMP_STRIDE * (r - n_chunk) + CMP_BLOCK - 1 <= pos
    r = np.arange(WINDOW + win_keys)[:, None]
    win_ok = (r <= pos + WINDOW) & (r > pos)
    to_bias = lambda ok: jnp.asarray(np.where(ok, 0.0, NEG_INF), jnp.float32)
    return to_bias(cmp_ok), to_bias(win_ok)


def _nsa_call(qt_nsa, k_cmp, vt_cmp, k_sel, vt_sel, k_win, vt_win, gates_t, imp_mat):
    B, _, _, S = qt_nsa.shape
    n_chunk = k_cmp.shape[2]
    tq = min(NSA_Q_TILE, S)
    sub = min(NSA_SUB_TILE, tq)
    tk, tw = vt_sel.shape[4], vt_win.shape[4]
    assert ATTN_K_SPLIT * tk == tq and tq % sub == 0 and sub == LANES and sub % tw == 0
    n_sel = min(N_SELECT, S // SEL_BLOCK)
    win_keys = min(WINDOW + sub, S)
    cmp_bias, win_bias = _mask_tables(n_chunk, sub, win_keys)
    gb = NSA_GROUPS_PER_STEP
    const = lambda a: pl.BlockSpec(a.shape, lambda b, g, i: (0, 0))
    kv = lambda n: pl.BlockSpec((1, gb, n, LANES), lambda b, g, i: (b, g, 0, 0))
    vt = lambda a: pl.BlockSpec((1, gb) + a.shape[2:], lambda b, g, i: (b, g) + (0,) * (a.ndim - 2))
    return pl.pallas_call(
        functools.partial(_nsa_kernel, n_sel=n_sel, win_keys=win_keys, sub=sub),
        grid=(B, NSA_KV_GROUPS // gb, S // tq),
        in_specs=[
            pl.BlockSpec((1, gb * NSA_REP, NSA_HEAD_DIM, tq), lambda b, g, i: (b, g, 0, i)),
            kv(n_chunk), vt(vt_cmp), kv(S), vt(vt_sel), kv(S), vt(vt_win),
            pl.BlockSpec((1, _GATE_ROWS, tq), lambda b, g, i: (b, 0, i)),
            const(imp_mat), const(cmp_bias), const(win_bias),
        ],
        out_specs=pl.BlockSpec((1, tq, gb * NSA_REP * NSA_HEAD_DIM), lambda b, g, i: (b, i, g)),
        out_shape=jax.ShapeDtypeStruct((B, S, NSA_WIDTH), jnp.float32),
        compiler_params=pltpu.CompilerParams(
            dimension_semantics=("parallel", "parallel", "arbitrary"), vmem_limit_bytes=VMEM_LIMIT),
        name="nsa",
    )(qt_nsa, k_cmp, vt_cmp, k_sel, vt_sel, k_win, vt_win, gates_t, imp_mat, cmp_bias, win_bias)


def _mla_kernel(qt_ref, k_ref, vt_ref, o_ref):
    n_heads = qt_ref.shape[1]
    tq = qt_ref.shape[3]
    tk = vt_ref.shape[4]
    q0 = pl.program_id(2) * tq
    scale = (MLA_NOPE_DIM + MLA_ROPE_DIM) ** -0.5
    t_row = q0 + lax.broadcasted_iota(jnp.int32, (1, tq), 1)
    qts = [qt_ref[0, h] for h in range(n_heads)]
    split = tq // tk
    chains = [(h, j) for h in range(n_heads) for j in range(split)]

    def step(it, state, causal):
        lo = [j * tk if causal else 0 for _, j in chains]

        def scores(c):
            h, j = chains[c]
            start = pl.multiple_of((it * split + j) * tk, tk)
            s_t = _dot(k_ref[0, h, pl.ds(start, tk), :], qts[h][:, lo[c]:])
            if causal:
                key = (it * split + j) * tk + lax.broadcasted_iota(jnp.int32, (tk, tq - lo[c]), 0)
                s_t = jnp.where(key <= t_row[:, lo[c]:], s_t, NEG_INF)
            return s_t

        def update(c, s_t):
            h, j = chains[c]
            m, acc = state[c]
            m_new, acc_new = _flash_update_t((m[:, lo[c]:], acc[:, lo[c]:]), s_t,
                                             vt_ref[0, h, it * split + j], scale)
            if lo[c]:
                m_new = jnp.concatenate([m[:, :lo[c]], m_new], axis=1)
                acc_new = jnp.concatenate([acc[:, :lo[c]], acc_new], axis=1)
            return m_new, acc_new

        s_ts = [scores(c) for c in range(len(chains))]
        return tuple(update(c, s_ts[c]) for c in range(len(chains)))

    n_full = pl.program_id(2)
    carry = lax.fori_loop(0, n_full, lambda it, c: step(it, c, False),
                          tuple(_flash_init_t(tq) for _ in chains))
    carry = step(n_full, carry, True)
    outs = [_flash_finish_t(_flash_merge_t(carry[h * split:(h + 1) * split], scale)) for h in range(n_heads)]
    for p in range(n_heads // 2):
        o_ref[0, :, p * LANES:(p + 1) * LANES] = jnp.concatenate([outs[2 * p], outs[2 * p + 1]], axis=0).T


def _mla_call(qt_mla, k_mla, vt_mla):
    B, _, _, S = qt_mla.shape
    tq = min(MLA_TILE, S)
    tk = vt_mla.shape[4]
    nh = MLA_HEADS_PER_STEP
    assert tq % tk == 0 and MLA_HEADS % nh == 0 and nh % 2 == 0
    return pl.pallas_call(
        _mla_kernel,
        grid=(B, MLA_HEADS // nh, S // tq),
        in_specs=[
            pl.BlockSpec((1, nh, LANES, tq), lambda b, p, i: (b, p, 0, i)),
            pl.BlockSpec((1, nh, S, LANES), lambda b, p, i: (b, p, 0, 0)),
            pl.BlockSpec((1, nh, S // tk, V_ROWS, tk), lambda b, p, i: (b, p, 0, 0, 0)),
        ],
        out_specs=pl.BlockSpec((1, tq, nh * MLA_V_DIM), lambda b, p, i: (b, i, p)),
        out_shape=jax.ShapeDtypeStruct((B, S, MLA_WIDTH), jnp.float32),
        compiler_params=pltpu.CompilerParams(
            dimension_semantics=("parallel", "parallel", "arbitrary"), vmem_limit_bytes=VMEM_LIMIT),
        name="mla",
    )(qt_mla, k_mla, vt_mla)


def _mix_mlp_kernel(x_ref, oa_ref, ob_ref, na_ref, nb_ref, woa_ref, wob_ref, mn_ref, w1_ref, w2_ref,
                    fn_ref, o_ref, x1_ref, hn_ref, acc_ref, *, final):
    f = pl.program_id(1)

    @pl.when(f == 0)
    def _():
        na = _rms(oa_ref[...], na_ref[...]).astype(MXU_DTYPE)
        nb = _rms(ob_ref[...], nb_ref[...]).astype(MXU_DTYPE)
        x1 = x_ref[...] + (_dot(na, woa_ref[...]) + _dot(nb, wob_ref[...]))
        x1_ref[...] = x1
        hn_ref[...] = _rms(x1, mn_ref[...]).astype(MXU_DTYPE)
        acc_ref[...] = jnp.zeros_like(acc_ref)

    a = jnp.square(jnp.maximum(_dot(hn_ref[...], w1_ref[...]), 0.0))
    acc_ref[...] += _dot(a.astype(MXU_DTYPE), w2_ref[...])

    @pl.when(f == pl.num_programs(1) - 1)
    def _():
        y = x1_ref[...] + acc_ref[...]
        o_ref[...] = _rms(y, fn_ref[...]) if final else y


def _mix_mlp_call(x, o_a, o_b, na, nb, wo_a, wo_b, mn, w1, w2, fn, final):
    T, D = x.shape
    rows = min(MLP_ROWS, T)
    tf = MLP_FF_TILE
    row = lambda w: pl.BlockSpec((rows, w), lambda i, f: (i, 0))
    const = lambda a: pl.BlockSpec(a.shape, lambda i, f: (0, 0))
    return pl.pallas_call(
        functools.partial(_mix_mlp_kernel, final=final),
        grid=(T // rows, D_FF // tf),
        in_specs=[row(D), row(NSA_WIDTH), row(MLA_WIDTH), const(na), const(nb), const(wo_a), const(wo_b),
                  const(mn),
                  pl.BlockSpec((D, tf), lambda i, f: (0, f)),
                  pl.BlockSpec((tf, D), lambda i, f: (f, 0)),
                  const(fn)],
        out_specs=row(D),
        out_shape=jax.ShapeDtypeStruct((T, D), jnp.float32),
        scratch_shapes=[pltpu.VMEM((rows, D), jnp.float32), pltpu.VMEM((rows, D), MXU_DTYPE),
                        pltpu.VMEM((rows, D), jnp.float32)],
        compiler_params=pltpu.CompilerParams(
            dimension_semantics=("parallel", "arbitrary"), vmem_limit_bytes=VMEM_LIMIT),
        name="mix_mlp",
    )(x, o_a, o_b, na, nb, wo_a, wo_b, mn, w1, w2, fn)


def _rot_cols(w, dim):
    lead = w.shape[:-1]
    w4 = w.reshape(lead + (w.shape[-1] // dim, 2, dim // 2))
    return jnp.concatenate([-w4[..., 1, :], w4[..., 0, :]], axis=-1).reshape(w.shape)


def _pad_cols(w, left, total):
    pad = [(0, 0)] * (w.ndim - 1) + [(left, total - left - w.shape[-1])]
    return jnp.pad(w, pad)


def _pack_in_proj(w_in):
    sizes = (NSA_WIDTH,) + (NSA_KV_GROUPS * NSA_HEAD_DIM,) * 6 + (
        NSA_BRANCHES * NSA_HEADS, Q_LORA_RANK, KV_LORA_RANK, MLA_ROPE_DIM)
    offs = np.cumsum((0,) + sizes)
    q_a, k_c, v_c, k_s, v_s, k_w, v_w, g_a, c_q, c_kv, k_r = (
        w_in[..., offs[i]:offs[i + 1]] for i in range(len(sizes)))
    roped = jnp.concatenate([k_c, k_s, k_w], axis=-1)
    w_pm = jnp.concatenate(
        [roped, _rot_cols(roped, NSA_HEAD_DIM), v_c, c_q, c_kv,
         _pad_cols(k_r, HALF, LANES), _pad_cols(_rot_cols(k_r, MLA_ROPE_DIM), HALF, LANES)], axis=-1)
    w_fm = jnp.concatenate([q_a, _rot_cols(q_a, NSA_HEAD_DIM), v_s, v_w, _pad_cols(g_a, 0, _GATE_ROWS)], axis=-1)
    assert w_pm.shape[-1] == _PROJ_COLS and w_fm.shape[-1] == _FT_ROWS
    return w_pm.astype(MXU_DTYPE), jnp.swapaxes(w_fm, 1, 2).astype(MXU_DTYPE)


def _pack_q_up(w_q_up):
    L, R, _ = w_q_up.shape
    w = w_q_up.reshape(L, R, MLA_HEADS, MLA_NOPE_DIM + MLA_ROPE_DIM)
    rot = _rot_cols(w[..., MLA_NOPE_DIM:], MLA_ROPE_DIM)
    packed = jnp.concatenate([w.reshape(L, R, -1), rot.reshape(L, R, -1)], axis=-1)
    return jnp.swapaxes(packed, 1, 2).astype(MXU_DTYPE)


def _pack_kv_up(w_kv_up):
    L, R, _ = w_kv_up.shape
    w = w_kv_up.reshape(L, R, MLA_HEADS, MLA_NOPE_DIM + MLA_V_DIM)
    k = _pad_cols(w[..., :MLA_NOPE_DIM], 0, LANES).reshape(L, R, -1)
    v_t = jnp.swapaxes(w[..., MLA_NOPE_DIM:].reshape(L, R, -1), 1, 2)
    return k.astype(MXU_DTYPE), v_t.astype(MXU_DTYPE)


def _pack_compress(pos, w1, w2):
    L = pos.shape[0]
    G, dk, H = NSA_KV_GROUPS, NSA_HEAD_DIM, CMP_HIDDEN
    pos2 = jnp.concatenate([pos] * G, axis=-1).reshape(L, 2, CMP_STRIDE * G * dk)
    on_diag = lambda w, g, axis: jnp.stack([w if g2 == g else jnp.zeros_like(w) for g2 in range(G)], axis=axis)
    w1r = w1.astype(MXU_DTYPE).reshape(L, 2, CMP_STRIDE, dk, H)
    w1b = jnp.stack([on_diag(w1r, g, 4) for g in range(G)], axis=3)
    w1b = w1b.reshape(L, 2, CMP_STRIDE * G * dk, G * H)
    w2p = _pad_cols(w2.astype(MXU_DTYPE), 0, LANES)
    w2b = jnp.stack([on_diag(w2p, g, 2) for g in range(G)], axis=1)
    return pos2, w1b, w2b.reshape(L, G * H, G * LANES)


def _importance_matrix(n_chunk):
    a, b = SEL_BLOCK // CMP_STRIDE, CMP_BLOCK // CMP_STRIDE
    overlap = np.convolve(np.ones(a), np.ones(b))
    n_cmp = n_chunk - (b - 1)
    m = np.zeros((MAX_SEL_BLOCKS, n_chunk), np.float32)
    for blk in range(n_chunk * CMP_STRIDE // SEL_BLOCK):
        for j, wgt in enumerate(overlap):
            i = a * blk + j - (b - 1)
            if 0 <= i < n_cmp:
                m[blk, i] = wgt
    return jnp.asarray(m, MXU_DTYPE)


def _rope_tables(seq, dim):
    inv_freq = 1.0 / (ROPE_THETA ** (jnp.arange(0, dim, 2, dtype=jnp.float32) / dim))
    ang = jnp.arange(seq, dtype=jnp.float32)[:, None] * inv_freq[None, :]
    ang = jnp.concatenate([ang, ang], axis=-1)
    return jnp.cos(ang), jnp.sin(ang)


def kernel(x, attn_norm, w_in, cmp_pos_k, cmp_w1_k, cmp_w2_k, cmp_pos_v, cmp_w1_v, cmp_w2_v, mla_q_norm,
           w_q_up, mla_kv_norm, w_kv_up, nsa_out_norm, mla_out_norm, w_out, mlp_norm, w_ff1, w_ff2, final_norm):
    B, S, D = x.shape
    depth = w_in.shape[0]
    assert S % SEL_BLOCK == 0 and S // SEL_BLOCK <= MAX_SEL_BLOCKS and S % CMP_STRIDE == 0
    n_chunk = S // CMP_STRIDE

    cos64, sin64 = _rope_tables(S, NSA_HEAD_DIM)
    cos_a, sin_a = jnp.concatenate([cos64, cos64], -1), jnp.concatenate([sin64, sin64], -1)
    cos32, sin32 = _rope_tables(S, MLA_ROPE_DIM)
    pad = jnp.zeros((S, LANES - MLA_NOPE_DIM - MLA_ROPE_DIM), jnp.float32)
    cos_b = jnp.concatenate([jnp.ones((S, MLA_NOPE_DIM), jnp.float32), cos32, pad], -1)
    sin_b = jnp.concatenate([jnp.zeros((S, MLA_NOPE_DIM), jnp.float32), sin32, pad], -1)
    tabs = (cos_a, sin_a, cos64.T, sin64.T, cos_b, sin_b, cos32.T, sin32.T)

    w_pm, w_fm = _pack_in_proj(w_in)
    wq2t = _pack_q_up(w_q_up)
    wk2, wv2t = _pack_kv_up(w_kv_up)
    posk, w1k, w2k = _pack_compress(cmp_pos_k, cmp_w1_k, cmp_w2_k)
    posv, w1v, w2v = _pack_compress(cmp_pos_v, cmp_w1_v, cmp_w2_v)
    w2vt = jnp.swapaxes(w2v, 1, 2)
    imp_mat = _importance_matrix(n_chunk)
    wo = w_out.astype(MXU_DTYPE)
    w1 = w_ff1.astype(MXU_DTYPE)
    w2 = w_ff2.astype(MXU_DTYPE)
    row = lambda v: v.reshape(1, -1)

    for l in range(depth):
        (qt_nsa, k_sel, k_win, k_c, v_c, vt_sel, vt_win, gates_t, qt_mla, k_mla, vt_mla) = _proj_call(
            x, row(attn_norm[l]), w_pm[l], w_fm[l], tabs,
            row(mla_q_norm[l]), wq2t[l], row(mla_kv_norm[l]), wk2[l], wv2t[l])
        k_cmp, vt_cmp = _compress_call(
            k_c, v_c, posk[l], posv[l], w1k[l], w1v[l], w2k[l], w2vt[l])
        o_a = _nsa_call(qt_nsa, k_cmp, vt_cmp, k_sel, vt_sel, k_win, vt_win, gates_t, imp_mat)
        o_b = _mla_call(qt_mla, k_mla, vt_mla)
        x = _mix_mlp_call(
            x.reshape(B * S, D), o_a.reshape(B * S, NSA_WIDTH), o_b.reshape(B * S, MLA_WIDTH),
            row(nsa_out_norm[l]), row(mla_out_norm[l]), wo[l, :NSA_WIDTH], wo[l, NSA_WIDTH:],
            row(mlp_norm[l]), w1[l], w2[l], row(final_norm), final=(l == depth - 1)).reshape(B, S, D)
    return x
```

```python
import functools

import numpy as np
import jax
import jax.numpy as jnp
from jax import lax
from jax.experimental import pallas as pl
from jax.experimental.pallas import tpu as pltpu

D_MODEL = 1024
NSA_HEADS = 8
NSA_KV_GROUPS = 2
NSA_REP = NSA_HEADS // NSA_KV_GROUPS
NSA_HEAD_DIM = 64
NSA_BRANCHES = 3
CMP_BLOCK = 32
CMP_STRIDE = 16
CMP_HIDDEN = 256
SEL_BLOCK = 64
SEL_BLOCK_LOG2 = 6
N_SELECT = 16
N_LOCAL = 2
WINDOW = 512
MLA_HEADS = 8
MLA_NOPE_DIM = 64
MLA_ROPE_DIM = 32
MLA_V_DIM = 64
Q_LORA_RANK = 256
KV_LORA_RANK = 128
D_FF = 4 * D_MODEL
ROPE_THETA = 10000.0
NORM_EPS = 1e-6
NEG_INF = -1e30
SOFTMAX_FLOOR = 1e-30
NOT_CAUSAL = -1.0
TAKEN = -2.0
LOG2_E = 1.4426950408889634
NSA_WIDTH = NSA_HEADS * NSA_HEAD_DIM
MLA_WIDTH = MLA_HEADS * MLA_V_DIM

LANES = 128
HALF = LANES // 2
BF16_SUBLANES = 16
V_ROWS = HALF + BF16_SUBLANES
MAX_SEL_BLOCKS = HALF
MXU_DTYPE = jnp.bfloat16
VMEM_LIMIT = 52 * 1024 * 1024
PROJ_ROWS = 1024
ATTN_K_TILE = 256
ATTN_K_SPLIT = 2
NSA_Q_TILE = ATTN_K_TILE * ATTN_K_SPLIT
NSA_GROUPS_PER_STEP = 2
NSA_SUB_TILE = 128
WIN_V_TILE = 128
MLA_TILE = ATTN_K_TILE * ATTN_K_SPLIT
MLA_HEADS_PER_STEP = 8
MLP_ROWS = 256
MLP_FF_TILE = 4096

_N_ROPE_BLK = 3
_OFF_ROT = _N_ROPE_BLK * LANES
_OFF_VC = 2 * _OFF_ROT
_OFF_CQ = _OFF_VC + LANES
_OFF_CKV = _OFF_CQ + Q_LORA_RANK
_OFF_KRA = _OFF_CKV + KV_LORA_RANK
_OFF_KRB = _OFF_KRA + LANES
_PROJ_COLS = _OFF_KRB + LANES
_FT_QROT = NSA_WIDTH
_FT_VSEL = 2 * NSA_WIDTH
_FT_VWIN = _FT_VSEL + NSA_KV_GROUPS * NSA_HEAD_DIM
_FT_GATE = _FT_VWIN + NSA_KV_GROUPS * NSA_HEAD_DIM
_GATE_ROWS = 32
_FT_ROWS = _FT_GATE + _GATE_ROWS


def _dot(a, b):
    return jnp.dot(a, b, preferred_element_type=jnp.float32)


def _dot_nt(a, b):
    return lax.dot_general(a, b, (((1,), (1,)), ((), ())), preferred_element_type=jnp.float32)


def _rms(x, gain):
    return x * lax.rsqrt(jnp.mean(x * x, axis=-1, keepdims=True) + NORM_EPS) * gain


def _ones_row_block(n):
    return jnp.where(lax.broadcasted_iota(jnp.int32, (V_ROWS - HALF, n), 0) == 0, 1.0, 0.0)


def _flash_update_t(carry, s_t, v_t, scale):
    m, acc = carry
    c = scale * LOG2_E
    m_new = jnp.maximum(m, jnp.max(s_t, axis=0, keepdims=True))
    alpha = jnp.exp2((m - m_new) * c)
    p_t = jnp.exp2((s_t - m_new).astype(MXU_DTYPE) * c)
    return m_new, alpha * acc + _dot(v_t, p_t)


def _flash_init_t(n):
    return jnp.full((1, n), NEG_INF, jnp.float32), jnp.zeros((V_ROWS, n), jnp.float32)


def _flash_merge_t(states, scale):
    c = scale * LOG2_E
    m = states[0][0]
    for st in states[1:]:
        m = jnp.maximum(m, st[0])
    acc = jnp.exp2((states[0][0] - m) * c) * states[0][1]
    for st in states[1:]:
        acc = acc + jnp.exp2((st[0] - m) * c) * st[1]
    return m, acc


def _flash_finish_t(carry):
    acc = carry[1]
    return acc[:HALF] / jnp.maximum(acc[HALF:HALF + 1], SOFTMAX_FLOOR)


def _proj_kernel(x_ref, gain_ref, w_ref, wt_ref, cos_a_ref, sin_a_ref, cos_at_ref, sin_at_ref,
                 cos_b_ref, sin_b_ref, cos_rt_ref, sin_rt_ref, qn_ref, wq2t_ref, kvn_ref, wk2_ref, wv2t_ref,
                 q_nsa_ref, k_sel_ref, k_win_ref, kc_ref, vc_ref, vt_sel_ref, vt_win_ref, gate_ref,
                 q_mla_ref, k_mla_ref, vt_mla_ref, chunk_ref):
    rows = x_ref.shape[1]
    hn = _rms(x_ref[0], gain_ref[...]).astype(MXU_DTYPE)
    res = _dot(hn, w_ref[...])
    feat_t = _dot_nt(wt_ref[...], hn)
    ones_rows = _ones_row_block(rows)

    cos_at, sin_at = cos_at_ref[...], sin_at_ref[...]
    scale = NSA_HEAD_DIM ** -0.5
    for h in range(NSA_HEADS):
        lo = h * NSA_HEAD_DIM
        q_t = feat_t[lo:lo + NSA_HEAD_DIM] * cos_at + feat_t[_FT_QROT + lo:_FT_QROT + lo + NSA_HEAD_DIM] * sin_at
        q_nsa_ref[0, h] = (q_t * scale).astype(q_nsa_ref.dtype)
    for off, out_ref in ((_FT_VSEL, vt_sel_ref), (_FT_VWIN, vt_win_ref)):
        tile = out_ref.shape[4]
        for g in range(NSA_KV_GROUPS):
            lo = off + g * NSA_HEAD_DIM
            v_t = jnp.concatenate([feat_t[lo:lo + NSA_HEAD_DIM], ones_rows], axis=0)
            for j in range(rows // tile):
                out_ref[0, g, j] = v_t[:, j * tile:(j + 1) * tile].astype(out_ref.dtype)
    gate_ref[0] = jax.nn.sigmoid(feat_t[_FT_GATE:_FT_GATE + _GATE_ROWS])

    cos_a, sin_a = cos_a_ref[...], sin_a_ref[...]
    lane = lax.broadcasted_iota(jnp.int32, (rows, LANES), 1)
    lower = lane < HALF

    def rope_blk(j):
        return (res[:, j * LANES:(j + 1) * LANES] * cos_a
                + res[:, _OFF_ROT + j * LANES:_OFF_ROT + (j + 1) * LANES] * sin_a)

    def split_pair(blk):
        return jnp.where(lower, blk, 0.0), jnp.where(lower, pltpu.roll(blk, HALF, 1), 0.0)

    for out_ref, val in ((kc_ref, rope_blk(0)), (vc_ref, res[:, _OFF_VC:_OFF_VC + LANES])):
        chunk_ref[...] = val
        for p in range(CMP_STRIDE):
            out_ref[0, p] = chunk_ref[pl.ds(p, rows // CMP_STRIDE, stride=CMP_STRIDE), :]
    pos = pl.program_id(1) * rows + lax.broadcasted_iota(jnp.int32, (rows, LANES), 0)
    onehot = jnp.where((lane - HALF) == jnp.right_shift(pos, SEL_BLOCK_LOG2), 1.0, 0.0)
    a, b = split_pair(rope_blk(1))
    k_sel_ref[0, 0] = (a + onehot).astype(k_sel_ref.dtype)
    k_sel_ref[0, 1] = (b + onehot).astype(k_sel_ref.dtype)
    a, b = split_pair(rope_blk(2))
    k_win_ref[0, 0] = a.astype(k_win_ref.dtype)
    k_win_ref[0, 1] = b.astype(k_win_ref.dtype)

    cos_b, sin_b = cos_b_ref[...], sin_b_ref[...]
    cos_rt, sin_rt = cos_rt_ref[...], sin_rt_ref[...]
    cqn = _rms(res[:, _OFF_CQ:_OFF_CQ + Q_LORA_RANK], qn_ref[...]).astype(MXU_DTYPE)
    yq_t = _dot_nt(wq2t_ref[...], cqn)
    d_qk = MLA_NOPE_DIM + MLA_ROPE_DIM
    rot_off = MLA_HEADS * d_qk
    q_pad = jnp.zeros((LANES - d_qk, rows), jnp.float32)
    for h in range(MLA_HEADS):
        nope = yq_t[h * d_qk:h * d_qk + MLA_NOPE_DIM]
        rope = (yq_t[h * d_qk + MLA_NOPE_DIM:(h + 1) * d_qk] * cos_rt
                + yq_t[rot_off + h * MLA_ROPE_DIM:rot_off + (h + 1) * MLA_ROPE_DIM] * sin_rt)
        q_mla_ref[0, h] = jnp.concatenate([nope, rope, q_pad], axis=0).astype(q_mla_ref.dtype)
    ckvn = _rms(res[:, _OFF_CKV:_OFF_CKV + KV_LORA_RANK], kvn_ref[...]).astype(MXU_DTYPE)
    yk = _dot(ckvn, wk2_ref[...])
    k_pe = res[:, _OFF_KRA:_OFF_KRA + LANES] * cos_b + res[:, _OFF_KRB:_OFF_KRB + LANES] * sin_b
    for h in range(MLA_HEADS):
        k_mla_ref[0, h] = (yk[:, h * LANES:(h + 1) * LANES] + k_pe).astype(k_mla_ref.dtype)
    yv_t = _dot_nt(wv2t_ref[...], ckvn)
    tk = vt_mla_ref.shape[4]
    for h in range(MLA_HEADS):
        v_t = jnp.concatenate([yv_t[h * MLA_V_DIM:(h + 1) * MLA_V_DIM], ones_rows], axis=0)
        for j in range(rows // tk):
            vt_mla_ref[0, h, j] = v_t[:, j * tk:(j + 1) * tk].astype(vt_mla_ref.dtype)


def _proj_call(x, gain, w_pm, w_fm, tabs, qn, wq2t, kvn, wk2, wv2t):
    B, S, D = x.shape
    rows = min(PROJ_ROWS, S)
    ns = S // rows
    tk = min(ATTN_K_TILE, S)
    tw = min(WIN_V_TILE, S)
    const = lambda b, i: (0, 0)
    tab = lambda t: (pl.BlockSpec((rows, t.shape[1]), lambda b, i: (i, 0)) if t.shape[0] == S
                     else pl.BlockSpec((t.shape[0], rows), lambda b, i: (0, i)))
    hm = lambda n: pl.BlockSpec((1, n, rows, LANES), lambda b, i: (b, 0, i, 0))
    fm = lambda n, f: pl.BlockSpec((1, n, f, rows), lambda b, i: (b, 0, 0, i))
    tiles = lambda n, t: pl.BlockSpec((1, n, rows // t, V_ROWS, t), lambda b, i: (b, 0, i, 0, 0))
    chunks = pl.BlockSpec((1, CMP_STRIDE, rows // CMP_STRIDE, LANES), lambda b, i: (b, 0, i, 0))
    sds = jax.ShapeDtypeStruct
    G = NSA_KV_GROUPS
    return pl.pallas_call(
        _proj_kernel,
        grid=(B, ns),
        in_specs=[
            pl.BlockSpec((1, rows, D), lambda b, i: (b, i, 0)),
            pl.BlockSpec((1, D), const),
            pl.BlockSpec(w_pm.shape, const),
            pl.BlockSpec(w_fm.shape, const),
            *[tab(t) for t in tabs],
            pl.BlockSpec((1, Q_LORA_RANK), const),
            pl.BlockSpec(wq2t.shape, const),
            pl.BlockSpec((1, KV_LORA_RANK), const),
            pl.BlockSpec(wk2.shape, const),
            pl.BlockSpec(wv2t.shape, const),
        ],
        out_specs=[fm(NSA_HEADS, NSA_HEAD_DIM), hm(G), hm(G), chunks, chunks, tiles(G, tk), tiles(G, tw),
                   pl.BlockSpec((1, _GATE_ROWS, rows), lambda b, i: (b, 0, i)),
                   fm(MLA_HEADS, LANES), hm(MLA_HEADS), tiles(MLA_HEADS, tk)],
        out_shape=[
            sds((B, NSA_HEADS, NSA_HEAD_DIM, S), MXU_DTYPE),
            sds((B, G, S, LANES), MXU_DTYPE),
            sds((B, G, S, LANES), MXU_DTYPE),
            sds((B, CMP_STRIDE, S // CMP_STRIDE, LANES), jnp.float32),
            sds((B, CMP_STRIDE, S // CMP_STRIDE, LANES), jnp.float32),
            sds((B, G, S // tk, V_ROWS, tk), MXU_DTYPE),
            sds((B, G, S // tw, V_ROWS, tw), MXU_DTYPE),
            sds((B, _GATE_ROWS, S), jnp.float32),
            sds((B, MLA_HEADS, LANES, S), MXU_DTYPE),
            sds((B, MLA_HEADS, S, LANES), MXU_DTYPE),
            sds((B, MLA_HEADS, S // tk, V_ROWS, tk), MXU_DTYPE),
        ],
        scratch_shapes=[pltpu.VMEM((rows, LANES), jnp.float32)],
        compiler_params=pltpu.CompilerParams(
            dimension_semantics=("parallel", "parallel"), vmem_limit_bytes=VMEM_LIMIT),
        name="proj",
    )(x, gain, w_pm, w_fm, *tabs, qn, wq2t, kvn, wk2, wv2t)


def _compress_kernel(xk_ref, xv_ref, posk_ref, posv_ref, w1k_ref, w1v_ref, w2k_ref, w2vt_ref,
                     kcmp_ref, vtcmp_ref):
    n_chunk = xk_ref.shape[2]

    def hidden(x_ref, pos_ref, w1_ref):
        x = jnp.concatenate([x_ref[0, p] for p in range(CMP_STRIDE)], axis=1)
        h_lo = _dot((x + pos_ref[0:1, :]).astype(MXU_DTYPE), w1_ref[0])
        h_hi = _dot((x + pos_ref[1:2, :]).astype(MXU_DTYPE), w1_ref[1])
        hid = h_lo + pltpu.roll(h_hi, n_chunk - 1, 0)
        return (hid * jax.nn.sigmoid(hid)).astype(MXU_DTYPE)

    k_out = _dot(hidden(xk_ref, posk_ref, w1k_ref), w2k_ref[...])
    v_out_t = _dot_nt(w2vt_ref[...], hidden(xv_ref, posv_ref, w1v_ref))
    for g in range(NSA_KV_GROUPS):
        kcmp_ref[0, g] = k_out[:, g * LANES:(g + 1) * LANES].astype(kcmp_ref.dtype)
        vtcmp_ref[0, g] = v_out_t[g * LANES:(g + 1) * LANES].astype(vtcmp_ref.dtype)


def _compress_call(xk, xv, posk, posv, w1k, w1v, w2k, w2vt):
    B, _, n_chunk, _ = xk.shape
    xs = pl.BlockSpec((1, CMP_STRIDE, n_chunk, LANES), lambda b: (b, 0, 0, 0))
    c2 = lambda a: pl.BlockSpec(a.shape, lambda b: (0,) * a.ndim)
    G = NSA_KV_GROUPS
    return pl.pallas_call(
        _compress_kernel,
        grid=(B,),
        in_specs=[xs, xs, c2(posk), c2(posv), c2(w1k), c2(w1v), c2(w2k), c2(w2vt)],
        out_specs=[pl.BlockSpec((1, G, n_chunk, LANES), lambda b: (b, 0, 0, 0)),
                   pl.BlockSpec((1, G, LANES, n_chunk), lambda b: (b, 0, 0, 0))],
        out_shape=[jax.ShapeDtypeStruct((B, G, n_chunk, LANES), MXU_DTYPE),
                   jax.ShapeDtypeStruct((B, G, LANES, n_chunk), MXU_DTYPE)],
        compiler_params=pltpu.CompilerParams(
            dimension_semantics=("parallel",), vmem_limit_bytes=VMEM_LIMIT),
        name="compress",
    )(xk, xv, posk, posv, w1k, w1v, w2k, w2vt)


def _nsa_kernel(qt_ref, kcmp_ref, vtcmp_ref, ksel_ref, vtsel_ref, kwin_ref, vtwin_ref, gate_ref,
                imp_ref, cbias_ref, wbias_ref, o_ref, *, n_sel, win_keys, sub):
    n_gb = kcmp_ref.shape[1]
    tq = qt_ref.shape[3]
    n_sub = tq // sub
    n_chunk = kcmp_ref.shape[2]
    tk = vtsel_ref.shape[4]
    tw = vtwin_ref.shape[4]
    lanes = NSA_REP * sub
    tile0 = pl.program_id(2) * tq
    m_t = imp_ref[...]
    lane_pos = jnp.bitwise_and(lax.broadcasted_iota(jnp.int32, (1, lanes), 1), sub - 1)

    def front(gi, u):
        s0 = tile0 + u * sub
        q_t = jnp.concatenate([qt_ref[0, gi * NSA_REP + r, :, u * sub:(u + 1) * sub] for r in range(NSA_REP)],
                              axis=1)
        q_t0 = jnp.concatenate([q_t, jnp.zeros_like(q_t)], axis=0)
        t_row = s0 + lane_pos

        c_off = pl.multiple_of(n_chunk - s0 // CMP_STRIDE, 8)
        s = _dot(kcmp_ref[0, gi], q_t0) + cbias_ref[pl.ds(c_off, n_chunk), :]
        e = jnp.exp(s - jnp.max(s, axis=0, keepdims=True))
        any_valid = t_row >= CMP_BLOCK - 1
        inv = jnp.where(any_valid, 1.0 / jnp.maximum(jnp.sum(e, axis=0, keepdims=True), SOFTMAX_FLOOR), 0.0)
        p_cmp = e * inv
        o_cmp = _dot(vtcmp_ref[0, gi], p_cmp.astype(MXU_DTYPE))[:HALF]

        p_sum = p_cmp[:, 0:sub]
        for r in range(1, NSA_REP):
            p_sum = p_sum + p_cmp[:, r * sub:(r + 1) * sub]
        hi = p_sum.astype(MXU_DTYPE)
        rem = p_sum - hi.astype(jnp.float32)
        mid = rem.astype(MXU_DTYPE)
        lo = (rem - mid.astype(jnp.float32)).astype(MXU_DTYPE)
        imp_t = _dot(m_t, hi) + _dot(m_t, mid) + _dot(m_t, lo)

        blk = lax.broadcasted_iota(jnp.int32, (MAX_SEL_BLOCKS, sub), 0)
        cur = jnp.right_shift(s0 + lax.broadcasted_iota(jnp.int32, (MAX_SEL_BLOCKS, sub), 1), SEL_BLOCK_LOG2)
        valid = blk <= cur
        forced = valid & ((blk == 0) | (blk > cur - N_LOCAL))
        blk_f = blk.astype(jnp.float32)
        bias_t = jnp.where(forced, 0.0, NEG_INF)
        left = jnp.where(forced, TAKEN, jnp.where(valid, imp_t, NOT_CAUSAL))
        for _ in range(n_sel - (N_LOCAL + 1)):
            top = jnp.max(left, axis=0, keepdims=True)
            first = jnp.min(jnp.where(left == top, blk_f, float(MAX_SEL_BLOCKS)), axis=0, keepdims=True)
            hit = blk_f == first
            bias_t = jnp.where(hit, 0.0, bias_t)
            left = jnp.where(hit, TAKEN, left)
        bias_t = bias_t.astype(MXU_DTYPE)
        q_aug = jnp.concatenate([q_t, jnp.concatenate([bias_t] * NSA_REP, axis=1)], axis=0)

        w_start = pl.multiple_of(jnp.maximum(s0 - WINDOW, 0), sub)
        w_off = pl.multiple_of(WINDOW - (s0 - w_start), sub)
        s = _dot(kwin_ref[0, gi, pl.ds(w_start, win_keys), :], q_t0) + wbias_ref[pl.ds(w_off, win_keys), :]
        e = jnp.exp((s - jnp.max(s, axis=0, keepdims=True)).astype(MXU_DTYPE))
        w_tile = w_start // tw
        v_t = jnp.concatenate([vtwin_ref[0, gi, w_tile + j] for j in range(win_keys // tw)], axis=1)
        o_win = _flash_finish_t((None, _dot(v_t, e)))
        return q_aug, t_row, o_cmp, o_win

    units = [(gi, u) for gi in range(n_gb) for u in range(n_sub)]
    fronts = [front(gi, u) for gi, u in units]

    split = ATTN_K_SPLIT
    chains = [(x, j) for x in range(len(units)) for j in range(split)]

    def sel_step(it, state, diagonal):
        sub_of = lambda c: units[chains[c][0]][1]
        active = [c for c, (x, j) in enumerate(chains) if not (diagonal and j * tk >= (sub_of(c) + 1) * sub)]
        n_keys = {c: min(tk, (sub_of(c) + 1) * sub - chains[c][1] * tk) if diagonal else tk for c in active}
        s_ts = {}
        for c in active:
            x, j = chains[c]
            start = pl.multiple_of((it * split + j) * tk, tk)
            s_ts[c] = _dot(ksel_ref[0, units[x][0], pl.ds(start, n_keys[c]), :], fronts[x][0])
        if diagonal:
            for c in active:
                x, j = chains[c]
                if j * tk + n_keys[c] > sub_of(c) * sub:
                    key = (it * split + j) * tk + lax.broadcasted_iota(jnp.int32, (n_keys[c], lanes), 0)
                    s_ts[c] = jnp.where(key <= fronts[x][1], s_ts[c], NEG_INF)
        out = list(state)
        for c in active:
            x, j = chains[c]
            v_t = vtsel_ref[0, units[x][0], it * split + j]
            out[c] = _flash_update_t(state[c], s_ts[c], v_t[:, :n_keys[c]], 1.0)
        return tuple(out)

    n_full = pl.program_id(2)
    carry = lax.fori_loop(0, n_full, lambda it, c: sel_step(it, c, False),
                          tuple(_flash_init_t(lanes) for _ in chains))
    carry = sel_step(n_full, carry, True)

    for x, (gi, u) in enumerate(units):
        _, _, o_cmp, o_win = fronts[x]
        o_sel = _flash_finish_t(_flash_merge_t(carry[x * split:(x + 1) * split], 1.0))
        gates = gate_ref[0, :, u * sub:(u + 1) * sub]

        def gate(r, br, gates=gates, gi=gi):
            rows = [((grp * NSA_REP + r) * NSA_BRANCHES + br) for grp in range(NSA_KV_GROUPS)]
            if n_gb == NSA_KV_GROUPS:
                return gates[rows[gi]:rows[gi] + 1]
            return jnp.where(pl.program_id(1) == 0, gates[rows[0]:rows[0] + 1], gates[rows[1]:rows[1] + 1])

        heads = []
        for r in range(NSA_REP):
            sl = slice(r * sub, (r + 1) * sub)
            heads.append(gate(r, 0) * o_cmp[:, sl] + gate(r, 1) * o_sel[:, sl] + gate(r, 2) * o_win[:, sl])
        for pair in range(NSA_REP // 2):
            col = (gi * (NSA_REP // 2) + pair) * LANES
            o_ref[0, u * sub:(u + 1) * sub, col:col + LANES] = jnp.concatenate(
                [heads[2 * pair], heads[2 * pair + 1]], axis=0).T


def _mask_tables(n_chunk, sub, win_keys):
    pos = (np.arange(NSA_REP * sub) % sub)[None, :]
    r = np.arange(2 * n_chunk)[:, None]
    cmp_ok = CMP_STRIDE * (r - n_chunk) + CMP_BLOCK - 1 <= pos
    r = np.arange(WINDOW + win_keys)[:, None]
    win_ok = (r <= pos + WINDOW) & (r > pos)
    to_bias = lambda ok: jnp.asarray(np.where(ok, 0.0, NEG_INF), jnp.float32)
    return to_bias(cmp_ok), to_bias(win_ok)


def _nsa_call(qt_nsa, k_cmp, vt_cmp, k_sel, vt_sel, k_win, vt_win, gates_t, imp_mat):
    B, _, _, S = qt_nsa.shape
    n_chunk = k_cmp.shape[2]
    tq = min(NSA_Q_TILE, S)
    sub = min(NSA_SUB_TILE, tq)
    tk, tw = vt_sel.shape[4], vt_win.shape[4]
    assert ATTN_K_SPLIT * tk == tq and tq % sub == 0 and sub == LANES and sub % tw == 0
    n_sel = min(N_SELECT, S // SEL_BLOCK)
    win_keys = min(WINDOW + sub, S)
    cmp_bias, win_bias = _mask_tables(n_chunk, sub, win_keys)
    gb = NSA_GROUPS_PER_STEP
    const = lambda a: pl.BlockSpec(a.shape, lambda b, g, i: (0, 0))
    kv = lambda n: pl.BlockSpec((1, gb, n, LANES), lambda b, g, i: (b, g, 0, 0))
    vt = lambda a: pl.BlockSpec((1, gb) + a.shape[2:], lambda b, g, i: (b, g) + (0,) * (a.ndim - 2))
    return pl.pallas_call(
        functools.partial(_nsa_kernel, n_sel=n_sel, win_keys=win_keys, sub=sub),
        grid=(B, NSA_KV_GROUPS // gb, S // tq),
        in_specs=[
            pl.BlockSpec((1, gb * NSA_REP, NSA_HEAD_DIM, tq), lambda b, g, i: (b, g, 0, i)),
            kv(n_chunk), vt(vt_cmp), kv(S), vt(vt_sel), kv(S), vt(vt_win),
            pl.BlockSpec((1, _GATE_ROWS, tq), lambda b, g, i: (b, 0, i)),
            const(imp_mat), const(cmp_bias), const(win_bias),
        ],
        out_specs=pl.BlockSpec((1, tq, gb * NSA_REP * NSA_HEAD_DIM), lambda b, g, i: (b, i, g)),
        out_shape=jax.ShapeDtypeStruct((B, S, NSA_WIDTH), jnp.float32),
        compiler_params=pltpu.CompilerParams(
            dimension_semantics=("parallel", "parallel", "arbitrary"), vmem_limit_bytes=VMEM_LIMIT),
        name="nsa",
    )(qt_nsa, k_cmp, vt_cmp, k_sel, vt_sel, k_win, vt_win, gates_t, imp_mat, cmp_bias, win_bias)


def _mla_kernel(qt_ref, k_ref, vt_ref, o_ref):
    n_heads = qt_ref.shape[1]
    tq = qt_ref.shape[3]
    tk = vt_ref.shape[4]
    q0 = pl.program_id(2) * tq
    scale = (MLA_NOPE_DIM + MLA_ROPE_DIM) ** -0.5
    t_row = q0 + lax.broadcasted_iota(jnp.int32, (1, tq), 1)
    qts = [qt_ref[0, h] for h in range(n_heads)]
    split = tq // tk
    chains = [(h, j) for h in range(n_heads) for j in range(split)]

    def step(it, state, causal):
        lo = [j * tk if causal else 0 for _, j in chains]

        def scores(c):
            h, j = chains[c]
            start = pl.multiple_of((it * split + j) * tk, tk)
            s_t = _dot(k_ref[0, h, pl.ds(start, tk), :], qts[h][:, lo[c]:])
            if causal:
                key = (it * split + j) * tk + lax.broadcasted_iota(jnp.int32, (tk, tq - lo[c]), 0)
                s_t = jnp.where(key <= t_row[:, lo[c]:], s_t, NEG_INF)
            return s_t

        def update(c, s_t):
            h, j = chains[c]
            m, acc = state[c]
            m_new, acc_new = _flash_update_t((m[:, lo[c]:], acc[:, lo[c]:]), s_t,
                                             vt_ref[0, h, it * split + j], scale)
            if lo[c]:
                m_new = jnp.concatenate([m[:, :lo[c]], m_new], axis=1)
                acc_new = jnp.concatenate([acc[:, :lo[c]], acc_new], axis=1)
            return m_new, acc_new

        s_ts = [scores(c) for c in range(len(chains))]
        return tuple(update(c, s_ts[c]) for c in range(len(chains)))

    n_full = pl.program_id(2)
    carry = lax.fori_loop(0, n_full, lambda it, c: step(it, c, False),
                          tuple(_flash_init_t(tq) for _ in chains))
    carry = step(n_full, carry, True)
    outs = [_flash_finish_t(_flash_merge_t(carry[h * split:(h + 1) * split], scale)) for h in range(n_heads)]
    for p in range(n_heads // 2):
        o_ref[0, :, p * LANES:(p + 1) * LANES] = jnp.concatenate([outs[2 * p], outs[2 * p + 1]], axis=0).T


def _mla_call(qt_mla, k_mla, vt_mla):
    B, _, _, S = qt_mla.shape
    tq = min(MLA_TILE, S)
    tk = vt_mla.shape[4]
    nh = MLA_HEADS_PER_STEP
    assert tq % tk == 0 and MLA_HEADS % nh == 0 and nh % 2 == 0
    return pl.pallas_call(
        _mla_kernel,
        grid=(B, MLA_HEADS // nh, S // tq),
        in_specs=[
            pl.BlockSpec((1, nh, LANES, tq), lambda b, p, i: (b, p, 0, i)),
            pl.BlockSpec((1, nh, S, LANES), lambda b, p, i: (b, p, 0, 0)),
            pl.BlockSpec((1, nh, S // tk, V_ROWS, tk), lambda b, p, i: (b, p, 0, 0, 0)),
        ],
        out_specs=pl.BlockSpec((1, tq, nh * MLA_V_DIM), lambda b, p, i: (b, i, p)),
        out_shape=jax.ShapeDtypeStruct((B, S, MLA_WIDTH), jnp.float32),
        compiler_params=pltpu.CompilerParams(
            dimension_semantics=("parallel", "parallel", "arbitrary"), vmem_limit_bytes=VMEM_LIMIT),
        name="mla",
    )(qt_mla, k_mla, vt_mla)


def _mix_mlp_kernel(x_ref, oa_ref, ob_ref, na_ref, nb_ref, woa_ref, wob_ref, mn_ref, w1_ref, w2_ref,
                    fn_ref, o_ref, x1_ref, hn_ref, acc_ref, *, final):
    f = pl.program_id(1)

    @pl.when(f == 0)
    def _():
        na = _rms(oa_ref[...], na_ref[...]).astype(MXU_DTYPE)
        nb = _rms(ob_ref[...], nb_ref[...]).astype(MXU_DTYPE)
        x1 = x_ref[...] + (_dot(na, woa_ref[...]) + _dot(nb, wob_ref[...]))
        x1_ref[...] = x1
        hn_ref[...] = _rms(x1, mn_ref[...]).astype(MXU_DTYPE)
        acc_ref[...] = jnp.zeros_like(acc_ref)

    a = jnp.square(jnp.maximum(_dot(hn_ref[...], w1_ref[...]), 0.0))
    acc_ref[...] += _dot(a.astype(MXU_DTYPE), w2_ref[...])

    @pl.when(f == pl.num_programs(1) - 1)
    def _():
        y = x1_ref[...] + acc_ref[...]
        o_ref[...] = _rms(y, fn_ref[...]) if final else y


def _mix_mlp_call(x, o_a, o_b, na, nb, wo_a, wo_b, mn, w1, w2, fn, final):
    T, D = x.shape
    rows = min(MLP_ROWS, T)
    tf = MLP_FF_TILE
    row = lambda w: pl.BlockSpec((rows, w), lambda i, f: (i, 0))
    const = lambda a: pl.BlockSpec(a.shape, lambda i, f: (0, 0))
    return pl.pallas_call(
        functools.partial(_mix_mlp_kernel, final=final),
        grid=(T // rows, D_FF // tf),
        in_specs=[row(D), row(NSA_WIDTH), row(MLA_WIDTH), const(na), const(nb), const(wo_a), const(wo_b),
                  const(mn),
                  pl.BlockSpec((D, tf), lambda i, f: (0, f)),
                  pl.BlockSpec((tf, D), lambda i, f: (f, 0)),
                  const(fn)],
        out_specs=row(D),
        out_shape=jax.ShapeDtypeStruct((T, D), jnp.float32),
        scratch_shapes=[pltpu.VMEM((rows, D), jnp.float32), pltpu.VMEM((rows, D), MXU_DTYPE),
                        pltpu.VMEM((rows, D), jnp.float32)],
        compiler_params=pltpu.CompilerParams(
            dimension_semantics=("parallel", "arbitrary"), vmem_limit_bytes=VMEM_LIMIT),
        name="mix_mlp",
    )(x, o_a, o_b, na, nb, wo_a, wo_b, mn, w1, w2, fn)


def _rot_cols(w, dim):
    lead = w.shape[:-1]
    w4 = w.reshape(lead + (w.shape[-1] // dim, 2, dim // 2))
    return jnp.concatenate([-w4[..., 1, :], w4[..., 0, :]], axis=-1).reshape(w.shape)


def _pad_cols(w, left, total):
    pad = [(0, 0)] * (w.ndim - 1) + [(left, total - left - w.shape[-1])]
    return jnp.pad(w, pad)


def _pack_in_proj(w_in):
    sizes = (NSA_WIDTH,) + (NSA_KV_GROUPS * NSA_HEAD_DIM,) * 6 + (
        NSA_BRANCHES * NSA_HEADS, Q_LORA_RANK, KV_LORA_RANK, MLA_ROPE_DIM)
    offs = np.cumsum((0,) + sizes)
    q_a, k_c, v_c, k_s, v_s, k_w, v_w, g_a, c_q, c_kv, k_r = (
        w_in[..., offs[i]:offs[i + 1]] for i in range(len(sizes)))
    roped = jnp.concatenate([k_c, k_s, k_w], axis=-1)
    w_pm = jnp.concatenate(
        [roped, _rot_cols(roped, NSA_HEAD_DIM), v_c, c_q, c_kv,
         _pad_cols(k_r, HALF, LANES), _pad_cols(_rot_cols(k_r, MLA_ROPE_DIM), HALF, LANES)], axis=-1)
    w_fm = jnp.concatenate([q_a, _rot_cols(q_a, NSA_HEAD_DIM), v_s, v_w, _pad_cols(g_a, 0, _GATE_ROWS)], axis=-1)
    assert w_pm.shape[-1] == _PROJ_COLS and w_fm.shape[-1] == _FT_ROWS
    return w_pm.astype(MXU_DTYPE), jnp.swapaxes(w_fm, 1, 2).astype(MXU_DTYPE)


def _pack_q_up(w_q_up):
    L, R, _ = w_q_up.shape
    w = w_q_up.reshape(L, R, MLA_HEADS, MLA_NOPE_DIM + MLA_ROPE_DIM)
    rot = _rot_cols(w[..., MLA_NOPE_DIM:], MLA_ROPE_DIM)
    packed = jnp.concatenate([w.reshape(L, R, -1), rot.reshape(L, R, -1)], axis=-1)
    return jnp.swapaxes(packed, 1, 2).astype(MXU_DTYPE)


def _pack_kv_up(w_kv_up):
    L, R, _ = w_kv_up.shape
    w = w_kv_up.reshape(L, R, MLA_HEADS, MLA_NOPE_DIM + MLA_V_DIM)
    k = _pad_cols(w[..., :MLA_NOPE_DIM], 0, LANES).reshape(L, R, -1)
    v_t = jnp.swapaxes(w[..., MLA_NOPE_DIM:].reshape(L, R, -1), 1, 2)
    return k.astype(MXU_DTYPE), v_t.astype(MXU_DTYPE)


def _pack_compress(pos, w1, w2):
    L = pos.shape[0]
    G, dk, H = NSA_KV_GROUPS, NSA_HEAD_DIM, CMP_HIDDEN
    pos2 = jnp.concatenate([pos] * G, axis=-1).reshape(L, 2, CMP_STRIDE * G * dk)
    on_diag = lambda w, g, axis: jnp.stack([w if g2 == g else jnp.zeros_like(w) for g2 in range(G)], axis=axis)
    w1r = w1.astype(MXU_DTYPE).reshape(L, 2, CMP_STRIDE, dk, H)
    w1b = jnp.stack([on_diag(w1r, g, 4) for g in range(G)], axis=3)
    w1b = w1b.reshape(L, 2, CMP_STRIDE * G * dk, G * H)
    w2p = _pad_cols(w2.astype(MXU_DTYPE), 0, LANES)
    w2b = jnp.stack([on_diag(w2p, g, 2) for g in range(G)], axis=1)
    return pos2, w1b, w2b.reshape(L, G * H, G * LANES)


def _importance_matrix(n_chunk):
    a, b = SEL_BLOCK // CMP_STRIDE, CMP_BLOCK // CMP_STRIDE
    overlap = np.convolve(np.ones(a), np.ones(b))
    n_cmp = n_chunk - (b - 1)
    m = np.zeros((MAX_SEL_BLOCKS, n_chunk), np.float32)
    for blk in range(n_chunk * CMP_STRIDE // SEL_BLOCK):
        for j, wgt in enumerate(overlap):
            i = a * blk + j - (b - 1)
            if 0 <= i < n_cmp:
                m[blk, i] = wgt
    return jnp.asarray(m, MXU_DTYPE)


def _rope_tables(seq, dim):
    inv_freq = 1.0 / (ROPE_THETA ** (jnp.arange(0, dim, 2, dtype=jnp.float32) / dim))
    ang = jnp.arange(seq, dtype=jnp.float32)[:, None] * inv_freq[None, :]
    ang = jnp.concatenate([ang, ang], axis=-1)
    return jnp.cos(ang), jnp.sin(ang)


def kernel(x, attn_norm, w_in, cmp_pos_k, cmp_w1_k, cmp_w2_k, cmp_pos_v, cmp_w1_v, cmp_w2_v, mla_q_norm,
           w_q_up, mla_kv_norm, w_kv_up, nsa_out_norm, mla_out_norm, w_out, mlp_norm, w_ff1, w_ff2, final_norm):
    B, S, D = x.shape
    depth = w_in.shape[0]
    assert S % SEL_BLOCK == 0 and S // SEL_BLOCK <= MAX_SEL_BLOCKS and S % CMP_STRIDE == 0
    n_chunk = S // CMP_STRIDE

    cos64, sin64 = _rope_tables(S, NSA_HEAD_DIM)
    cos_a, sin_a = jnp.concatenate([cos64, cos64], -1), jnp.concatenate([sin64, sin64], -1)
    cos32, sin32 = _rope_tables(S, MLA_ROPE_DIM)
    pad = jnp.zeros((S, LANES - MLA_NOPE_DIM - MLA_ROPE_DIM), jnp.float32)
    cos_b = jnp.concatenate([jnp.ones((S, MLA_NOPE_DIM), jnp.float32), cos32, pad], -1)
    sin_b = jnp.concatenate([jnp.zeros((S, MLA_NOPE_DIM), jnp.float32), sin32, pad], -1)
    tabs = (cos_a, sin_a, cos64.T, sin64.T, cos_b, sin_b, cos32.T, sin32.T)

    w_pm, w_fm = _pack_in_proj(w_in)
    wq2t = _pack_q_up(w_q_up)
    wk2, wv2t = _pack_kv_up(w_kv_up)
    posk, w1k, w2k = _pack_compress(cmp_pos_k, cmp_w1_k, cmp_w2_k)
    posv, w1v, w2v = _pack_compress(cmp_pos_v, cmp_w1_v, cmp_w2_v)
    w2vt = jnp.swapaxes(w2v, 1, 2)
    imp_mat = _importance_matrix(n_chunk)
    wo = w_out.astype(MXU_DTYPE)
    w1 = w_ff1.astype(MXU_DTYPE)
    w2 = w_ff2.astype(MXU_DTYPE)
    row = lambda v: v.reshape(1, -1)

    for l in range(depth):
        (qt_nsa, k_sel, k_win, k_c, v_c, vt_sel, vt_win, gates_t, qt_mla, k_mla, vt_mla) = _proj_call(
            x, row(attn_norm[l]), w_pm[l], w_fm[l], tabs,
            row(mla_q_norm[l]), wq2t[l], row(mla_kv_norm[l]), wk2[l], wv2t[l])
        k_cmp, vt_cmp = _compress_call(
            k_c, v_c, posk[l], posv[l], w1k[l], w1v[l], w2k[l], w2vt[l])
        o_a = _nsa_call(qt_nsa, k_cmp, vt_cmp, k_sel, vt_sel, k_win, vt_win, gates_t, imp_mat)
        o_b = _mla_call(qt_mla, k_mla, vt_mla)
        x = _mix_mlp_call(
            x.reshape(B * S, D), o_a.reshape(B * S, NSA_WIDTH), o_b.reshape(B * S, MLA_WIDTH),
            row(nsa_out_norm[l]), row(mla_out_norm[l]), wo[l, :NSA_WIDTH], wo[l, NSA_WIDTH:],
            row(mlp_norm[l]), w1[l], w2[l], row(final_norm), final=(l == depth - 1)).reshape(B, S, D)
    return x
```
